```python
import jax, jax.numpy as jnp
from jax import lax
import numpy as np

D_MODEL = 2048
BATCH = 32
SEQ = 256
DEPTH = 1
DEC_BATCH = 4
DEC_SEQ = 2048
PAST_LEN = 512

GRID_W = 64
MIX_W = D_MODEL
ML_HEADS = 4
ML_W = MIX_W // 2
ML_DH = ML_W // ML_HEADS
SSM_W = MIX_W - ML_W
SSM_P = 64
SSM_HEADS = SSM_W // SSM_P
SSM_N = 128
SSM_G = 2
SSM_R = SSM_HEADS // SSM_G
CONV_K = 5
CONV_CH = SSM_W + 2 * SSM_G * SSM_N
D_FF = 256 * ((8 * D_MODEL // 3 + 255) // 256)
CHUNK = 128
N_MOD = 9
EPS = 1e-6
COL_SIZES = (ML_W, ML_W, ML_W, ML_W, 2 * ML_HEADS, 2 * ML_HEADS, SSM_W, CONV_CH, 2 * SSM_HEADS)
COL_SPLITS = tuple(sum(COL_SIZES[:i + 1]) for i in range(len(COL_SIZES) - 1))
IN_W = sum(COL_SIZES)

kernel_name = "hybrid_mlstm_ssd_diffusion_step"


def rmsnorm(x, g):
    x32 = x.astype(jnp.float32)
    y = x32 * lax.rsqrt(jnp.mean(x32 * x32, axis=-1, keepdims=True) + EPS)
    return (y * g.astype(jnp.float32)).astype(x.dtype)


def flip(a):
    return jnp.flip(a, axis=1)


def to_chunks(a):
    b, l = a.shape[:2]
    return jnp.moveaxis(a.reshape((b, l // CHUNK, CHUNK) + a.shape[2:]), 1, 0)


def from_chunks(a):
    a = jnp.moveaxis(a, 0, 1)
    return a.reshape((a.shape[0], a.shape[1] * a.shape[2]) + a.shape[3:])


def mlstm_direction(q, k, v, log_i, log_f, C0, n0, m0):
    tri = jnp.tril(jnp.ones((CHUNK, CHUNK), dtype=bool))

    def body(carry, inp):
        C, n, m = carry
        qc, kc, vc, li, lf = inp
        b = jnp.cumsum(lf, axis=1)
        log_d = b[:, :, None, :] - b[:, None, :, :] + li[:, None, :, :]
        log_d = jnp.where(tri[None, :, :, None], log_d, -jnp.inf)
        m_inter = b + m[:, None, :]
        m_t = jnp.maximum(m_inter, jnp.max(log_d, axis=2))
        scores = jnp.einsum('bthd,bshd->btsh', qc, kc) * jnp.exp(log_d - m_t[:, :, None, :])
        inter = jnp.exp(m_inter - m_t)
        num = (jnp.einsum('btsh,bshv->bthv', scores, vc)
               + inter[..., None] * jnp.einsum('bthk,bhkv->bthv', qc, C))
        den = jnp.sum(scores, axis=2) + inter * jnp.einsum('bthk,bhk->bth', qc, n)
        h = num / jnp.maximum(jnp.abs(den), jnp.exp(-m_t))[..., None]
        log_w = b[:, -1:, :] - b + li
        m_end_inter = b[:, -1, :] + m
        m_new = jnp.maximum(m_end_inter, jnp.max(log_w, axis=1))
        w = jnp.exp(log_w - m_new[:, None, :])
        decay = jnp.exp(m_end_inter - m_new)
        C_new = decay[..., None, None] * C + jnp.einsum('bshk,bshv->bhkv', w[..., None] * kc, vc)
        n_new = decay[..., None] * n + jnp.einsum('bsh,bshk->bhk', w, kc)
        return (C_new, n_new, m_new), h

    xs = (to_chunks(q), to_chunks(k), to_chunks(v), to_chunks(log_i), to_chunks(log_f))
    (C, n, m), h = lax.scan(body, (C0, n0, m0), xs)
    return from_chunks(h), C, n, m


def ssd_direction(x, bm, cm, dt, A, S0):
    tri = jnp.tril(jnp.ones((CHUNK, CHUNK), dtype=bool))

    def body(S, inp):
        xc, bc, cc, dtc = inp
        a = jnp.cumsum(dtc * A, axis=1)
        seg = a[:, :, None] - a[:, None]
        decay = jnp.exp(jnp.where(tri[None, :, :, None, None], seg, -jnp.inf))
        cb = jnp.einsum('btgn,bsgn->btsg', cc, bc)
        wts = cb[..., None] * decay * dtc[:, None]
        y = (jnp.einsum('btsgr,bsgrp->btgrp', wts, xc)
             + jnp.exp(a)[..., None] * jnp.einsum('btgn,bgrpn->btgrp', cc, S))
        to_end = jnp.exp(a[:, -1:] - a) * dtc
        S_new = (jnp.exp(a[:, -1])[..., None, None] * S
                 + jnp.einsum('bsgrp,bsgn->bgrpn', to_end[..., None] * xc, bc))
        return S_new, y

    S, y = lax.scan(body, S0, (to_chunks(x), to_chunks(bm), to_chunks(cm), to_chunks(dt)))
    return from_chunks(y), S


def dwconv(u, w, bias, n_seg):
    b, l, ch = u.shape
    seg = u.reshape(b * n_seg, l // n_seg, ch)
    out = lax.conv_general_dilated(seg, w[:, None, :].astype(u.dtype), window_strides=(1,),
                                   padding=[(CONV_K // 2, CONV_K // 2)],
                                   dimension_numbers=('NWC', 'WIO', 'NWC'), feature_group_count=ch)
    return (out + bias.astype(u.dtype)).reshape(b, l, ch)


def hybrid_mixer(h, p, ml_state, ssm_state, n_seg):
    f32 = jnp.float32
    bsz, l, _ = h.shape
    proj = h @ p['w_in']
    q, k, v, o_pre, ig, fg, z, xbc, dt_raw = jnp.split(proj, COL_SPLITS, axis=-1)
    heads = lambda a: a.reshape(bsz, l, ML_HEADS, ML_DH).astype(f32)
    q, k, v = heads(q), heads(k) * (ML_DH ** -0.5), heads(v)
    gb = p['gate_bias'].astype(f32)
    log_i = ig.reshape(bsz, l, 2, ML_HEADS).astype(f32) + gb[0]
    log_f = jax.nn.log_sigmoid(fg.reshape(bsz, l, 2, ML_HEADS).astype(f32) + gb[1])
    C0, n0, m0 = (s.astype(f32) for s in ml_state)
    h_f, Cf, nf, mf = mlstm_direction(q, k, v, log_i[:, :, 0], log_f[:, :, 0], C0[:, 0], n0[:, 0], m0[:, 0])
    h_b, Cb, nb, mb = mlstm_direction(flip(q), flip(k), flip(v), flip(log_i[:, :, 1]), flip(log_f[:, :, 1]),
                                      C0[:, 1], n0[:, 1], m0[:, 1])
    h_ml = h_f + flip(h_b)
    mu = jnp.mean(h_ml, axis=-1, keepdims=True)
    var = jnp.mean(jnp.square(h_ml - mu), axis=-1, keepdims=True)
    h_ml = (h_ml - mu) * lax.rsqrt(var + EPS) * p['ml_norm_g'].astype(f32).reshape(ML_HEADS, ML_DH)
    h_ml = jax.nn.sigmoid(o_pre.astype(f32)) * h_ml.reshape(bsz, l, ML_W)
    xbc = jax.nn.silu(dwconv(xbc, p['conv_w'], p['conv_b'], n_seg).astype(f32))
    xs, bm, cm = jnp.split(xbc, (SSM_W, SSM_W + SSM_G * SSM_N), axis=-1)
    xs = xs.reshape(bsz, l, SSM_G, SSM_R, SSM_P)
    bm = bm.reshape(bsz, l, SSM_G, SSM_N)
    cm = cm.reshape(bsz, l, SSM_G, SSM_N)
    dt = jax.nn.softplus(dt_raw.reshape(bsz, l, 2, SSM_HEADS).astype(f32) + p['dt_bias'].astype(f32))
    dt = dt.reshape(bsz, l, 2, SSM_G, SSM_R)
    A = (-jnp.exp(p['a_log'].astype(f32))).reshape(2, SSM_G, SSM_R)
    S0 = ssm_state.astype(f32).reshape(bsz, 2, SSM_G, SSM_R, SSM_P, SSM_N)
    y_f, Sf = ssd_direction(xs, bm, cm, dt[:, :, 0], A[0], S0[:, 0])
    y_b, Sb = ssd_direction(flip(xs), flip(bm), flip(cm), flip(dt[:, :, 1]), A[1], S0[:, 1])
    y = y_f + flip(y_b) + p['d_skip'].astype(f32).reshape(SSM_G, SSM_R)[..., None] * xs
    y = rmsnorm(y.reshape(bsz, l, SSM_W) * jax.nn.silu(z.astype(f32)), p['ssm_norm_g'])
    out = jnp.concatenate([h_ml, y], axis=-1).astype(h.dtype) @ p['w_out']
    new_ml = (jnp.stack([Cf, Cb], axis=1), jnp.stack([nf, nb], axis=1), jnp.stack([mf, mb], axis=1))
    new_ssm = jnp.stack([Sf, Sb], axis=1).reshape(bsz, 2, SSM_HEADS, SSM_P, SSM_N)
    return out, new_ml, new_ssm


def swiglu(h, wg, wu, wd):
    return (jax.nn.silu(h @ wg) * (h @ wu)) @ wd


def adaln_in(x, g, shift, scale):
    return rmsnorm(x, g) * (1 + scale) + shift


def trunk_layer(x, mod, p, ml_state, ssm_state, n_seg):
    mod = mod.astype(x.dtype)
    sh = lambda j: mod[:, j, None, :]
    g = p['norm_g']
    h = adaln_in(x, g[0], sh(0), sh(1))
    x = x + 0.5 * sh(2) * rmsnorm(swiglu(h, p['w_gate'][0], p['w_up'][0], p['w_down'][0]), g[1])
    h = adaln_in(x, g[2], sh(3), sh(4))
    mix, new_ml, new_ssm = hybrid_mixer(h, p, ml_state, ssm_state, n_seg)
    x = x + sh(5) * rmsnorm(mix, g[3])
    h = adaln_in(x, g[4], sh(6), sh(7))
    x = x + 0.5 * sh(8) * rmsnorm(swiglu(h, p['w_gate'][1], p['w_up'][1], p['w_down'][1]), g[5])
    return x, new_ml, new_ssm


def setup_inputs(seed: int = 0) -> dict:
    key = jax.random.key(seed)
    ks = jax.random.split(key, 26)
    nrm = lambda k, shape, s: jax.random.normal(k, shape, jnp.float32) * s
    dt0 = jnp.exp(jax.random.uniform(ks[13], (DEPTH, 2, SSM_HEADS), jnp.float32, np.log(1e-3), np.log(1e-1)))
    gate_bias = nrm(ks[12], (DEPTH, 2, 2, ML_HEADS), 0.1)
    gate_bias = gate_bias.at[:, 1].add(jnp.linspace(3.0, 6.0, ML_HEADS, dtype=jnp.float32))
    return {
        'x_prompt': nrm(ks[0], (BATCH, SEQ, D_MODEL), 1.0),
        'x_sample': nrm(ks[1], (DEC_BATCH, DEC_SEQ, D_MODEL), 1.0),
        'state_mlstm_C': nrm(ks[2], (DEC_BATCH, DEPTH, 2, ML_HEADS, ML_DH, ML_DH), 0.5),
        'state_mlstm_n': nrm(ks[3], (DEC_BATCH, DEPTH, 2, ML_HEADS, ML_DH), 0.5),
        'state_mlstm_m': nrm(ks[4], (DEC_BATCH, DEPTH, 2, ML_HEADS), 1.0),
        'state_ssm': nrm(ks[5], (DEC_BATCH, DEPTH, 2, SSM_HEADS, SSM_P, SSM_N), 0.5),
        'c': nrm(ks[6], (DEC_BATCH, D_MODEL), 1.0),
        'c_ctx': nrm(ks[7], (D_MODEL,), 1.0),
        'w_ada': nrm(ks[8], (DEPTH, D_MODEL, N_MOD * D_MODEL), D_MODEL ** -0.5),
        'b_ada': nrm(ks[9], (DEPTH, N_MOD * D_MODEL), 0.02),
        'norm_g': 1.0 + nrm(ks[10], (DEPTH, 6, D_MODEL), 0.05),
        'w_in': nrm(ks[11], (DEPTH, D_MODEL, IN_W), D_MODEL ** -0.5),
        'gate_bias': gate_bias,
        'dt_bias': dt0 + jnp.log(-jnp.expm1(-dt0)),
        'a_log': jnp.log(jax.random.uniform(ks[14], (DEPTH, 2, SSM_HEADS), jnp.float32, 1.0, 16.0)),
        'd_skip': 1.0 + nrm(ks[15], (DEPTH, SSM_HEADS), 0.1),
        'conv_w': nrm(ks[16], (DEPTH, CONV_K, CONV_CH), CONV_K ** -0.5),
        'conv_b': nrm(ks[17], (DEPTH, CONV_CH), 0.02),
        'ml_norm_g': 1.0 + nrm(ks[18], (DEPTH, ML_W), 0.05),
        'ssm_norm_g': 1.0 + nrm(ks[19], (DEPTH, SSM_W), 0.05),
        'w_out': nrm(ks[20], (DEPTH, MIX_W, D_MODEL), MIX_W ** -0.5),
        'ffn_w_gate': nrm(ks[21], (DEPTH, 2, D_MODEL, D_FF), D_MODEL ** -0.5),
        'ffn_w_up': nrm(ks[22], (DEPTH, 2, D_MODEL, D_FF), D_MODEL ** -0.5),
        'ffn_w_down': nrm(ks[23], (DEPTH, 2, D_FF, D_MODEL), D_FF ** -0.5),
    }


def reference(x_prompt, x_sample, state_mlstm_C, state_mlstm_n, state_mlstm_m, state_ssm, c, c_ctx,
              w_ada, b_ada, norm_g, w_in, gate_bias, dt_bias, a_log, d_skip, conv_w, conv_b,
              ml_norm_g, ssm_norm_g, w_out, ffn_w_gate, ffn_w_up, ffn_w_down):
    f32 = jnp.float32
    bp = x_prompt.shape[0]
    bd = x_sample.shape[0]
    rows = x_sample.shape[1] // GRID_W
    y_p, y_s = x_prompt, x_sample
    out_C, out_n, out_m, out_S = [], [], [], []
    for l in range(DEPTH):
        p = {'norm_g': norm_g[l], 'w_in': w_in[l], 'gate_bias': gate_bias[l], 'dt_bias': dt_bias[l],
             'a_log': a_log[l], 'd_skip': d_skip[l], 'conv_w': conv_w[l], 'conv_b': conv_b[l],
             'ml_norm_g': ml_norm_g[l], 'ssm_norm_g': ssm_norm_g[l], 'w_out': w_out[l],
             'w_gate': ffn_w_gate[l], 'w_up': ffn_w_up[l], 'w_down': ffn_w_down[l]}
        mod_ctx = (jax.nn.silu(c_ctx)[None] @ w_ada[l] + b_ada[l]).reshape(1, N_MOD, D_MODEL)
        mod_lat = (jax.nn.silu(c) @ w_ada[l] + b_ada[l]).reshape(bd, N_MOD, D_MODEL)
        zero_ml = (jnp.zeros((bp, 2, ML_HEADS, ML_DH, ML_DH), f32), jnp.zeros((bp, 2, ML_HEADS, ML_DH), f32),
                   jnp.zeros((bp, 2, ML_HEADS), f32))
        zero_ssm = jnp.zeros((bp, 2, SSM_HEADS, SSM_P, SSM_N), f32)
        y_p, (Cn, nn_, mn), Sn = trunk_layer(y_p, mod_ctx, p, zero_ml, zero_ssm, 1)
        lat_ml = (state_mlstm_C[:, l], state_mlstm_n[:, l], state_mlstm_m[:, l])
        y_s, _, _ = trunk_layer(y_s, mod_lat, p, lat_ml, state_ssm[:, l], rows)
        out_C.append(Cn)
        out_n.append(nn_)
        out_m.append(mn)
        out_S.append(Sn)
    new_mlstm_C = jnp.stack(out_C, axis=1)
    new_mlstm_n = jnp.stack(out_n, axis=1)
    new_mlstm_m = jnp.stack(out_m, axis=1)
    new_ssm = jnp.stack(out_S, axis=1)
    return (y_p, y_s, new_mlstm_C, new_mlstm_n, new_mlstm_m, new_ssm)
```

```python
import functools

import jax
import jax.numpy as jnp
from jax import lax
from jax.experimental import pallas as pl
from jax.experimental.pallas import tpu as pltpu

F32 = jnp.float32
BF16 = jnp.bfloat16
HIGHEST = lax.Precision.HIGHEST

D_MODEL = 2048
GRID_W = 64
ML_HEADS = 4
ML_W = 1024
ML_DH = ML_W // ML_HEADS
SSM_W = 1024
SSM_P = 64
SSM_HEADS = SSM_W // SSM_P
SSM_N = 128
SSM_G = 2
SSM_R = SSM_HEADS // SSM_G
CONV_K = 5
CONV_CH = SSM_W + 2 * SSM_G * SSM_N
D_FF = 5632
CHUNK = 128
N_MOD = 9
EPS = 1e-6

MAIN_W = 4 * ML_W + SSM_W + CONV_CH
COL_Q, COL_K, COL_V, COL_O, COL_Z = 0, 1, 2, 3, 4
CONV_TILE = 512
COL_XBC = (4 * ML_W + SSM_W) // CONV_TILE
SMALL_W = 128
LANE_IG, LANE_FG, LANE_DT = 0, 2 * ML_HEADS, 4 * ML_HEADS

VMEM_LIMIT = 48 * 1024 * 1024
ROW_CHUNK = 128


def _cparams(n_axes, vmem=VMEM_LIMIT):
    return pltpu.CompilerParams(dimension_semantics=("arbitrary",) * n_axes, vmem_limit_bytes=vmem)


def _dot(a, b):
    return jnp.dot(a, b, preferred_element_type=F32)


def _dot_nt(a, b):
    return lax.dot_general(a, b, (((1,), (1,)), ((), ())), preferred_element_type=F32)


def _dot_tn(a, b):
    return lax.dot_general(a, b, (((0,), (0,)), ((), ())), preferred_element_type=F32)


def _dot_exact(a, b):
    return jnp.dot(a, b, preferred_element_type=F32, precision=HIGHEST)


def _silu(x):
    return x * jax.nn.sigmoid(x)


def _softplus(x):
    return jnp.maximum(x, 0.0) + jnp.log1p(jnp.exp(-jnp.abs(x)))


def _log_sigmoid(x):
    return jnp.minimum(x, 0.0) - jnp.log1p(jnp.exp(-jnp.abs(x)))


def _rms(x):
    return lax.rsqrt(jnp.mean(x * x, axis=-1, keepdims=True) + EPS)


def _adaln_rows(x_ref, h_ref, g_ref, mod_ref, mod_base):
    def body(i, carry):
        r0 = pl.multiple_of(i * ROW_CHUNK, ROW_CHUNK)
        x = x_ref[pl.ds(r0, ROW_CHUNK), :]
        shift = mod_ref[mod_base:mod_base + 1, :]
        scale = mod_ref[mod_base + 1:mod_base + 2, :]
        h = (x * _rms(x)) * g_ref[...] * (1.0 + scale) + shift
        h_ref[pl.ds(r0, ROW_CHUNK), :] = h.astype(BF16)
        return carry
    lax.fori_loop(0, x_ref.shape[0] // ROW_CHUNK, body, 0)


def _residual_rows(x_ref, y_ref, o_ref, g_ref, mod_ref, gate_row, gate_mul):
    def body(i, carry):
        r0 = pl.multiple_of(i * ROW_CHUNK, ROW_CHUNK)
        y = y_ref[pl.ds(r0, ROW_CHUNK), :]
        gate = mod_ref[gate_row:gate_row + 1, :]
        upd = gate_mul * gate * ((y * _rms(y)) * g_ref[...])
        o_ref[pl.ds(r0, ROW_CHUNK), :] = x_ref[pl.ds(r0, ROW_CHUNK), :] + upd
        return carry
    lax.fori_loop(0, x_ref.shape[0] // ROW_CHUNK, body, 0)


def _ada_kernel(c_ref, w_ref, b_ref, o_ref):
    s = _silu(c_ref[...]).astype(BF16)
    o_ref[...] = _dot(s, w_ref[...].astype(BF16)) + b_ref[...]


def _ada(cv, w, b):
    n = w.shape[1]
    tn = 1024
    return pl.pallas_call(
        _ada_kernel,
        grid=(n // tn,),
        in_specs=[pl.BlockSpec((cv.shape[0], D_MODEL), lambda j: (0, 0)),
                  pl.BlockSpec((D_MODEL, tn), lambda j: (0, j)),
                  pl.BlockSpec((1, tn), lambda j: (0, j))],
        out_specs=pl.BlockSpec((cv.shape[0], tn), lambda j: (0, j)),
        out_shape=jax.ShapeDtypeStruct((cv.shape[0], n), F32),
        compiler_params=_cparams(1),
        name="ada_mod",
    )(cv, w, b)


def _ffn_kernel(x_ref, mod_ref, gin_ref, gout_ref, wg_ref, wu_ref, wd_ref, o_ref, h_ref, *, mod_base):
    j = pl.program_id(1)

    @pl.when(j == 0)
    def _():
        _adaln_rows(x_ref, h_ref, gin_ref, mod_ref, mod_base)
        o_ref[...] = jnp.zeros_like(o_ref)

    h = h_ref[...]
    a = (_silu(_dot(h, wg_ref[...])) * _dot(h, wu_ref[...])).astype(BF16)
    o_ref[...] += _dot(a, wd_ref[...])

    @pl.when(j == pl.num_programs(1) - 1)
    def _():
        _residual_rows(x_ref, o_ref, o_ref, gout_ref, mod_ref, mod_base + 2, 0.5)


def _ffn(x, mod, g_in, g_out, wg, wu, wd, *, mod_base, tm, tf):
    t = x.shape[0]
    rows_per_mod = t // mod.shape[0]
    return pl.pallas_call(
        functools.partial(_ffn_kernel, mod_base=mod_base),
        grid=(t // tm, D_FF // tf),
        in_specs=[pl.BlockSpec((tm, D_MODEL), lambda i, j: (i, 0)),
                  pl.BlockSpec((None, N_MOD, D_MODEL), lambda i, j: ((i * tm) // rows_per_mod, 0, 0)),
                  pl.BlockSpec((1, D_MODEL), lambda i, j: (0, 0)),
                  pl.BlockSpec((1, D_MODEL), lambda i, j: (0, 0)),
                  pl.BlockSpec((D_MODEL, tf), lambda i, j: (0, j)),
                  pl.BlockSpec((D_MODEL, tf), lambda i, j: (0, j)),
                  pl.BlockSpec((tf, D_MODEL), lambda i, j: (j, 0))],
        out_specs=pl.BlockSpec((tm, D_MODEL), lambda i, j: (i, 0)),
        out_shape=jax.ShapeDtypeStruct((t, D_MODEL), F32),
        scratch_shapes=[pltpu.VMEM((tm, D_MODEL), BF16)],
        compiler_params=_cparams(2),
        name="ffn",
    )(x, mod, g_in, g_out, wg, wu, wd)


def _inproj_kernel(x_ref, mod_ref, g_ref, w_ref, ws_ref, o_ref, os_ref, h_ref):
    j = pl.program_id(1)

    @pl.when(j == 0)
    def _():
        _adaln_rows(x_ref, h_ref, g_ref, mod_ref, 3)
        os_ref[...] = _dot(h_ref[...], ws_ref[...])

    o_ref[...] = _dot(h_ref[...], w_ref[...]).astype(BF16)


def _inproj(x, mod, g, w_main, w_small, *, tm, tn):
    t = x.shape[0]
    rows_per_mod = t // mod.shape[0]
    return pl.pallas_call(
        _inproj_kernel,
        grid=(t // tm, MAIN_W // tn),
        in_specs=[pl.BlockSpec((tm, D_MODEL), lambda i, j: (i, 0)),
                  pl.BlockSpec((None, N_MOD, D_MODEL), lambda i, j: ((i * tm) // rows_per_mod, 0, 0)),
                  pl.BlockSpec((1, D_MODEL), lambda i, j: (0, 0)),
                  pl.BlockSpec((D_MODEL, tn), lambda i, j: (0, j)),
                  pl.BlockSpec((D_MODEL, SMALL_W), lambda i, j: (0, 0))],
        out_specs=[pl.BlockSpec((tm, tn), lambda i, j: (i, j)),
                   pl.BlockSpec((tm, SMALL_W), lambda i, j: (i, 0))],
        out_shape=[jax.ShapeDtypeStruct((t, MAIN_W), BF16),
                   jax.ShapeDtypeStruct((t, SMALL_W), F32)],
        scratch_shapes=[pltpu.VMEM((tm, D_MODEL), BF16)],
        compiler_params=_cparams(2),
        name="inproj",
    )(x, mod, g, w_main, w_small)


def _conv_kernel(u_ref, w_ref, b_ref, o_ref, *, seg_len):
    u = u_ref[...].astype(F32)
    rows = u.shape[0]
    pos = lax.broadcasted_iota(jnp.int32, u.shape, 0) % seg_len
    acc = jnp.zeros_like(u) + b_ref[...]
    for j in range(CONV_K):
        off = j - CONV_K // 2
        shifted = u if off == 0 else pltpu.roll(u, (-off) % rows, 0)
        valid = jnp.logical_and(pos + off >= 0, pos + off < seg_len)
        acc = acc + jnp.where(valid, shifted, 0.0) * w_ref[j:j + 1, :]
    o_ref[...] = _silu(acc).astype(BF16)


def _conv(p_main, conv_w, conv_b, *, seg_len, tile0, n_tiles, rows):
    t = p_main.shape[0]
    return pl.pallas_call(
        functools.partial(_conv_kernel, seg_len=seg_len),
        grid=(t // rows, n_tiles),
        in_specs=[pl.BlockSpec((rows, CONV_TILE), lambda i, j: (i, COL_XBC + tile0 + j)),
                  pl.BlockSpec((CONV_K, CONV_TILE), lambda i, j: (0, tile0 + j)),
                  pl.BlockSpec((1, CONV_TILE), lambda i, j: (0, tile0 + j))],
        out_specs=pl.BlockSpec((rows, CONV_TILE), lambda i, j: (i, j)),
        out_shape=jax.ShapeDtypeStruct((t, n_tiles * CONV_TILE), BF16),
        compiler_params=_cparams(2),
        name="dwconv",
    )(p_main, conv_w, conv_b)


def _tri_mask(direction):
    row = lax.broadcasted_iota(jnp.int32, (CHUNK, CHUNK), 0)
    col = lax.broadcasted_iota(jnp.int32, (CHUNK, CHUNK), 1)
    return (col <= row) if direction == 0 else (col >= row)


def _chunk_index(c, nc, direction):
    return c if direction == 0 else nc - 1 - c


def _mlstm_kernel(*refs, direction, zero_init, final):
    refs = list(refs)
    q_ref, k_ref, v_ref, sm_ref, gb_ref = refs[:5]
    pos = 5
    if not zero_init:
        c0_ref, n0_ref, m0_ref = refs[pos:pos + 3]
        pos += 3
    if final:
        hf_ref, og_ref, mlg_ref = refs[pos:pos + 3]
        pos += 3
    h_out_ref, c_out_ref, n_out_ref, m_out_ref, c_s, n_s, m_s = refs[pos:]

    c = pl.program_id(1)

    @pl.when(c == 0)
    def _():
        if zero_init:
            c_s[...] = jnp.zeros_like(c_s)
            n_s[...] = jnp.zeros_like(n_s)
            m_s[...] = jnp.zeros_like(m_s)
        else:
            c_s[...] = c0_ref[...]
            n_s[...] = n0_ref[...]
            m_s[...] = m0_ref[...]

    tri = _tri_mask(direction)
    lane = lax.broadcasted_iota(jnp.int32, (CHUNK, SMALL_W), 1)
    g = sm_ref[...] + gb_ref[...]
    is_fg = jnp.logical_and(lane >= LANE_FG, lane < LANE_DT)
    p = jnp.where(is_fg, _log_sigmoid(g), g)
    cum = _dot_exact(tri.astype(F32), p)
    p_t = p.T
    cum_t = cum.T
    last = CHUNK - 1 if direction == 0 else 0

    for h in range(ML_HEADS):
        li_lane = LANE_IG + direction * ML_HEADS + h
        lf_lane = LANE_FG + direction * ML_HEADS + h
        b_col = cum[:, lf_lane:lf_lane + 1]
        b_row = cum_t[lf_lane:lf_lane + 1, :]
        li_col = p[:, li_lane:li_lane + 1]
        li_row = p_t[li_lane:li_lane + 1, :]
        b_last = cum[last:last + 1, lf_lane:lf_lane + 1]
        m_prev = m_s[h:h + 1, 0:1]

        log_d = jnp.where(tri, b_col - b_row + li_row, -jnp.inf)
        m_inter = b_col + m_prev
        m_t = jnp.maximum(m_inter, jnp.max(log_d, axis=1, keepdims=True))
        d = jnp.exp(log_d - m_t)

        sl = slice(h * ML_DH, (h + 1) * ML_DH)
        qh = q_ref[:, sl]
        kh = (k_ref[:, sl].astype(F32) * (ML_DH ** -0.5))
        kh_b = kh.astype(BF16)
        vh = v_ref[:, sl]
        scores = _dot_nt(qh, kh_b) * d
        inter = jnp.exp(m_inter - m_t)
        c_old = c_s[h]
        n_old = n_s[h:h + 1, :]
        num = _dot(scores.astype(BF16), vh) + inter * _dot(qh, c_old.astype(BF16))
        qn = jnp.sum(qh.astype(F32) * n_old, axis=1, keepdims=True)
        den = jnp.sum(scores, axis=1, keepdims=True) + inter * qn
        hh = num / jnp.maximum(jnp.abs(den), jnp.exp(-m_t))

        log_w = b_last - b_col + li_col
        m_end = b_last + m_prev
        m_new = jnp.maximum(m_end, jnp.max(log_w, axis=0, keepdims=True))
        w = jnp.exp(log_w - m_new)
        decay = jnp.exp(m_end - m_new)
        kw = kh * w
        c_s[h] = decay * c_old + _dot_tn(kw.astype(BF16), vh)
        n_s[h:h + 1, :] = decay * n_old + jnp.sum(kw, axis=0, keepdims=True)
        m_s[h:h + 1, :] = jnp.broadcast_to(m_new, (1, SMALL_W))

        if final:
            hs = hf_ref[:, sl] + hh
            mu = jnp.mean(hs, axis=1, keepdims=True)
            cen = hs - mu
            var = jnp.mean(cen * cen, axis=1, keepdims=True)
            hn = cen * lax.rsqrt(var + EPS) * mlg_ref[:, sl]
            h_out_ref[:, sl] = (jax.nn.sigmoid(og_ref[:, sl].astype(F32)) * hn).astype(h_out_ref.dtype)
        else:
            h_out_ref[:, sl] = hh

    @pl.when(c == pl.num_programs(1) - 1)
    def _():
        c_out_ref[...] = c_s[...]
        n_out_ref[...] = n_s[...]
        m_out_ref[...] = m_s[...]


def _mlstm(p_main, small, gb_row, state, hf, mlg, *, bsz, seq, direction):
    nc = seq // CHUNK
    t = bsz * seq
    zero_init = state is None
    final = hf is not None

    def rows(col):
        return lambda b, c: (b * nc + _chunk_index(c, nc, direction), col)

    in_specs = [pl.BlockSpec((CHUNK, ML_W), rows(COL_Q)),
                pl.BlockSpec((CHUNK, ML_W), rows(COL_K)),
                pl.BlockSpec((CHUNK, ML_W), rows(COL_V)),
                pl.BlockSpec((CHUNK, SMALL_W), rows(0)),
                pl.BlockSpec((1, SMALL_W), lambda b, c: (0, 0))]
    args = [p_main, p_main, p_main, small, gb_row]
    if not zero_init:
        c0, n0, m0 = state
        in_specs += [pl.BlockSpec((None, None, ML_HEADS, ML_DH, ML_DH), lambda b, c: (b, direction, 0, 0, 0)),
                     pl.BlockSpec((None, None, ML_HEADS, ML_DH), lambda b, c: (b, direction, 0, 0)),
                     pl.BlockSpec((None, None, ML_HEADS, SMALL_W), lambda b, c: (b, direction, 0, 0))]
        args += [c0, n0, m0]
    if final:
        in_specs += [pl.BlockSpec((CHUNK, ML_W), rows(0)),
                     pl.BlockSpec((CHUNK, ML_W), rows(COL_O)),
                     pl.BlockSpec((1, ML_W), lambda b, c: (0, 0))]
        args += [hf, p_main, mlg]
    out_specs = [pl.BlockSpec((CHUNK, ML_W), rows(0)),
                 pl.BlockSpec((None, ML_HEADS, ML_DH, ML_DH), lambda b, c: (b, 0, 0, 0)),
                 pl.BlockSpec((None, ML_HEADS, ML_DH), lambda b, c: (b, 0, 0)),
                 pl.BlockSpec((None, ML_HEADS, SMALL_W), lambda b, c: (b, 0, 0))]
    out_shape = [jax.ShapeDtypeStruct((t, ML_W), BF16 if final else F32),
                 jax.ShapeDtypeStruct((bsz, ML_HEADS, ML_DH, ML_DH), F32),
                 jax.ShapeDtypeStruct((bsz, ML_HEADS, ML_DH), F32),
                 jax.ShapeDtypeStruct((bsz, ML_HEADS, SMALL_W), F32)]
    return pl.pallas_call(
        functools.partial(_mlstm_kernel, direction=direction, zero_init=zero_init, final=final),
        grid=(bsz, nc),
        in_specs=in_specs,
        out_specs=out_specs,
        out_shape=out_shape,
        scratch_shapes=[pltpu.VMEM((ML_HEADS, ML_DH, ML_DH), F32),
                        pltpu.VMEM((ML_HEADS, ML_DH), F32),
                        pltpu.VMEM((ML_HEADS, SMALL_W), F32)],
        compiler_params=_cparams(2),
        name="mlstm_bwd" if direction else "mlstm_fwd",
    )(*args)


def _ssd_kernel(*refs, direction, zero_init, final):
    refs = list(refs)
    xs_ref, bc_ref, sm_ref, gb_ref, arow_ref = refs[:5]
    pos = 5
    if not zero_init:
        s0_ref = refs[pos]
        pos += 1
    if final:
        yf_ref, z_ref, dsk_ref, ng_ref = refs[pos:pos + 4]
        pos += 4
    y_out_ref, s_out_ref, st_s = refs[pos:]

    c = pl.program_id(1)

    @pl.when(c == 0)
    def _():
        if zero_init:
            st_s[...] = jnp.zeros_like(st_s)
        else:
            st_s[...] = s0_ref[...].T

    tri = _tri_mask(direction)
    lane = lax.broadcasted_iota(jnp.int32, (CHUNK, SMALL_W), 1)
    g = sm_ref[...] + gb_ref[...]
    is_dt = jnp.logical_and(lane >= LANE_DT, lane < LANE_DT + 2 * SSM_HEADS)
    p = jnp.where(is_dt, _softplus(g), g)
    cum = _dot_exact(tri.astype(F32), p * arow_ref[...])
    p_t = p.T
    cum_t = cum.T
    last = CHUNK - 1 if direction == 0 else 0
    base = LANE_DT + direction * SSM_HEADS

    a16 = cum[:, base:base + SSM_HEADS]
    dt16 = p[:, base:base + SSM_HEADS]
    a_last = cum[last:last + 1, base:base + SSM_HEADS]
    to_end = jnp.exp(a_last - a16) * dt16
    ea16 = jnp.exp(a16)
    expand = (lax.broadcasted_iota(jnp.int32, (SSM_HEADS, SSM_W), 1) // SSM_P
              == lax.broadcasted_iota(jnp.int32, (SSM_HEADS, SSM_W), 0)).astype(F32)
    to_end_x = _dot_exact(to_end, expand)
    ea_last_x = _dot_exact(jnp.broadcast_to(jnp.exp(a_last), (8, SSM_HEADS)), expand)[0:1, :]

    lane_p = lax.broadcasted_iota(jnp.int32, (CHUNK, 2 * SSM_P), 1)
    lo = lane_p < SSM_P
    zero_b = jnp.zeros((CHUNK, 2 * SSM_P), BF16)

    y_pairs = []
    for gi in range(SSM_G):
        bg = bc_ref[:, gi * SSM_N:(gi + 1) * SSM_N]
        cg = bc_ref[:, (SSM_G + gi) * SSM_N:(SSM_G + gi + 1) * SSM_N]
        cb = _dot_nt(cg, bg)
        cg32 = cg.astype(F32)
        for pr in range(SSM_R // 2):
            h0 = gi * SSM_R + 2 * pr
            lhs = []
            for hh in (h0, h0 + 1):
                a_col = a16[:, hh:hh + 1]
                a_row = cum_t[base + hh:base + hh + 1, :]
                dt_row = p_t[base + hh:base + hh + 1, :]
                decay = jnp.exp(jnp.where(tri, a_col - a_row, -jnp.inf))
                lhs.append((cb * decay * dt_row).astype(BF16))
            for hh in (h0, h0 + 1):
                lhs.append((cg32 * ea16[:, hh:hh + 1]).astype(BF16))
            cols = slice(h0 * SSM_P, (h0 + 2) * SSM_P)
            xp = xs_ref[:, cols]
            sp = st_s[:, cols].astype(BF16)
            rhs = jnp.concatenate([jnp.where(lo, xp, zero_b), jnp.where(lo, zero_b, xp),
                                   jnp.where(lo, sp, zero_b), jnp.where(lo, zero_b, sp)], axis=0)
            y_pairs.append(_dot(jnp.concatenate(lhs, axis=1), rhs))

        gcols = slice(gi * SSM_R * SSM_P, (gi + 1) * SSM_R * SSM_P)
        xw = (xs_ref[:, gcols].astype(F32) * to_end_x[:, gcols]).astype(BF16)
        st_s[:, gcols] = st_s[:, gcols] * ea_last_x[:, gcols] + _dot_tn(bg, xw)

    y = jnp.concatenate(y_pairs, axis=1)
    if final:
        y = yf_ref[...] + y + dsk_ref[...] * xs_ref[...].astype(F32)
        yz = y * _silu(z_ref[...].astype(F32))
        y_out_ref[...] = (yz * _rms(yz) * ng_ref[...]).astype(y_out_ref.dtype)
    else:
        y_out_ref[...] = y

    @pl.when(c == pl.num_programs(1) - 1)
    def _():
        s_out_ref[...] = st_s[...].T


def _ssd(xs, bc, small, gb_row, a_row, state, yf, p_main, dsk, ng, *, bsz, seq, direction):
    nc = seq // CHUNK
    t = bsz * seq
    zero_init = state is None
    final = yf is not None

    def rows(col):
        return lambda b, c: (b * nc + _chunk_index(c, nc, direction), col)

    const = lambda b, c: (0, 0)
    in_specs = [pl.BlockSpec((CHUNK, SSM_W), rows(0)),
                pl.BlockSpec((CHUNK, 2 * SSM_G * SSM_N), rows(0)),
                pl.BlockSpec((CHUNK, SMALL_W), rows(0)),
                pl.BlockSpec((1, SMALL_W), const),
                pl.BlockSpec((1, SMALL_W), const)]
    args = [xs, bc, small, gb_row, a_row]
    if not zero_init:
        in_specs.append(pl.BlockSpec((None, None, SSM_W, SSM_N), lambda b, c: (b, direction, 0, 0)))
        args.append(state)
    if final:
        in_specs += [pl.BlockSpec((CHUNK, SSM_W), rows(0)),
                     pl.BlockSpec((CHUNK, SSM_W), rows(COL_Z)),
                     pl.BlockSpec((1, SSM_W), const),
                     pl.BlockSpec((1, SSM_W), const)]
        args += [yf, p_main, dsk, ng]
    return pl.pallas_call(
        functools.partial(_ssd_kernel, direction=direction, zero_init=zero_init, final=final),
        grid=(bsz, nc),
        in_specs=in_specs,
        out_specs=[pl.BlockSpec((CHUNK, SSM_W), rows(0)),
                   pl.BlockSpec((None, SSM_W, SSM_N), lambda b, c: (b, 0, 0))],
        out_shape=[jax.ShapeDtypeStruct((t, SSM_W), BF16 if final else F32),
                   jax.ShapeDtypeStruct((bsz, SSM_W, SSM_N), F32)],
        scratch_shapes=[pltpu.VMEM((SSM_N, SSM_W), F32)],
        compiler_params=_cparams(2),
        name="ssd_bwd" if direction else "ssd_fwd",
    )(*args)


def _outproj_kernel(hml_ref, y_ref, x_ref, mod_ref, g_ref, w_ref, o_ref):
    o_ref[...] = _dot(hml_ref[...], w_ref[0:ML_W, :]) + _dot(y_ref[...], w_ref[ML_W:ML_W + SSM_W, :])
    _residual_rows(x_ref, o_ref, o_ref, g_ref, mod_ref, 5, 1.0)


def _outproj(hml, y, x, mod, g, w, *, tm):
    t = x.shape[0]
    rows_per_mod = t // mod.shape[0]
    return pl.pallas_call(
        _outproj_kernel,
        grid=(t // tm,),
        in_specs=[pl.BlockSpec((tm, ML_W), lambda i: (i, 0)),
                  pl.BlockSpec((tm, SSM_W), lambda i: (i, 0)),
                  pl.BlockSpec((tm, D_MODEL), lambda i: (i, 0)),
                  pl.BlockSpec((None, N_MOD, D_MODEL), lambda i: ((i * tm) // rows_per_mod, 0, 0)),
                  pl.BlockSpec((1, D_MODEL), lambda i: (0, 0)),
                  pl.BlockSpec((ML_W + SSM_W, D_MODEL), lambda i: (0, 0))],
        out_specs=pl.BlockSpec((tm, D_MODEL), lambda i: (i, 0)),
        out_shape=jax.ShapeDtypeStruct((t, D_MODEL), F32),
        compiler_params=_cparams(1),
        name="outproj",
    )(hml, y, x, mod, g, w)


def _prepare_params(norm_g, w_in, gate_bias, dt_bias, a_log, d_skip, conv_w, conv_b, ml_norm_g, ssm_norm_g,
                    w_out, w_gate, w_up, w_down):
    n_gate = 4 * ML_W
    s0 = n_gate + 4 * ML_HEADS
    s1 = s0 + SSM_W + CONV_CH
    w_main = jnp.concatenate([w_in[:, :n_gate], w_in[:, s0:s1]], axis=1).astype(BF16)
    n_small = 4 * ML_HEADS + 2 * SSM_HEADS
    w_small = jnp.concatenate([w_in[:, n_gate:s0], w_in[:, s1:],
                               jnp.zeros((D_MODEL, SMALL_W - n_small), F32)], axis=1).astype(BF16)
    gb_row = jnp.concatenate([gate_bias.reshape(-1), dt_bias.reshape(-1),
                              jnp.zeros((SMALL_W - n_small,), F32)]).reshape(1, SMALL_W)
    a_row = jnp.concatenate([jnp.zeros((LANE_DT,), F32), -jnp.exp(a_log.reshape(-1)),
                             jnp.zeros((SMALL_W - n_small,), F32)]).reshape(1, SMALL_W)
    return dict(
        g=[norm_g[i].reshape(1, D_MODEL) for i in range(6)],
        w_main=w_main, w_small=w_small, gb_row=gb_row, a_row=a_row,
        dsk=jnp.repeat(d_skip, SSM_P).reshape(1, SSM_W),
        conv_w=conv_w, conv_b=conv_b.reshape(1, CONV_CH),
        mlg=ml_norm_g.reshape(1, ML_W), ng=ssm_norm_g.reshape(1, SSM_W),
        w_out=w_out.astype(BF16),
        wg=[w_gate[i].astype(BF16) for i in range(2)],
        wu=[w_up[i].astype(BF16) for i in range(2)],
        wd=[w_down[i].astype(BF16) for i in range(2)],
    )


def _trunk_path(x, mod, p, ml_state, ssm_state, seg_len, *, tm_ffn=512, tf=512, tm_proj=512, tn_proj=512):
    bsz, seq, _ = x.shape
    t = bsz * seq
    x = x.reshape(t, D_MODEL)
    x = _ffn(x, mod, p['g'][0], p['g'][1], p['wg'][0], p['wu'][0], p['wd'][0], mod_base=0, tm=tm_ffn, tf=tf)
    p_main, small = _inproj(x, mod, p['g'][2], p['w_main'], p['w_small'], tm=tm_proj, tn=tn_proj)

    conv_rows = max(seg_len, 256)
    xs = _conv(p_main, p['conv_w'], p['conv_b'], seg_len=seg_len, tile0=0, n_tiles=SSM_W // CONV_TILE, rows=conv_rows)
    bc = _conv(p_main, p['conv_w'], p['conv_b'], seg_len=seg_len, tile0=SSM_W // CONV_TILE, n_tiles=1, rows=conv_rows)

    if ml_state is not None:
        c0, n0, m0 = ml_state
        ml_state = (c0, n0, jnp.broadcast_to(m0[..., None], m0.shape + (SMALL_W,)))
        ssm_state = ssm_state.reshape(bsz, 2, SSM_W, SSM_N)
    hf, c_f, n_f, m_f = _mlstm(p_main, small, p['gb_row'], ml_state, None, None, bsz=bsz, seq=seq, direction=0)
    hml, c_b, n_b, m_b = _mlstm(p_main, small, p['gb_row'], ml_state, hf, p['mlg'], bsz=bsz, seq=seq, direction=1)
    yf, s_f = _ssd(xs, bc, small, p['gb_row'], p['a_row'], ssm_state, None, None, None, None,
                   bsz=bsz, seq=seq, direction=0)
    y, s_b = _ssd(xs, bc, small, p['gb_row'], p['a_row'], ssm_state, yf, p_main, p['dsk'], p['ng'],
                  bsz=bsz, seq=seq, direction=1)

    x = _outproj(hml, y, x, mod, p['g'][3], p['w_out'], tm=tm_proj)
    x = _ffn(x, mod, p['g'][4], p['g'][5], p['wg'][1], p['wu'][1], p['wd'][1], mod_base=6, tm=tm_ffn, tf=tf)

    new_c = jnp.stack([c_f, c_b], axis=1)
    new_n = jnp.stack([n_f, n_b], axis=1)
    new_m = jnp.stack([m_f[..., 0], m_b[..., 0]], axis=1)
    new_s = jnp.stack([s_f, s_b], axis=1).reshape(bsz, 2, SSM_HEADS, SSM_P, SSM_N)
    return x.reshape(bsz, seq, D_MODEL), new_c, new_n, new_m, new_s


def kernel(x_prompt, x_sample, state_mlstm_C, state_mlstm_n, state_mlstm_m, state_ssm, c, c_ctx, w_ada, b_ada,
           norm_g, w_in, gate_bias, dt_bias, a_log, d_skip, conv_w, conv_b, ml_norm_g, ssm_norm_g, w_out,
           ffn_w_gate, ffn_w_up, ffn_w_down):
    depth = w_in.shape[0]
    bd = x_sample.shape[0]
    y_p, y_s = x_prompt, x_sample
    out_c, out_n, out_m, out_s = [], [], [], []
    cvec = jnp.concatenate([c_ctx[None], c, jnp.zeros((8 - 1 - bd, D_MODEL), F32)], axis=0)
    for l in range(depth):
        p = _prepare_params(norm_g[l], w_in[l], gate_bias[l], dt_bias[l], a_log[l], d_skip[l], conv_w[l], conv_b[l],
                            ml_norm_g[l], ssm_norm_g[l], w_out[l], ffn_w_gate[l], ffn_w_up[l], ffn_w_down[l])
        mod = _ada(cvec, w_ada[l], b_ada[l].reshape(1, -1)).reshape(8, N_MOD, D_MODEL)
        y_p, cn, nn, mn, sn = _trunk_path(y_p, mod[0:1], p, None, None, x_prompt.shape[1])
        lat_state = (state_mlstm_C[:, l], state_mlstm_n[:, l], state_mlstm_m[:, l])
        y_s, _, _, _, _ = _trunk_path(y_s, mod[1:1 + bd], p, lat_state, state_ssm[:, l], GRID_W)
        out_c.append(cn)
        out_n.append(nn)
        out_m.append(mn)
        out_s.append(sn)
    return (y_p, y_s, jnp.stack(out_c, axis=1), jnp.stack(out_n, axis=1), jnp.stack(out_m, axis=1),
            jnp.stack(out_s, axis=1))
```

```python
import functools

import jax
import jax.numpy as jnp
from jax import lax
from jax.experimental import pallas as pl
from jax.experimental.pallas import tpu as pltpu

F32 = jnp.float32
BF16 = jnp.bfloat16
HIGHEST = lax.Precision.HIGHEST

D_MODEL = 2048
GRID_W = 64
ML_HEADS = 4
ML_W = 1024
ML_DH = ML_W // ML_HEADS
SSM_W = 1024
SSM_P = 64
SSM_HEADS = SSM_W // SSM_P
SSM_N = 128
SSM_G = 2
SSM_R = SSM_HEADS // SSM_G
CONV_K = 5
CONV_CH = SSM_W + 2 * SSM_G * SSM_N
D_FF = 5632
CHUNK = 128
N_MOD = 9
EPS = 1e-6

MAIN_W = 4 * ML_W + SSM_W + CONV_CH
COL_Q, COL_K, COL_V, COL_O, COL_Z = 0, 1, 2, 3, 4
CONV_TILE = 512
COL_XBC = (4 * ML_W + SSM_W) // CONV_TILE
SMALL_W = 128
LANE_IG, LANE_FG, LANE_DT = 0, 2 * ML_HEADS, 4 * ML_HEADS

VMEM_LIMIT = 48 * 1024 * 1024
FFN_VMEM_LIMIT = 60 * 1024 * 1024
ROW_CHUNK = 128


def _cparams(n_axes, vmem=VMEM_LIMIT):
    return pltpu.CompilerParams(dimension_semantics=("arbitrary",) * n_axes, vmem_limit_bytes=vmem)


def _dot(a, b):
    return jnp.dot(a, b, preferred_element_type=F32)


def _dot_nt(a, b):
    return lax.dot_general(a, b, (((1,), (1,)), ((), ())), preferred_element_type=F32)


def _dot_tn(a, b):
    return lax.dot_general(a, b, (((0,), (0,)), ((), ())), preferred_element_type=F32)


def _dot_exact(a, b):
    return jnp.dot(a, b, preferred_element_type=F32, precision=HIGHEST)


def _silu(x):
    return x * jax.nn.sigmoid(x)


def _softplus(x):
    return jnp.maximum(x, 0.0) + jnp.log1p(jnp.exp(-jnp.abs(x)))


def _log_sigmoid(x):
    return jnp.minimum(x, 0.0) - jnp.log1p(jnp.exp(-jnp.abs(x)))


def _rms(x):
    return lax.rsqrt(jnp.mean(x * x, axis=-1, keepdims=True) + EPS)


def _adaln_rows(x_ref, h_ref, g_ref, mod_ref, mod_base):
    def body(i, carry):
        r0 = pl.multiple_of(i * ROW_CHUNK, ROW_CHUNK)
        x = x_ref[pl.ds(r0, ROW_CHUNK), :]
        shift = mod_ref[mod_base:mod_base + 1, :]
        scale = mod_ref[mod_base + 1:mod_base + 2, :]
        h = (x * _rms(x)) * g_ref[...] * (1.0 + scale) + shift
        h_ref[pl.ds(r0, ROW_CHUNK), :] = h.astype(BF16)
        return carry
    lax.fori_loop(0, x_ref.shape[0] // ROW_CHUNK, body, 0)


def _residual_rows(x_ref, y_ref, o_ref, g_ref, mod_ref, gate_row, gate_mul):
    def body(i, carry):
        r0 = pl.multiple_of(i * ROW_CHUNK, ROW_CHUNK)
        y = y_ref[pl.ds(r0, ROW_CHUNK), :]
        gate = mod_ref[gate_row:gate_row + 1, :]
        upd = gate_mul * gate * ((y * _rms(y)) * g_ref[...])
        o_ref[pl.ds(r0, ROW_CHUNK), :] = x_ref[pl.ds(r0, ROW_CHUNK), :] + upd
        return carry
    lax.fori_loop(0, x_ref.shape[0] // ROW_CHUNK, body, 0)


def _ada_kernel(c_ref, w_ref, b_ref, o_ref):
    s = _silu(c_ref[...]).astype(BF16)
    o_ref[...] = _dot(s, w_ref[...].astype(BF16)) + b_ref[...]


def _ada(cv, w, b):
    n = w.shape[1]
    tn = 1024
    return pl.pallas_call(
        _ada_kernel,
        grid=(n // tn,),
        in_specs=[pl.BlockSpec((cv.shape[0], D_MODEL), lambda j: (0, 0)),
                  pl.BlockSpec((D_MODEL, tn), lambda j: (0, j)),
                  pl.BlockSpec((1, tn), lambda j: (0, j))],
        out_specs=pl.BlockSpec((cv.shape[0], tn), lambda j: (0, j)),
        out_shape=jax.ShapeDtypeStruct((cv.shape[0], n), F32),
        compiler_params=_cparams(1),
        name="ada_mod",
    )(cv, w, b)


def _ffn_kernel(x_ref, mod_ref, gin_ref, gout_ref, wg_ref, wu_ref, wd_ref, o_ref, h_ref, *, mod_base):
    j = pl.program_id(1)

    @pl.when(j == 0)
    def _():
        _adaln_rows(x_ref, h_ref, gin_ref, mod_ref, mod_base)
        o_ref[...] = jnp.zeros_like(o_ref)

    h = h_ref[...]
    a = (_silu(_dot(h, wg_ref[...])) * _dot(h, wu_ref[...])).astype(BF16)
    o_ref[...] += _dot(a, wd_ref[...])

    @pl.when(j == pl.num_programs(1) - 1)
    def _():
        _residual_rows(x_ref, o_ref, o_ref, gout_ref, mod_ref, mod_base + 2, 0.5)


def _ffn(x, mod, g_in, g_out, wg, wu, wd, *, mod_base, tm, tf):
    t = x.shape[0]
    rows_per_mod = t // mod.shape[0]
    return pl.pallas_call(
        functools.partial(_ffn_kernel, mod_base=mod_base),
        grid=(t // tm, D_FF // tf),
        in_specs=[pl.BlockSpec((tm, D_MODEL), lambda i, j: (i, 0), pipeline_mode=pl.Buffered(1)),
                  pl.BlockSpec((None, N_MOD, D_MODEL), lambda i, j: ((i * tm) // rows_per_mod, 0, 0)),
                  pl.BlockSpec((1, D_MODEL), lambda i, j: (0, 0)),
                  pl.BlockSpec((1, D_MODEL), lambda i, j: (0, 0)),
                  pl.BlockSpec((D_MODEL, tf), lambda i, j: (0, j)),
                  pl.BlockSpec((D_MODEL, tf), lambda i, j: (0, j)),
                  pl.BlockSpec((tf, D_MODEL), lambda i, j: (j, 0))],
        out_specs=pl.BlockSpec((tm, D_MODEL), lambda i, j: (i, 0)),
        out_shape=jax.ShapeDtypeStruct((t, D_MODEL), F32),
        scratch_shapes=[pltpu.VMEM((tm, D_MODEL), BF16)],
        compiler_params=_cparams(2, FFN_VMEM_LIMIT),
        name="ffn",
    )(x, mod, g_in, g_out, wg, wu, wd)


def _inproj_kernel(x_ref, mod_ref, g_ref, w_ref, ws_ref, o_ref, os_ref, h_ref):
    j = pl.program_id(1)

    @pl.when(j == 0)
    def _():
        _adaln_rows(x_ref, h_ref, g_ref, mod_ref, 3)
        os_ref[...] = _dot(h_ref[...], ws_ref[...])

    o_ref[...] = _dot(h_ref[...], w_ref[...]).astype(BF16)


def _inproj(x, mod, g, w_main, w_small, *, tm, tn):
    t = x.shape[0]
    rows_per_mod = t // mod.shape[0]
    return pl.pallas_call(
        _inproj_kernel,
        grid=(t // tm, MAIN_W // tn),
        in_specs=[pl.BlockSpec((tm, D_MODEL), lambda i, j: (i, 0)),
                  pl.BlockSpec((None, N_MOD, D_MODEL), lambda i, j: ((i * tm) // rows_per_mod, 0, 0)),
                  pl.BlockSpec((1, D_MODEL), lambda i, j: (0, 0)),
                  pl.BlockSpec((D_MODEL, tn), lambda i, j: (0, j)),
                  pl.BlockSpec((D_MODEL, SMALL_W), lambda i, j: (0, 0))],
        out_specs=[pl.BlockSpec((tm, tn), lambda i, j: (i, j)),
                   pl.BlockSpec((tm, SMALL_W), lambda i, j: (i, 0))],
        out_shape=[jax.ShapeDtypeStruct((t, MAIN_W), BF16),
                   jax.ShapeDtypeStruct((t, SMALL_W), F32)],
        scratch_shapes=[pltpu.VMEM((tm, D_MODEL), BF16)],
        compiler_params=_cparams(2, FFN_VMEM_LIMIT),
        name="inproj",
    )(x, mod, g, w_main, w_small)


def _conv_kernel(u_ref, w_ref, b_ref, o_ref, *, seg_len):
    u = u_ref[...].astype(F32)
    rows = u.shape[0]
    pos = lax.broadcasted_iota(jnp.int32, u.shape, 0) % seg_len
    acc = jnp.zeros_like(u) + b_ref[...]
    for j in range(CONV_K):
        off = j - CONV_K // 2
        shifted = u if off == 0 else pltpu.roll(u, (-off) % rows, 0)
        valid = jnp.logical_and(pos + off >= 0, pos + off < seg_len)
        acc = acc + jnp.where(valid, shifted, 0.0) * w_ref[j:j + 1, :]
    o_ref[...] = _silu(acc).astype(BF16)


def _conv(p_main, conv_w, conv_b, *, seg_len, tile0, n_tiles, rows):
    t = p_main.shape[0]
    return pl.pallas_call(
        functools.partial(_conv_kernel, seg_len=seg_len),
        grid=(t // rows, n_tiles),
        in_specs=[pl.BlockSpec((rows, CONV_TILE), lambda i, j: (i, COL_XBC + tile0 + j)),
                  pl.BlockSpec((CONV_K, CONV_TILE), lambda i, j: (0, tile0 + j)),
                  pl.BlockSpec((1, CONV_TILE), lambda i, j: (0, tile0 + j))],
        out_specs=pl.BlockSpec((rows, CONV_TILE), lambda i, j: (i, j)),
        out_shape=jax.ShapeDtypeStruct((t, n_tiles * CONV_TILE), BF16),
        compiler_params=_cparams(2),
        name="dwconv",
    )(p_main, conv_w, conv_b)


def _tri_mask(direction):
    row = lax.broadcasted_iota(jnp.int32, (CHUNK, CHUNK), 0)
    col = lax.broadcasted_iota(jnp.int32, (CHUNK, CHUNK), 1)
    return (col <= row) if direction == 0 else (col >= row)


def _chunk_index(c, nc, direction):
    return c if direction == 0 else nc - 1 - c


def _mlstm_kernel(*refs, direction, zero_init, final):
    refs = list(refs)
    q_ref, k_ref, v_ref, sm_ref, gb_ref = refs[:5]
    pos = 5
    if not zero_init:
        c0_ref, n0_ref, m0_ref = refs[pos:pos + 3]
        pos += 3
    if final:
        hf_ref, og_ref, mlg_ref = refs[pos:pos + 3]
        pos += 3
    h_out_ref, c_out_ref, n_out_ref, m_out_ref, c_s, n_s, m_s = refs[pos:]

    c = pl.program_id(1)

    @pl.when(c == 0)
    def _():
        if zero_init:
            c_s[...] = jnp.zeros_like(c_s)
            n_s[...] = jnp.zeros_like(n_s)
            m_s[...] = jnp.zeros_like(m_s)
        else:
            c_s[...] = c0_ref[...]
            n_s[...] = n0_ref[...]
            m_s[...] = m0_ref[...]

    tri = _tri_mask(direction)
    lane = lax.broadcasted_iota(jnp.int32, (CHUNK, SMALL_W), 1)
    g = sm_ref[...] + gb_ref[...]
    is_fg = jnp.logical_and(lane >= LANE_FG, lane < LANE_DT)
    p = jnp.where(is_fg, _log_sigmoid(g), g)
    cum = _dot_exact(tri.astype(F32), p)
    p_t = p.T
    cum_t = cum.T
    last = CHUNK - 1 if direction == 0 else 0

    for h in range(ML_HEADS):
        li_lane = LANE_IG + direction * ML_HEADS + h
        lf_lane = LANE_FG + direction * ML_HEADS + h
        b_col = cum[:, lf_lane:lf_lane + 1]
        b_row = cum_t[lf_lane:lf_lane + 1, :]
        li_col = p[:, li_lane:li_lane + 1]
        li_row = p_t[li_lane:li_lane + 1, :]
        b_last = cum[last:last + 1, lf_lane:lf_lane + 1]
        m_prev = m_s[h:h + 1, 0:1]

        log_d = jnp.where(tri, b_col - b_row + li_row, -jnp.inf)
        m_inter = b_col + m_prev
        m_t = jnp.maximum(m_inter, jnp.max(log_d, axis=1, keepdims=True))
        d = jnp.exp(log_d - m_t)

        sl = slice(h * ML_DH, (h + 1) * ML_DH)
        qh = q_ref[:, sl]
        kh = (k_ref[:, sl].astype(F32) * (ML_DH ** -0.5))
        kh_b = kh.astype(BF16)
        vh = v_ref[:, sl]
        scores = _dot_nt(qh, kh_b) * d
        inter = jnp.exp(m_inter - m_t)
        c_old = c_s[h]
        n_old = n_s[h:h + 1, :]
        num = _dot(scores.astype(BF16), vh) + inter * _dot(qh, c_old.astype(BF16))
        qn = jnp.sum(qh.astype(F32) * n_old, axis=1, keepdims=True)
        den = jnp.sum(scores, axis=1, keepdims=True) + inter * qn
        hh = num / jnp.maximum(jnp.abs(den), jnp.exp(-m_t))

        log_w = b_last - b_col + li_col
        m_end = b_last + m_prev
        m_new = jnp.maximum(m_end, jnp.max(log_w, axis=0, keepdims=True))
        w = jnp.exp(log_w - m_new)
        decay = jnp.exp(m_end - m_new)
        kw = kh * w
        c_s[h] = decay * c_old + _dot_tn(kw.astype(BF16), vh)
        n_s[h:h + 1, :] = decay * n_old + jnp.sum(kw, axis=0, keepdims=True)
        m_s[h:h + 1, :] = jnp.broadcast_to(m_new, (1, SMALL_W))

        if final:
            hs = hf_ref[:, sl] + hh
            mu = jnp.mean(hs, axis=1, keepdims=True)
            cen = hs - mu
            var = jnp.mean(cen * cen, axis=1, keepdims=True)
            hn = cen * lax.rsqrt(var + EPS) * mlg_ref[:, sl]
            h_out_ref[:, sl] = (jax.nn.sigmoid(og_ref[:, sl].astype(F32)) * hn).astype(h_out_ref.dtype)
        else:
            h_out_ref[:, sl] = hh

    @pl.when(c == pl.num_programs(1) - 1)
    def _():
        c_out_ref[...] = c_s[...]
        n_out_ref[...] = n_s[...]
        m_out_ref[...] = m_s[...]


def _mlstm(p_main, small, gb_row, state, hf, mlg, *, bsz, seq, direction):
    nc = seq // CHUNK
    t = bsz * seq
    zero_init = state is None
    final = hf is not None

    def rows(col):
        return lambda b, c: (b * nc + _chunk_index(c, nc, direction), col)

    in_specs = [pl.BlockSpec((CHUNK, ML_W), rows(COL_Q)),
                pl.BlockSpec((CHUNK, ML_W), rows(COL_K)),
                pl.BlockSpec((CHUNK, ML_W), rows(COL_V)),
                pl.BlockSpec((CHUNK, SMALL_W), rows(0)),
                pl.BlockSpec((1, SMALL_W), lambda b, c: (0, 0))]
    args = [p_main, p_main, p_main, small, gb_row]
    if not zero_init:
        c0, n0, m0 = state
        in_specs += [pl.BlockSpec((None, None, ML_HEADS, ML_DH, ML_DH), lambda b, c: (b, direction, 0, 0, 0)),
                     pl.BlockSpec((None, None, ML_HEADS, ML_DH), lambda b, c: (b, direction, 0, 0)),
                     pl.BlockSpec((None, None, ML_HEADS, SMALL_W), lambda b, c: (b, direction, 0, 0))]
        args += [c0, n0, m0]
    if final:
        in_specs += [pl.BlockSpec((CHUNK, ML_W), rows(0)),
                     pl.BlockSpec((CHUNK, ML_W), rows(COL_O)),
                     pl.BlockSpec((1, ML_W), lambda b, c: (0, 0))]
        args += [hf, p_main, mlg]
    out_specs = [pl.BlockSpec((CHUNK, ML_W), rows(0)),
                 pl.BlockSpec((None, ML_HEADS, ML_DH, ML_DH), lambda b, c: (b, 0, 0, 0)),
                 pl.BlockSpec((None, ML_HEADS, ML_DH), lambda b, c: (b, 0, 0)),
                 pl.BlockSpec((None, ML_HEADS, SMALL_W), lambda b, c: (b, 0, 0))]
    out_shape = [jax.ShapeDtypeStruct((t, ML_W), BF16 if final else F32),
                 jax.ShapeDtypeStruct((bsz, ML_HEADS, ML_DH, ML_DH), F32),
                 jax.ShapeDtypeStruct((bsz, ML_HEADS, ML_DH), F32),
                 jax.ShapeDtypeStruct((bsz, ML_HEADS, SMALL_W), F32)]
    return pl.pallas_call(
        functools.partial(_mlstm_kernel, direction=direction, zero_init=zero_init, final=final),
        grid=(bsz, nc),
        in_specs=in_specs,
        out_specs=out_specs,
        out_shape=out_shape,
        scratch_shapes=[pltpu.VMEM((ML_HEADS, ML_DH, ML_DH), F32),
                        pltpu.VMEM((ML_HEADS, ML_DH), F32),
                        pltpu.VMEM((ML_HEADS, SMALL_W), F32)],
        compiler_params=_cparams(2),
        name="mlstm_bwd" if direction else "mlstm_fwd",
    )(*args)


def _ssd_kernel(*refs, direction, zero_init, final):
    refs = list(refs)
    xs_ref, bc_ref, sm_ref, gb_ref, arow_ref = refs[:5]
    pos = 5
    if not zero_init:
        s0_ref = refs[pos]
        pos += 1
    if final:
        yf_ref, z_ref, dsk_ref, ng_ref = refs[pos:pos + 4]
        pos += 4
    y_out_ref, s_out_ref, st_s = refs[pos:]

    c = pl.program_id(1)

    @pl.when(c == 0)
    def _():
        if zero_init:
            st_s[...] = jnp.zeros_like(st_s)
        else:
            st_s[...] = s0_ref[...].T

    tri = _tri_mask(direction)
    lane = lax.broadcasted_iota(jnp.int32, (CHUNK, SMALL_W), 1)
    g = sm_ref[...] + gb_ref[...]
    is_dt = jnp.logical_and(lane >= LANE_DT, lane < LANE_DT + 2 * SSM_HEADS)
    p = jnp.where(is_dt, _softplus(g), g)
    cum = _dot_exact(tri.astype(F32), p * arow_ref[...])
    p_t = p.T
    cum_t = cum.T
    last = CHUNK - 1 if direction == 0 else 0
    base = LANE_DT + direction * SSM_HEADS

    a16 = cum[:, base:base + SSM_HEADS]
    dt16 = p[:, base:base + SSM_HEADS]
    a_last = cum[last:last + 1, base:base + SSM_HEADS]
    to_end = jnp.exp(a_last - a16) * dt16
    ea16 = jnp.exp(a16)
    expand = (lax.broadcasted_iota(jnp.int32, (SSM_HEADS, SSM_W), 1) // SSM_P
              == lax.broadcasted_iota(jnp.int32, (SSM_HEADS, SSM_W), 0)).astype(F32)
    to_end_x = _dot_exact(to_end, expand)
    ea_last_x = _dot_exact(jnp.broadcast_to(jnp.exp(a_last), (8, SSM_HEADS)), expand)[0:1, :]

    lane_p = lax.broadcasted_iota(jnp.int32, (CHUNK, 2 * SSM_P), 1)
    lo = lane_p < SSM_P
    zero_b = jnp.zeros((CHUNK, 2 * SSM_P), BF16)

    y_pairs = []
    for gi in range(SSM_G):
        bg = bc_ref[:, gi * SSM_N:(gi + 1) * SSM_N]
        cg = bc_ref[:, (SSM_G + gi) * SSM_N:(SSM_G + gi + 1) * SSM_N]
        cb = _dot_nt(cg, bg)
        cg32 = cg.astype(F32)
        for pr in range(SSM_R // 2):
            h0 = gi * SSM_R + 2 * pr
            lhs = []
            for hh in (h0, h0 + 1):
                a_col = a16[:, hh:hh + 1]
                a_row = cum_t[base + hh:base + hh + 1, :]
                dt_row = p_t[base + hh:base + hh + 1, :]
                decay = jnp.exp(jnp.where(tri, a_col - a_row, -jnp.inf))
                lhs.append((cb * decay * dt_row).astype(BF16))
            for hh in (h0, h0 + 1):
                lhs.append((cg32 * ea16[:, hh:hh + 1]).astype(BF16))
            cols = slice(h0 * SSM_P, (h0 + 2) * SSM_P)
            xp = xs_ref[:, cols]
            sp = st_s[:, cols].astype(BF16)
            rhs = jnp.concatenate([jnp.where(lo, xp, zero_b), jnp.where(lo, zero_b, xp),
                                   jnp.where(lo, sp, zero_b), jnp.where(lo, zero_b, sp)], axis=0)
            y_pairs.append(_dot(jnp.concatenate(lhs, axis=1), rhs))

        gcols = slice(gi * SSM_R * SSM_P, (gi + 1) * SSM_R * SSM_P)
        xw = (xs_ref[:, gcols].astype(F32) * to_end_x[:, gcols]).astype(BF16)
        st_s[:, gcols] = st_s[:, gcols] * ea_last_x[:, gcols] + _dot_tn(bg, xw)

    y = jnp.concatenate(y_pairs, axis=1)
    if final:
        y = yf_ref[...] + y + dsk_ref[...] * xs_ref[...].astype(F32)
        yz = y * _silu(z_ref[...].astype(F32))
        y_out_ref[...] = (yz * _rms(yz) * ng_ref[...]).astype(y_out_ref.dtype)
    else:
        y_out_ref[...] = y

    @pl.when(c == pl.num_programs(1) - 1)
    def _():
        s_out_ref[...] = st_s[...].T


def _ssd(xs, bc, small, gb_row, a_row, state, yf, p_main, dsk, ng, *, bsz, seq, direction):
    nc = seq // CHUNK
    t = bsz * seq
    zero_init = state is None
    final = yf is not None

    def rows(col):
        return lambda b, c: (b * nc + _chunk_index(c, nc, direction), col)

    const = lambda b, c: (0, 0)
    in_specs = [pl.BlockSpec((CHUNK, SSM_W), rows(0)),
                pl.BlockSpec((CHUNK, 2 * SSM_G * SSM_N), rows(0)),
                pl.BlockSpec((CHUNK, SMALL_W), rows(0)),
                pl.BlockSpec((1, SMALL_W), const),
                pl.BlockSpec((1, SMALL_W), const)]
    args = [xs, bc, small, gb_row, a_row]
    if not zero_init:
        in_specs.append(pl.BlockSpec((None, None, SSM_W, SSM_N), lambda b, c: (b, direction, 0, 0)))
        args.append(state)
    if final:
        in_specs += [pl.BlockSpec((CHUNK, SSM_W), rows(0)),
                     pl.BlockSpec((CHUNK, SSM_W), rows(COL_Z)),
                     pl.BlockSpec((1, SSM_W), const),
                     pl.BlockSpec((1, SSM_W), const)]
        args += [yf, p_main, dsk, ng]
    return pl.pallas_call(
        functools.partial(_ssd_kernel, direction=direction, zero_init=zero_init, final=final),
        grid=(bsz, nc),
        in_specs=in_specs,
        out_specs=[pl.BlockSpec((CHUNK, SSM_W), rows(0)),
                   pl.BlockSpec((None, SSM_W, SSM_N), lambda b, c: (b, 0, 0))],
        out_shape=[jax.ShapeDtypeStruct((t, SSM_W), BF16 if final else F32),
                   jax.ShapeDtypeStruct((bsz, SSM_W, SSM_N), F32)],
        scratch_shapes=[pltpu.VMEM((SSM_N, SSM_W), F32)],
        compiler_params=_cparams(2),
        name="ssd_bwd" if direction else "ssd_fwd",
    )(*args)


def _outproj_kernel(hml_ref, y_ref, x_ref, mod_ref, g_ref, w_ref, o_ref):
    o_ref[...] = _dot(hml_ref[...], w_ref[0:ML_W, :]) + _dot(y_ref[...], w_ref[ML_W:ML_W + SSM_W, :])
    _residual_rows(x_ref, o_ref, o_ref, g_ref, mod_ref, 5, 1.0)


def _outproj(hml, y, x, mod, g, w, *, tm):
    t = x.shape[0]
    rows_per_mod = t // mod.shape[0]
    return pl.pallas_call(
        _outproj_kernel,
        grid=(t // tm,),
        in_specs=[pl.BlockSpec((tm, ML_W), lambda i: (i, 0)),
                  pl.BlockSpec((tm, SSM_W), lambda i: (i, 0)),
                  pl.BlockSpec((tm, D_MODEL), lambda i: (i, 0)),
                  pl.BlockSpec((None, N_MOD, D_MODEL), lambda i: ((i * tm) // rows_per_mod, 0, 0)),
                  pl.BlockSpec((1, D_MODEL), lambda i: (0, 0)),
                  pl.BlockSpec((ML_W + SSM_W, D_MODEL), lambda i: (0, 0))],
        out_specs=pl.BlockSpec((tm, D_MODEL), lambda i: (i, 0)),
        out_shape=jax.ShapeDtypeStruct((t, D_MODEL), F32),
        compiler_params=_cparams(1),
        name="outproj",
    )(hml, y, x, mod, g, w)


def _prepare_params(norm_g, w_in, gate_bias, dt_bias, a_log, d_skip, conv_w, conv_b, ml_norm_g, ssm_norm_g,
                    w_out, w_gate, w_up, w_down):
    n_gate = 4 * ML_W
    s0 = n_gate + 4 * ML_HEADS
    s1 = s0 + SSM_W + CONV_CH
    w_main = jnp.concatenate([w_in[:, :n_gate], w_in[:, s0:s1]], axis=1).astype(BF16)
    n_small = 4 * ML_HEADS + 2 * SSM_HEADS
    w_small = jnp.concatenate([w_in[:, n_gate:s0], w_in[:, s1:],
                               jnp.zeros((D_MODEL, SMALL_W - n_small), F32)], axis=1).astype(BF16)
    gb_row = jnp.concatenate([gate_bias.reshape(-1), dt_bias.reshape(-1),
                              jnp.zeros((SMALL_W - n_small,), F32)]).reshape(1, SMALL_W)
    a_row = jnp.concatenate([jnp.zeros((LANE_DT,), F32), -jnp.exp(a_log.reshape(-1)),
                             jnp.zeros((SMALL_W - n_small,), F32)]).reshape(1, SMALL_W)
    return dict(
        g=[norm_g[i].reshape(1, D_MODEL) for i in range(6)],
        w_main=w_main, w_small=w_small, gb_row=gb_row, a_row=a_row,
        dsk=jnp.repeat(d_skip, SSM_P).reshape(1, SSM_W),
        conv_w=conv_w, conv_b=conv_b.reshape(1, CONV_CH),
        mlg=ml_norm_g.reshape(1, ML_W), ng=ssm_norm_g.reshape(1, SSM_W),
        w_out=w_out.astype(BF16),
        wg=[w_gate[i].astype(BF16) for i in range(2)],
        wu=[w_up[i].astype(BF16) for i in range(2)],
        wd=[w_down[i].astype(BF16) for i in range(2)],
    )


def _trunk_path(x, mod, p, ml_state, ssm_state, seg_len, *, tm_ffn=1024, tf=512, tm_proj=512, tn_proj=3328):
    bsz, seq, _ = x.shape
    t = bsz * seq
    x = x.reshape(t, D_MODEL)
    x = _ffn(x, mod, p['g'][0], p['g'][1], p['wg'][0], p['wu'][0], p['wd'][0], mod_base=0, tm=tm_ffn, tf=tf)
    p_main, small = _inproj(x, mod, p['g'][2], p['w_main'], p['w_small'], tm=tm_proj, tn=tn_proj)

    conv_rows = max(seg_len, 256)
    xs = _conv(p_main, p['conv_w'], p['conv_b'], seg_len=seg_len, tile0=0, n_tiles=SSM_W // CONV_TILE, rows=conv_rows)
    bc = _conv(p_main, p['conv_w'], p['conv_b'], seg_len=seg_len, tile0=SSM_W // CONV_TILE, n_tiles=1, rows=conv_rows)

    if ml_state is not None:
        c0, n0, m0 = ml_state
        ml_state = (c0, n0, jnp.broadcast_to(m0[..., None], m0.shape + (SMALL_W,)))
        ssm_state = ssm_state.reshape(bsz, 2, SSM_W, SSM_N)
    hf, c_f, n_f, m_f = _mlstm(p_main, small, p['gb_row'], ml_state, None, None, bsz=bsz, seq=seq, direction=0)
    hml, c_b, n_b, m_b = _mlstm(p_main, small, p['gb_row'], ml_state, hf, p['mlg'], bsz=bsz, seq=seq, direction=1)
    yf, s_f = _ssd(xs, bc, small, p['gb_row'], p['a_row'], ssm_state, None, None, None, None,
                   bsz=bsz, seq=seq, direction=0)
    y, s_b = _ssd(xs, bc, small, p['gb_row'], p['a_row'], ssm_state, yf, p_main, p['dsk'], p['ng'],
                  bsz=bsz, seq=seq, direction=1)

    x = _outproj(hml, y, x, mod, p['g'][3], p['w_out'], tm=tm_proj)
    x = _ffn(x, mod, p['g'][4], p['g'][5], p['wg'][1], p['wu'][1], p['wd'][1], mod_base=6, tm=tm_ffn, tf=tf)

    new_c = jnp.stack([c_f, c_b], axis=1)
    new_n = jnp.stack([n_f, n_b], axis=1)
    new_m = jnp.stack([m_f[..., 0], m_b[..., 0]], axis=1)
    new_s = jnp.stack([s_f, s_b], axis=1).reshape(bsz, 2, SSM_HEADS, SSM_P, SSM_N)
    return x.reshape(bsz, seq, D_MODEL), new_c, new_n, new_m, new_s


def kernel(x_prompt, x_sample, state_mlstm_C, state_mlstm_n, state_mlstm_m, state_ssm, c, c_ctx, w_ada, b_ada,
           norm_g, w_in, gate_bias, dt_bias, a_log, d_skip, conv_w, conv_b, ml_norm_g, ssm_norm_g, w_out,
           ffn_w_gate, ffn_w_up, ffn_w_down):
    depth = w_in.shape[0]
    bd = x_sample.shape[0]
    y_p, y_s = x_prompt, x_sample
    out_c, out_n, out_m, out_s = [], [], [], []
    cvec = jnp.concatenate([c_ctx[None], c, jnp.zeros((8 - 1 - bd, D_MODEL), F32)], axis=0)
    for l in range(depth):
        p = _prepare_params(norm_g[l], w_in[l], gate_bias[l], dt_bias[l], a_log[l], d_skip[l], conv_w[l], conv_b[l],
                            ml_norm_g[l], ssm_norm_g[l], w_out[l], ffn_w_gate[l], ffn_w_up[l], ffn_w_down[l])
        mod = _ada(cvec, w_ada[l], b_ada[l].reshape(1, -1)).reshape(8, N_MOD, D_MODEL)
        y_p, cn, nn, mn, sn = _trunk_path(y_p, mod[0:1], p, None, None, x_prompt.shape[1])
        lat_state = (state_mlstm_C[:, l], state_mlstm_n[:, l], state_mlstm_m[:, l])
        y_s, _, _, _, _ = _trunk_path(y_s, mod[1:1 + bd], p, lat_state, state_ssm[:, l], GRID_W)
        out_c.append(cn)
        out_n.append(nn)
        out_m.append(mn)
        out_s.append(sn)
    return (y_p, y_s, jnp.stack(out_c, axis=1), jnp.stack(out_n, axis=1), jnp.stack(out_m, axis=1),
            jnp.stack(out_s, axis=1))
```

```python
import functools

import jax
import jax.numpy as jnp
from jax import lax
from jax.experimental import pallas as pl
from jax.experimental.pallas import tpu as pltpu

F32 = jnp.float32
BF16 = jnp.bfloat16
HIGHEST = lax.Precision.HIGHEST

D_MODEL = 2048
GRID_W = 64
ML_HEADS = 4
ML_W = 1024
ML_DH = ML_W // ML_HEADS
SSM_W = 1024
SSM_P = 64
SSM_HEADS = SSM_W // SSM_P
SSM_N = 128
SSM_G = 2
SSM_R = SSM_HEADS // SSM_G
CONV_K = 5
CONV_CH = SSM_W + 2 * SSM_G * SSM_N
BC_W = 2 * SSM_G * SSM_N
D_FF = 5632
CHUNK = 128
N_MOD = 9
EPS = 1e-6

MAIN_W = 4 * ML_W + SSM_W + CONV_CH
COL_Q, COL_K, COL_V, COL_O, COL_Z, COL_XS = 0, 1, 2, 3, 4, 5
COL_BC = (MAIN_W - BC_W) // BC_W
SMALL_W = 128
N_SMALL = 4 * ML_HEADS + 2 * SSM_HEADS
LANE_IG, LANE_FG, LANE_DT = 0, 2 * ML_HEADS, 4 * ML_HEADS

VMEM_LIMIT = 48 * 1024 * 1024
BIG_VMEM_LIMIT = 60 * 1024 * 1024
ROW_CHUNK = 128
CONV_ROWS = 256


def _cparams(n_axes, vmem=VMEM_LIMIT):
    return pltpu.CompilerParams(dimension_semantics=("arbitrary",) * n_axes, vmem_limit_bytes=vmem)


def _dot(a, b):
    return jnp.dot(a, b, preferred_element_type=F32)


def _dot_nt(a, b):
    return lax.dot_general(a, b, (((1,), (1,)), ((), ())), preferred_element_type=F32)


def _dot_tn(a, b):
    return lax.dot_general(a, b, (((0,), (0,)), ((), ())), preferred_element_type=F32)


def _dot_exact(a, b):
    return jnp.dot(a, b, preferred_element_type=F32, precision=HIGHEST)


def _silu(x):
    return x * jax.nn.sigmoid(x)


def _softplus(x):
    return jnp.maximum(x, 0.0) + jnp.log1p(jnp.exp(-jnp.abs(x)))


def _log_sigmoid(x):
    return jnp.minimum(x, 0.0) - jnp.log1p(jnp.exp(-jnp.abs(x)))


def _rms(x):
    return lax.rsqrt(jnp.mean(x * x, axis=-1, keepdims=True) + EPS)


def _adaln_rows(x_ref, h_ref, g_ref, mod_ref, mod_base):
    def body(i, carry):
        r0 = pl.multiple_of(i * ROW_CHUNK, ROW_CHUNK)
        x = x_ref[pl.ds(r0, ROW_CHUNK), :]
        shift = mod_ref[mod_base:mod_base + 1, :]
        scale = mod_ref[mod_base + 1:mod_base + 2, :]
        h = (x * _rms(x)) * g_ref[...] * (1.0 + scale) + shift
        h_ref[pl.ds(r0, ROW_CHUNK), :] = h.astype(BF16)
        return carry
    lax.fori_loop(0, x_ref.shape[0] // ROW_CHUNK, body, 0)


def _residual_rows(x_ref, y_ref, o_ref, g_ref, mod_ref, gate_row, gate_mul):
    def body(i, carry):
        r0 = pl.multiple_of(i * ROW_CHUNK, ROW_CHUNK)
        y = y_ref[pl.ds(r0, ROW_CHUNK), :]
        gate = mod_ref[gate_row:gate_row + 1, :]
        upd = gate_mul * gate * ((y * _rms(y)) * g_ref[...])
        o_ref[pl.ds(r0, ROW_CHUNK), :] = x_ref[pl.ds(r0, ROW_CHUNK), :] + upd
        return carry
    lax.fori_loop(0, x_ref.shape[0] // ROW_CHUNK, body, 0)


def _cast_kernel(x_ref, o_ref):
    o_ref[...] = x_ref[...].astype(BF16)


def _cast_bf16(w):
    shape = w.shape
    w2 = w.reshape(-1, shape[-1])
    rows = w2.shape[0]
    tr = max(8, min(rows, (8 * 1024 * 1024) // (4 * shape[-1]) // 256 * 256))
    while rows % tr:
        tr //= 2
    out = pl.pallas_call(
        _cast_kernel,
        grid=(rows // tr,),
        in_specs=[pl.BlockSpec((tr, shape[-1]), lambda i: (i, 0))],
        out_specs=pl.BlockSpec((tr, shape[-1]), lambda i: (i, 0)),
        out_shape=jax.ShapeDtypeStruct(w2.shape, BF16),
        compiler_params=_cparams(1),
        name="cast_bf16",
    )(w2)
    return out.reshape(shape)


def _ada_kernel(c_ref, w_ref, b_ref, o_ref):
    s = _silu(c_ref[...]).astype(BF16)
    o_ref[...] = _dot(s, w_ref[...].astype(BF16)) + b_ref[...]


def _ada(cv, w, b):
    n = w.shape[1]
    tn = 1024
    return pl.pallas_call(
        _ada_kernel,
        grid=(n // tn,),
        in_specs=[pl.BlockSpec((cv.shape[0], D_MODEL), lambda j: (0, 0)),
                  pl.BlockSpec((D_MODEL, tn), lambda j: (0, j)),
                  pl.BlockSpec((1, tn), lambda j: (0, j))],
        out_specs=pl.BlockSpec((cv.shape[0], tn), lambda j: (0, j)),
        out_shape=jax.ShapeDtypeStruct((cv.shape[0], n), F32),
        compiler_params=_cparams(1),
        name="ada_mod",
    )(cv, w, b)


def _ffn_kernel(x_ref, mod_ref, gin_ref, gout_ref, wg_ref, wu_ref, wd_ref, o_ref, h_ref, *, mod_base):
    j = pl.program_id(1)

    @pl.when(j == 0)
    def _():
        _adaln_rows(x_ref, h_ref, gin_ref, mod_ref, mod_base)
        o_ref[...] = jnp.zeros_like(o_ref)

    h = h_ref[...]
    a = (_silu(_dot(h, wg_ref[...])) * _dot(h, wu_ref[...])).astype(BF16)
    o_ref[...] += _dot(a, wd_ref[...])

    @pl.when(j == pl.num_programs(1) - 1)
    def _():
        _residual_rows(x_ref, o_ref, o_ref, gout_ref, mod_ref, mod_base + 2, 0.5)


def _ffn(x, mod, g_in, g_out, wg, wu, wd, *, idx, tm, tf):
    t = x.shape[0]
    rows_per_mod = t // mod.shape[0]
    return pl.pallas_call(
        functools.partial(_ffn_kernel, mod_base=6 * idx),
        grid=(t // tm, D_FF // tf),
        in_specs=[pl.BlockSpec((tm, D_MODEL), lambda i, j: (i, 0)),
                  pl.BlockSpec((None, N_MOD, D_MODEL), lambda i, j: ((i * tm) // rows_per_mod, 0, 0)),
                  pl.BlockSpec((1, D_MODEL), lambda i, j: (0, 0)),
                  pl.BlockSpec((1, D_MODEL), lambda i, j: (0, 0)),
                  pl.BlockSpec((None, D_MODEL, tf), lambda i, j: (idx, 0, j)),
                  pl.BlockSpec((None, D_MODEL, tf), lambda i, j: (idx, 0, j)),
                  pl.BlockSpec((None, tf, D_MODEL), lambda i, j: (idx, j, 0))],
        out_specs=pl.BlockSpec((tm, D_MODEL), lambda i, j: (i, 0)),
        out_shape=jax.ShapeDtypeStruct((t, D_MODEL), F32),
        scratch_shapes=[pltpu.VMEM((tm, D_MODEL), BF16)],
        compiler_params=_cparams(2),
        name="ffn",
    )(x, mod, g_in, g_out, wg, wu, wd)


def _conv_rows(u_ref, o_ref, cw_ref, cb_ref, col0, seg_len):
    def body(i, carry):
        r0 = pl.multiple_of(i * CONV_ROWS, CONV_ROWS)
        for ct in range(CONV_CH // 512):
            cols = slice(ct * 512, (ct + 1) * 512)
            u = u_ref[pl.ds(r0, CONV_ROWS), cols]
            pos = lax.broadcasted_iota(jnp.int32, u.shape, 0) % seg_len
            acc = jnp.zeros_like(u) + cb_ref[:, cols]
            for j in range(CONV_K):
                off = j - CONV_K // 2
                shifted = u if off == 0 else pltpu.roll(u, (-off) % CONV_ROWS, 0)
                valid = jnp.logical_and(pos + off >= 0, pos + off < seg_len)
                acc = acc + jnp.where(valid, shifted, 0.0) * cw_ref[j:j + 1, cols]
            o_ref[pl.ds(r0, CONV_ROWS), col0 + ct * 512:col0 + (ct + 1) * 512] = _silu(acc).astype(BF16)
        return carry
    lax.fori_loop(0, u_ref.shape[0] // CONV_ROWS, body, 0)


def _inproj_kernel(x_ref, mod_ref, g_ref, w_ref, ws_ref, cw_ref, cb_ref, o_ref, os_ref, h_ref, u_ref, *,
                   conv_col0, seg_len):
    j = pl.program_id(1)
    last = pl.num_programs(1) - 1

    @pl.when(j == 0)
    def _():
        _adaln_rows(x_ref, h_ref, g_ref, mod_ref, 3)
        os_ref[...] = _dot(h_ref[...], ws_ref[...])

    acc = _dot(h_ref[...], w_ref[...])

    @pl.when(j != last)
    def _():
        o_ref[...] = acc.astype(BF16)

    @pl.when(j == last)
    def _():
        o_ref[:, :conv_col0] = acc[:, :conv_col0].astype(BF16)
        u_ref[...] = acc[:, conv_col0:]
        _conv_rows(u_ref, o_ref, cw_ref, cb_ref, conv_col0, seg_len)


def _inproj(x, mod, g, w_main, w_small, conv_w, conv_b, *, seg_len, tm, tn):
    t = x.shape[0]
    rows_per_mod = t // mod.shape[0]
    n_tiles = MAIN_W // tn
    conv_col0 = MAIN_W - CONV_CH - (n_tiles - 1) * tn
    assert n_tiles * tn == MAIN_W and conv_col0 >= 0 and conv_col0 % 128 == 0
    assert tm % CONV_ROWS == 0 and CONV_ROWS % seg_len == 0
    return pl.pallas_call(
        functools.partial(_inproj_kernel, conv_col0=conv_col0, seg_len=seg_len),
        grid=(t // tm, n_tiles),
        in_specs=[pl.BlockSpec((tm, D_MODEL), lambda i, j: (i, 0)),
                  pl.BlockSpec((None, N_MOD, D_MODEL), lambda i, j: ((i * tm) // rows_per_mod, 0, 0)),
                  pl.BlockSpec((1, D_MODEL), lambda i, j: (0, 0)),
                  pl.BlockSpec((D_MODEL, tn), lambda i, j: (0, j)),
                  pl.BlockSpec((D_MODEL, SMALL_W), lambda i, j: (0, 0)),
                  pl.BlockSpec((CONV_K, CONV_CH), lambda i, j: (0, 0)),
                  pl.BlockSpec((1, CONV_CH), lambda i, j: (0, 0))],
        out_specs=[pl.BlockSpec((tm, tn), lambda i, j: (i, j)),
                   pl.BlockSpec((tm, SMALL_W), lambda i, j: (i, 0))],
        out_shape=[jax.ShapeDtypeStruct((t, MAIN_W), BF16),
                   jax.ShapeDtypeStruct((t, SMALL_W), F32)],
        scratch_shapes=[pltpu.VMEM((tm, D_MODEL), BF16),
                        pltpu.VMEM((tm, CONV_CH), F32)],
        compiler_params=_cparams(2, BIG_VMEM_LIMIT),
        name="inproj",
    )(x, mod, g, w_main, w_small, conv_w, conv_b)


def _tri_mask(direction):
    row = lax.broadcasted_iota(jnp.int32, (CHUNK, CHUNK), 0)
    col = lax.broadcasted_iota(jnp.int32, (CHUNK, CHUNK), 1)
    return (col <= row) if direction == 0 else (col >= row)


def _scan_specs(nc, width, col):
    return [pl.BlockSpec((CHUNK, width), lambda b, c: (b * nc + c, col)),
            pl.BlockSpec((CHUNK, width), lambda b, c: (b * nc + nc - 1 - c, col))]


def _mlstm_chain(q_ref, k_ref, v_ref, sm_ref, gb_ref, c_s, n_s, m_s, d):
    hs, new_c, new_n, new_m = [], [], [], []
    tri = _tri_mask(d)
    lane = lax.broadcasted_iota(jnp.int32, (CHUNK, SMALL_W), 1)
    g = sm_ref[...] + gb_ref[...]
    is_fg = jnp.logical_and(lane >= LANE_FG, lane < LANE_DT)
    p = jnp.where(is_fg, _log_sigmoid(g), g)
    cum = _dot_exact(tri.astype(F32), p)
    p_t = p.T
    cum_t = cum.T
    last = CHUNK - 1 if d == 0 else 0

    for h in range(ML_HEADS):
        li_lane = LANE_IG + d * ML_HEADS + h
        lf_lane = LANE_FG + d * ML_HEADS + h
        b_col = cum[:, lf_lane:lf_lane + 1]
        b_row = cum_t[lf_lane:lf_lane + 1, :]
        li_col = p[:, li_lane:li_lane + 1]
        li_row = p_t[li_lane:li_lane + 1, :]
        b_last = cum[last:last + 1, lf_lane:lf_lane + 1]
        row = d * ML_HEADS + h
        m_prev = m_s[row:row + 1, 0:1]

        log_d = jnp.where(tri, b_col - b_row + li_row, -jnp.inf)
        m_inter = b_col + m_prev
        m_t = jnp.maximum(m_inter, jnp.max(log_d, axis=1, keepdims=True))
        dmat = jnp.exp(log_d - m_t)

        sl = slice(h * ML_DH, (h + 1) * ML_DH)
        qh = q_ref[:, sl]
        kh = (k_ref[:, sl].astype(F32) * (ML_DH ** -0.5))
        vh = v_ref[:, sl]
        scores = _dot_nt(qh, kh.astype(BF16)) * dmat
        inter = jnp.exp(m_inter - m_t)
        c_old = c_s[:, row * ML_DH:(row + 1) * ML_DH]
        n_old = n_s[row:row + 1, :]
        num = _dot(scores.astype(BF16), vh) + inter * _dot(qh, c_old.astype(BF16))
        qn = jnp.sum(qh.astype(F32) * n_old, axis=1, keepdims=True)
        den = jnp.sum(scores, axis=1, keepdims=True) + inter * qn
        hs.append(num / jnp.maximum(jnp.abs(den), jnp.exp(-m_t)))

        log_w = b_last - b_col + li_col
        m_end = b_last + m_prev
        m_new = jnp.maximum(m_end, jnp.max(log_w, axis=0, keepdims=True))
        w = jnp.exp(log_w - m_new)
        decay = jnp.exp(m_end - m_new)
        kw = kh * w
        new_c.append(decay * c_old + _dot_tn(kw.astype(BF16), vh))
        new_n.append(decay * n_old + jnp.sum(kw, axis=0, keepdims=True))
        new_m.append(jnp.broadcast_to(m_new, (1, SMALL_W)))
    return jnp.concatenate(hs, axis=1), new_c, new_n, new_m


def _mlstm_finish(hsum, og_ref, mlg_ref):
    outs = []
    for h in range(ML_HEADS):
        sl = slice(h * ML_DH, (h + 1) * ML_DH)
        hs = hsum[:, sl]
        cen = hs - jnp.mean(hs, axis=1, keepdims=True)
        var = jnp.mean(cen * cen, axis=1, keepdims=True)
        hn = cen * lax.rsqrt(var + EPS) * mlg_ref[:, sl]
        outs.append((jax.nn.sigmoid(og_ref[:, sl].astype(F32)) * hn).astype(BF16))
    return jnp.concatenate(outs, axis=1)


def _mlstm_kernel(*refs, zero_init, nc):
    refs = list(refs)
    qf, qb, kf, kb, vf, vb, of, ob, smf, smb, gb_ref, mlg_ref = refs[:12]
    pos = 12
    if not zero_init:
        c0_ref, n0_ref, m0_ref = refs[pos:pos + 3]
        pos += 3
    h_out_ref, c_out_ref, n_out_ref, m_out_ref, c_s, n_s, m_s, hcur, stash = refs[pos:]

    c = pl.program_id(1)

    @pl.when(c == 0)
    def _():
        if zero_init:
            c_s[...] = jnp.zeros_like(c_s)
            n_s[...] = jnp.zeros_like(n_s)
            m_s[...] = jnp.zeros_like(m_s)
        else:
            for r in range(2 * ML_HEADS):
                c_s[:, r * ML_DH:(r + 1) * ML_DH] = c0_ref[r // ML_HEADS, r % ML_HEADS]
            n_s[...] = jnp.concatenate([n0_ref[0], n0_ref[1]], axis=0)
            m_s[...] = jnp.concatenate([m0_ref[0], m0_ref[1]], axis=0)

    h_f, c_f, n_f, m_f = _mlstm_chain(qf, kf, vf, smf, gb_ref, c_s, n_s, m_s, 0)
    h_b, c_b, n_b, m_b = _mlstm_chain(qb, kb, vb, smb, gb_ref, c_s, n_s, m_s, 1)
    hcur[...] = jnp.concatenate([h_f, h_b], axis=1)
    c_s[...] = jnp.concatenate(c_f + c_b, axis=1)
    n_s[...] = jnp.concatenate(n_f + n_b, axis=0)
    m_s[...] = jnp.concatenate(m_f + m_b, axis=0)

    half = nc // 2

    @pl.when(c < half)
    def _():
        stash[c] = hcur[...]

    @pl.when(c >= half)
    def _():
        s = nc - 1 - c
        h_out_ref[c] = _mlstm_finish(hcur[:, :ML_W] + stash[s, :, ML_W:], of, mlg_ref)
        h_out_ref[s] = _mlstm_finish(stash[s, :, :ML_W] + hcur[:, ML_W:], ob, mlg_ref)

    @pl.when(c == nc - 1)
    def _():
        for r in range(2 * ML_HEADS):
            c_out_ref[r // ML_HEADS, r % ML_HEADS] = c_s[:, r * ML_DH:(r + 1) * ML_DH]
        n_out_ref[0] = n_s[:ML_HEADS, :]
        n_out_ref[1] = n_s[ML_HEADS:, :]
        m_out_ref[0] = m_s[:ML_HEADS, :]
        m_out_ref[1] = m_s[ML_HEADS:, :]


def _mlstm(p_main, small, gb_row, mlg, state, *, bsz, seq):
    nc = seq // CHUNK
    assert nc % 2 == 0
    zero_init = state is None
    const = lambda b, c: (0, 0)
    in_specs = (_scan_specs(nc, ML_W, COL_Q) + _scan_specs(nc, ML_W, COL_K) + _scan_specs(nc, ML_W, COL_V)
                + _scan_specs(nc, ML_W, COL_O) + _scan_specs(nc, SMALL_W, 0)
                + [pl.BlockSpec((1, SMALL_W), const), pl.BlockSpec((1, ML_W), const)])
    args = [p_main] * 8 + [small, small, gb_row, mlg]
    if not zero_init:
        in_specs += [pl.BlockSpec((None, 2, ML_HEADS, ML_DH, ML_DH), lambda b, c: (b, 0, 0, 0, 0)),
                     pl.BlockSpec((None, 2, ML_HEADS, ML_DH), lambda b, c: (b, 0, 0, 0)),
                     pl.BlockSpec((None, 2, ML_HEADS, SMALL_W), lambda b, c: (b, 0, 0, 0))]
        args += list(state)
    out_specs = [pl.BlockSpec((nc, CHUNK, ML_W), lambda b, c: (b, 0, 0)),
                 pl.BlockSpec((None, 2, ML_HEADS, ML_DH, ML_DH), lambda b, c: (b, 0, 0, 0, 0)),
                 pl.BlockSpec((None, 2, ML_HEADS, ML_DH), lambda b, c: (b, 0, 0, 0)),
                 pl.BlockSpec((None, 2, ML_HEADS, SMALL_W), lambda b, c: (b, 0, 0, 0))]
    out_shape = [jax.ShapeDtypeStruct((bsz * nc, CHUNK, ML_W), BF16),
                 jax.ShapeDtypeStruct((bsz, 2, ML_HEADS, ML_DH, ML_DH), F32),
                 jax.ShapeDtypeStruct((bsz, 2, ML_HEADS, ML_DH), F32),
                 jax.ShapeDtypeStruct((bsz, 2, ML_HEADS, SMALL_W), F32)]
    return pl.pallas_call(
        functools.partial(_mlstm_kernel, zero_init=zero_init, nc=nc),
        grid=(bsz, nc),
        in_specs=in_specs,
        out_specs=out_specs,
        out_shape=out_shape,
        scratch_shapes=[pltpu.VMEM((ML_DH, 2 * ML_W), F32),
                        pltpu.VMEM((2 * ML_HEADS, ML_DH), F32),
                        pltpu.VMEM((2 * ML_HEADS, SMALL_W), F32),
                        pltpu.VMEM((CHUNK, 2 * ML_W), F32),
                        pltpu.VMEM((nc // 2, CHUNK, 2 * ML_W), F32)],
        compiler_params=_cparams(2),
        name="mlstm",
    )(*args)


def _ssd_chain(xs_ref, bc_ref, sm_ref, gb_ref, arow_ref, st_s, d):
    ys, new_st = [], []
    tri = _tri_mask(d)
    lane = lax.broadcasted_iota(jnp.int32, (CHUNK, SMALL_W), 1)
    g = sm_ref[...] + gb_ref[...]
    is_dt = jnp.logical_and(lane >= LANE_DT, lane < LANE_DT + 2 * SSM_HEADS)
    p = jnp.where(is_dt, _softplus(g), g)
    cum = _dot_exact(tri.astype(F32), p * arow_ref[...])
    p_t = p.T
    cum_t = cum.T
    last = CHUNK - 1 if d == 0 else 0
    base = LANE_DT + d * SSM_HEADS

    a16 = cum[:, base:base + SSM_HEADS]
    dt16 = p[:, base:base + SSM_HEADS]
    a_last = cum[last:last + 1, base:base + SSM_HEADS]
    to_end = jnp.exp(a_last - a16) * dt16
    ea16 = jnp.exp(a16)
    expand = (lax.broadcasted_iota(jnp.int32, (SSM_HEADS, SSM_W), 1) // SSM_P
              == lax.broadcasted_iota(jnp.int32, (SSM_HEADS, SSM_W), 0)).astype(F32)
    to_end_x = _dot_exact(to_end, expand)
    ea_last_x = _dot_exact(jnp.broadcast_to(jnp.exp(a_last), (8, SSM_HEADS)), expand)[0:1, :]

    lo = lax.broadcasted_iota(jnp.int32, (CHUNK, 2 * SSM_P), 1) < SSM_P
    zero_b = jnp.zeros((CHUNK, 2 * SSM_P), BF16)

    for gi in range(SSM_G):
        bg = bc_ref[:, gi * SSM_N:(gi + 1) * SSM_N]
        cg = bc_ref[:, (SSM_G + gi) * SSM_N:(SSM_G + gi + 1) * SSM_N]
        cb = _dot_nt(cg, bg)
        cg32 = cg.astype(F32)
        for pr in range(SSM_R // 2):
            h0 = gi * SSM_R + 2 * pr
            lhs = []
            for hh in (h0, h0 + 1):
                a_col = a16[:, hh:hh + 1]
                a_row = cum_t[base + hh:base + hh + 1, :]
                dt_row = p_t[base + hh:base + hh + 1, :]
                decay = jnp.exp(jnp.where(tri, a_col - a_row, -jnp.inf))
                lhs.append((cb * decay * dt_row).astype(BF16))
            for hh in (h0, h0 + 1):
                lhs.append((cg32 * ea16[:, hh:hh + 1]).astype(BF16))
            cols = slice(h0 * SSM_P, (h0 + 2) * SSM_P)
            xp = xs_ref[:, cols]
            sp = st_s[:, d * SSM_W + h0 * SSM_P:d * SSM_W + (h0 + 2) * SSM_P].astype(BF16)
            rhs = jnp.concatenate([jnp.where(lo, xp, zero_b), jnp.where(lo, zero_b, xp),
                                   jnp.where(lo, sp, zero_b), jnp.where(lo, zero_b, sp)], axis=0)
            ys.append(_dot(jnp.concatenate(lhs, axis=1), rhs))

        gcols = slice(gi * SSM_R * SSM_P, (gi + 1) * SSM_R * SSM_P)
        xw = (xs_ref[:, gcols].astype(F32) * to_end_x[:, gcols]).astype(BF16)
        st_old = st_s[:, d * SSM_W + gi * SSM_R * SSM_P:d * SSM_W + (gi + 1) * SSM_R * SSM_P]
        new_st.append(st_old * ea_last_x[:, gcols] + _dot_tn(bg, xw))
    return jnp.concatenate(ys, axis=1), jnp.concatenate(new_st, axis=1)


def _ssd_finish(ysum, xs_ref, z_ref, dsk_ref, ng_ref):
    y = ysum + dsk_ref[...] * xs_ref[...].astype(F32)
    yz = y * _silu(z_ref[...].astype(F32))
    return (yz * _rms(yz) * ng_ref[...]).astype(BF16)


def _ssd_kernel(*refs, zero_init, nc):
    refs = list(refs)
    xf, xb, bcf, bcb, zf, zb, smf, smb, gb_ref, arow_ref, dsk_ref, ng_ref = refs[:12]
    pos = 12
    if not zero_init:
        s0_ref = refs[pos]
        pos += 1
    y_out_ref, s_out_ref, st_s, ycur, stash = refs[pos:]

    c = pl.program_id(1)

    @pl.when(c == 0)
    def _():
        if zero_init:
            st_s[...] = jnp.zeros_like(st_s)
        else:
            st_s[:, :SSM_W] = s0_ref[0].T
            st_s[:, SSM_W:] = s0_ref[1].T

    y_f, st_f = _ssd_chain(xf, bcf, smf, gb_ref, arow_ref, st_s, 0)
    y_b, st_b = _ssd_chain(xb, bcb, smb, gb_ref, arow_ref, st_s, 1)
    ycur[...] = jnp.concatenate([y_f, y_b], axis=1)
    st_s[...] = jnp.concatenate([st_f, st_b], axis=1)

    half = nc // 2

    @pl.when(c < half)
    def _():
        stash[c] = ycur[...]

    @pl.when(c >= half)
    def _():
        s = nc - 1 - c
        y_out_ref[c] = _ssd_finish(ycur[:, :SSM_W] + stash[s, :, SSM_W:], xf, zf, dsk_ref, ng_ref)
        y_out_ref[s] = _ssd_finish(stash[s, :, :SSM_W] + ycur[:, SSM_W:], xb, zb, dsk_ref, ng_ref)

    @pl.when(c == nc - 1)
    def _():
        s_out_ref[0] = st_s[:, :SSM_W].T
        s_out_ref[1] = st_s[:, SSM_W:].T


def _ssd(p_main, small, gb_row, a_row, dsk, ng, state, *, bsz, seq):
    nc = seq // CHUNK
    assert nc % 2 == 0
    zero_init = state is None
    const = lambda b, c: (0, 0)
    in_specs = (_scan_specs(nc, SSM_W, COL_XS) + _scan_specs(nc, BC_W, COL_BC) + _scan_specs(nc, SSM_W, COL_Z)
                + _scan_specs(nc, SMALL_W, 0)
                + [pl.BlockSpec((1, SMALL_W), const), pl.BlockSpec((1, SMALL_W), const),
                   pl.BlockSpec((1, SSM_W), const), pl.BlockSpec((1, SSM_W), const)])
    args = [p_main] * 6 + [small, small, gb_row, a_row, dsk, ng]
    if not zero_init:
        in_specs.append(pl.BlockSpec((None, 2, SSM_W, SSM_N), lambda b, c: (b, 0, 0, 0)))
        args.append(state)
    return pl.pallas_call(
        functools.partial(_ssd_kernel, zero_init=zero_init, nc=nc),
        grid=(bsz, nc),
        in_specs=in_specs,
        out_specs=[pl.BlockSpec((nc, CHUNK, SSM_W), lambda b, c: (b, 0, 0)),
                   pl.BlockSpec((None, 2, SSM_W, SSM_N), lambda b, c: (b, 0, 0, 0))],
        out_shape=[jax.ShapeDtypeStruct((bsz * nc, CHUNK, SSM_W), BF16),
                   jax.ShapeDtypeStruct((bsz, 2, SSM_W, SSM_N), F32)],
        scratch_shapes=[pltpu.VMEM((SSM_N, 2 * SSM_W), F32),
                        pltpu.VMEM((CHUNK, 2 * SSM_W), F32),
                        pltpu.VMEM((nc // 2, CHUNK, 2 * SSM_W), F32)],
        compiler_params=_cparams(2),
        name="ssd",
    )(*args)


def _outproj_kernel(hml_ref, y_ref, x_ref, mod_ref, g_ref, w_ref, o_ref):
    o_ref[...] = _dot(hml_ref[...], w_ref[0:ML_W, :]) + _dot(y_ref[...], w_ref[ML_W:ML_W + SSM_W, :])
    _residual_rows(x_ref, o_ref, o_ref, g_ref, mod_ref, 5, 1.0)


def _outproj(hml, y, x, mod, g, w, *, tm):
    t = x.shape[0]
    rows_per_mod = t // mod.shape[0]
    return pl.pallas_call(
        _outproj_kernel,
        grid=(t // tm,),
        in_specs=[pl.BlockSpec((tm, ML_W), lambda i: (i, 0)),
                  pl.BlockSpec((tm, SSM_W), lambda i: (i, 0)),
                  pl.BlockSpec((tm, D_MODEL), lambda i: (i, 0)),
                  pl.BlockSpec((None, N_MOD, D_MODEL), lambda i: ((i * tm) // rows_per_mod, 0, 0)),
                  pl.BlockSpec((1, D_MODEL), lambda i: (0, 0)),
                  pl.BlockSpec((ML_W + SSM_W, D_MODEL), lambda i: (0, 0))],
        out_specs=pl.BlockSpec((tm, D_MODEL), lambda i: (i, 0)),
        out_shape=jax.ShapeDtypeStruct((t, D_MODEL), F32),
        compiler_params=_cparams(1),
        name="outproj",
    )(hml, y, x, mod, g, w)


def _prepare_params(norm_g, w_in, gate_bias, dt_bias, a_log, d_skip, conv_w, conv_b, ml_norm_g, ssm_norm_g,
                    w_out, w_gate, w_up, w_down):
    n_gate = 4 * ML_W
    s0 = n_gate + 4 * ML_HEADS
    s1 = s0 + SSM_W + CONV_CH
    w_main = jnp.concatenate([w_in[:, :n_gate], w_in[:, s0:s1]], axis=1).astype(BF16)
    w_small = jnp.concatenate([w_in[:, n_gate:s0], w_in[:, s1:],
                               jnp.zeros((D_MODEL, SMALL_W - N_SMALL), F32)], axis=1).astype(BF16)
    gb_row = jnp.concatenate([gate_bias.reshape(-1), dt_bias.reshape(-1),
                              jnp.zeros((SMALL_W - N_SMALL,), F32)]).reshape(1, SMALL_W)
    a_row = jnp.concatenate([jnp.zeros((LANE_DT,), F32), -jnp.exp(a_log.reshape(-1)),
                             jnp.zeros((SMALL_W - N_SMALL,), F32)]).reshape(1, SMALL_W)
    return dict(
        g=[norm_g[i].reshape(1, D_MODEL) for i in range(6)],
        w_main=w_main, w_small=w_small, gb_row=gb_row, a_row=a_row,
        dsk=jnp.repeat(d_skip, SSM_P).reshape(1, SSM_W),
        conv_w=conv_w, conv_b=conv_b.reshape(1, CONV_CH),
        mlg=ml_norm_g.reshape(1, ML_W), ng=ssm_norm_g.reshape(1, SSM_W),
        w_out=_cast_bf16(w_out), wg=_cast_bf16(w_gate), wu=_cast_bf16(w_up), wd=_cast_bf16(w_down),
    )


def _trunk_path(x, mod, p, ml_state, ssm_state, seg_len, *, tm_ffn=512, tf=512, tm_proj=512, tn_proj=3328):
    bsz, seq, _ = x.shape
    t = bsz * seq
    x = x.reshape(t, D_MODEL)
    x = _ffn(x, mod, p['g'][0], p['g'][1], p['wg'], p['wu'], p['wd'], idx=0, tm=tm_ffn, tf=tf)
    p_main, small = _inproj(x, mod, p['g'][2], p['w_main'], p['w_small'], p['conv_w'], p['conv_b'],
                            seg_len=seg_len, tm=tm_proj, tn=tn_proj)
    if ml_state is not None:
        c0, n0, m0 = ml_state
        ml_state = (c0, n0, jnp.broadcast_to(m0[..., None], m0.shape + (SMALL_W,)))
        ssm_state = ssm_state.reshape(bsz, 2, SSM_W, SSM_N)
    hml, new_c, new_n, new_m = _mlstm(p_main, small, p['gb_row'], p['mlg'], ml_state, bsz=bsz, seq=seq)
    y, new_s = _ssd(p_main, small, p['gb_row'], p['a_row'], p['dsk'], p['ng'], ssm_state, bsz=bsz, seq=seq)
    x = _outproj(hml.reshape(t, ML_W), y.reshape(t, SSM_W), x, mod, p['g'][3], p['w_out'], tm=tm_proj)
    x = _ffn(x, mod, p['g'][4], p['g'][5], p['wg'], p['wu'], p['wd'], idx=1, tm=tm_ffn, tf=tf)
    return (x.reshape(bsz, seq, D_MODEL), new_c, new_n, new_m[..., 0],
            new_s.reshape(bsz, 2, SSM_HEADS, SSM_P, SSM_N))


def kernel(x_prompt, x_sample, state_mlstm_C, state_mlstm_n, state_mlstm_m, state_ssm, c, c_ctx, w_ada, b_ada,
           norm_g, w_in, gate_bias, dt_bias, a_log, d_skip, conv_w, conv_b, ml_norm_g, ssm_norm_g, w_out,
           ffn_w_gate, ffn_w_up, ffn_w_down):
    depth = w_in.shape[0]
    bd = x_sample.shape[0]
    y_p, y_s = x_prompt, x_sample
    out_c, out_n, out_m, out_s = [], [], [], []
    cvec = jnp.concatenate([c_ctx[None], c, jnp.zeros((8 - 1 - bd, D_MODEL), F32)], axis=0)
    for l in range(depth):
        p = _prepare_params(norm_g[l], w_in[l], gate_bias[l], dt_bias[l], a_log[l], d_skip[l], conv_w[l], conv_b[l],
                            ml_norm_g[l], ssm_norm_g[l], w_out[l], ffn_w_gate[l], ffn_w_up[l], ffn_w_down[l])
        mod = _ada(cvec, w_ada[l], b_ada[l].reshape(1, -1)).reshape(8, N_MOD, D_MODEL)
        y_p, cn, nn, mn, sn = _trunk_path(y_p, mod[0:1], p, None, None, x_prompt.shape[1])
        lat_state = (state_mlstm_C[:, l], state_mlstm_n[:, l], state_mlstm_m[:, l])
        y_s, _, _, _, _ = _trunk_path(y_s, mod[1:1 + bd], p, lat_state, state_ssm[:, l], GRID_W)
        out_c.append(cn)
        out_n.append(nn)
        out_m.append(mn)
        out_s.append(sn)
    return (y_p, y_s, jnp.stack(out_c, axis=1), jnp.stack(out_n, axis=1), jnp.stack(out_m, axis=1),
            jnp.stack(out_s, axis=1))
```

```python
import functools

import jax
import jax.numpy as jnp
from jax import lax
from jax.experimental import pallas as pl
from jax.experimental.pallas import tpu as pltpu

F32 = jnp.float32
BF16 = jnp.bfloat16

D_MODEL = 2048
GRID_W = 64
ML_HEADS = 4
ML_W = 1024
ML_DH = ML_W // ML_HEADS
SSM_W = 1024
SSM_P = 64
SSM_HEADS = SSM_W // SSM_P
SSM_N = 128
SSM_G = 2
SSM_R = SSM_HEADS // SSM_G
CONV_K = 5
CONV_CH = SSM_W + 2 * SSM_G * SSM_N
BC_W = 2 * SSM_G * SSM_N
D_FF = 5632
CHUNK = 128
N_MOD = 9
EPS = 1e-6

MAIN_W = 4 * ML_W + SSM_W + CONV_CH
COL_Q, COL_K, COL_V, COL_O, COL_Z, COL_XS = 0, 1, 2, 3, 4, 5
COL_BC = (MAIN_W - BC_W) // BC_W
SMALL_W = 128
BLK_IG, BLK_FG, BLK_DT = 0, 1, 2
N_BLK = 3

VMEM_LIMIT = 48 * 1024 * 1024
BIG_VMEM_LIMIT = 60 * 1024 * 1024
ROW_CHUNK = 128
CONV_ROWS = 256


def _cparams(n_axes, vmem=VMEM_LIMIT):
    return pltpu.CompilerParams(dimension_semantics=("arbitrary",) * n_axes, vmem_limit_bytes=vmem)


def _dot(a, b):
    return jnp.dot(a, b, preferred_element_type=F32)


def _dot_nt(a, b):
    return lax.dot_general(a, b, (((1,), (1,)), ((), ())), preferred_element_type=F32)


def _dot_tn(a, b):
    return lax.dot_general(a, b, (((0,), (0,)), ((), ())), preferred_element_type=F32)


def _split_bf16(x, terms):
    parts, rest = [], x
    for i in range(terms):
        piece = rest.astype(BF16)
        parts.append(piece)
        if i + 1 < terms:
            rest = rest - piece.astype(F32)
    return parts


def _select_dot(x, sel, terms):
    return _dot(jnp.concatenate(_split_bf16(x, terms), axis=1), jnp.concatenate([sel] * terms, axis=0))


def _scan_sum(tri, x):
    tri_b = jnp.where(tri, 1.0, 0.0).astype(BF16)
    return _dot(jnp.concatenate([tri_b] * 3, axis=1), jnp.concatenate(_split_bf16(x, 3), axis=0))


def _lane_selector(first_lane, n_blocks, width):
    row = lax.broadcasted_iota(jnp.int32, (SMALL_W, n_blocks * width), 0)
    col = lax.broadcasted_iota(jnp.int32, (SMALL_W, n_blocks * width), 1)
    return jnp.where(row == first_lane + col // width, 1.0, 0.0).astype(BF16)


def _silu(x):
    return x * jax.nn.sigmoid(x)


def _softplus(x):
    return jnp.maximum(x, 0.0) + jnp.log(1.0 + jnp.exp(-jnp.abs(x)))


def _log_sigmoid(x):
    return jnp.minimum(x, 0.0) - jnp.log(1.0 + jnp.exp(-jnp.abs(x)))


def _rms(x):
    return lax.rsqrt(jnp.mean(x * x, axis=-1, keepdims=True) + EPS)


def _adaln_rows(x_ref, h_ref, g_ref, mod_ref, mod_base):
    def body(i, carry):
        r0 = pl.multiple_of(i * ROW_CHUNK, ROW_CHUNK)
        x = x_ref[pl.ds(r0, ROW_CHUNK), :]
        shift = mod_ref[mod_base:mod_base + 1, :]
        scale = mod_ref[mod_base + 1:mod_base + 2, :]
        h = (x * _rms(x)) * g_ref[...] * (1.0 + scale) + shift
        h_ref[pl.ds(r0, ROW_CHUNK), :] = h.astype(BF16)
        return carry
    lax.fori_loop(0, x_ref.shape[0] // ROW_CHUNK, body, 0)


def _residual_rows(x_ref, y_ref, o_ref, g_ref, mod_ref, gate_row, gate_mul, hn_ref=None, gn_ref=None, next_base=None):
    def body(i, carry):
        r0 = pl.multiple_of(i * ROW_CHUNK, ROW_CHUNK)
        y = y_ref[pl.ds(r0, ROW_CHUNK), :]
        gate = mod_ref[gate_row:gate_row + 1, :]
        upd = gate_mul * gate * ((y * _rms(y)) * g_ref[...])
        o = x_ref[pl.ds(r0, ROW_CHUNK), :] + upd
        o_ref[pl.ds(r0, ROW_CHUNK), :] = o
        if hn_ref is not None:
            shift = mod_ref[next_base:next_base + 1, :]
            scale = mod_ref[next_base + 1:next_base + 2, :]
            hn_ref[pl.ds(r0, ROW_CHUNK), :] = ((o * _rms(o)) * gn_ref[...] * (1.0 + scale) + shift).astype(BF16)
        return carry
    lax.fori_loop(0, x_ref.shape[0] // ROW_CHUNK, body, 0)


def _cast_kernel(x_ref, o_ref):
    o_ref[...] = x_ref[...].astype(BF16)


def _cast_bf16(w):
    shape = w.shape
    w2 = w.reshape(-1, shape[-1])
    rows = w2.shape[0]
    tr = max(8, min(rows, (8 * 1024 * 1024) // (4 * shape[-1]) // 256 * 256))
    while rows % tr:
        tr //= 2
    out = pl.pallas_call(
        _cast_kernel,
        grid=(rows // tr,),
        in_specs=[pl.BlockSpec((tr, shape[-1]), lambda i: (i, 0))],
        out_specs=pl.BlockSpec((tr, shape[-1]), lambda i: (i, 0)),
        out_shape=jax.ShapeDtypeStruct(w2.shape, BF16),
        compiler_params=_cparams(1),
        name="cast_bf16",
    )(w2)
    return out.reshape(shape)


def _ada_kernel(c_ref, w_ref, b_ref, o_ref):
    s = _silu(c_ref[...]).astype(BF16)
    o_ref[...] = _dot(s, w_ref[...].astype(BF16)) + b_ref[...]


def _ada(cv, w, b):
    n = w.shape[1]
    tn = 1024
    return pl.pallas_call(
        _ada_kernel,
        grid=(n // tn,),
        in_specs=[pl.BlockSpec((cv.shape[0], D_MODEL), lambda j: (0, 0)),
                  pl.BlockSpec((D_MODEL, tn), lambda j: (0, j)),
                  pl.BlockSpec((1, tn), lambda j: (0, j))],
        out_specs=pl.BlockSpec((cv.shape[0], tn), lambda j: (0, j)),
        out_shape=jax.ShapeDtypeStruct((cv.shape[0], n), F32),
        compiler_params=_cparams(1),
        name="ada_mod",
    )(cv, w, b)


def _ffn_kernel(x_ref, mod_ref, gin_ref, gout_ref, wg_ref, wu_ref, wd_ref, *rest, mod_base, first):
    if first:
        o_ref, hn_ref, h_ref = rest
    else:
        h_ref, o_ref = rest
    j = pl.program_id(1)

    @pl.when(j == 0)
    def _():
        if first:
            _adaln_rows(x_ref, h_ref, gin_ref.at[0:1], mod_ref, mod_base)
        o_ref[...] = jnp.zeros_like(o_ref)

    h = h_ref[...]
    a = (_silu(_dot(h, wg_ref[...])) * _dot(h, wu_ref[...])).astype(BF16)
    o_ref[...] += _dot(a, wd_ref[...])

    @pl.when(j == pl.num_programs(1) - 1)
    def _():
        if first:
            _residual_rows(x_ref, o_ref, o_ref, gout_ref, mod_ref, mod_base + 2, 0.5, hn_ref, gin_ref.at[1:2], 3)
        else:
            _residual_rows(x_ref, o_ref, o_ref, gout_ref, mod_ref, mod_base + 2, 0.5)


def _ffn(x, h, mod, g_in, g_out, wg, wu, wd, *, idx, tm, tf):
    t = x.shape[0]
    rows_per_mod = t // mod.shape[0]
    first = h is None
    row_block = pl.BlockSpec((tm, D_MODEL), lambda i, j: (i, 0))
    in_specs = [row_block,
                pl.BlockSpec((None, N_MOD, D_MODEL), lambda i, j: ((i * tm) // rows_per_mod, 0, 0)),
                pl.BlockSpec(g_in.shape, lambda i, j: (0, 0)),
                pl.BlockSpec((1, D_MODEL), lambda i, j: (0, 0)),
                pl.BlockSpec((None, D_MODEL, tf), lambda i, j: (idx, 0, j)),
                pl.BlockSpec((None, D_MODEL, tf), lambda i, j: (idx, 0, j)),
                pl.BlockSpec((None, tf, D_MODEL), lambda i, j: (idx, j, 0))]
    args = [x, mod, g_in, g_out, wg, wu, wd]
    if first:
        out_specs = [row_block, row_block]
        out_shape = [jax.ShapeDtypeStruct((t, D_MODEL), F32), jax.ShapeDtypeStruct((t, D_MODEL), BF16)]
        scratch = [pltpu.VMEM((tm, D_MODEL), BF16)]
    else:
        in_specs.append(row_block)
        args.append(h)
        out_specs = row_block
        out_shape = jax.ShapeDtypeStruct((t, D_MODEL), F32)
        scratch = []
    return pl.pallas_call(
        functools.partial(_ffn_kernel, mod_base=6 * idx, first=first),
        grid=(t // tm, D_FF // tf),
        in_specs=in_specs,
        out_specs=out_specs,
        out_shape=out_shape,
        scratch_shapes=scratch,
        compiler_params=_cparams(2),
        name="ffn",
    )(*args)


def _conv_rows(u_ref, o_ref, cw_ref, cb_ref, col0, seg_len):
    def body(i, carry):
        r0 = pl.multiple_of(i * CONV_ROWS, CONV_ROWS)
        for ct in range(CONV_CH // 512):
            cols = slice(ct * 512, (ct + 1) * 512)
            u = u_ref[pl.ds(r0, CONV_ROWS), cols]
            pos = lax.broadcasted_iota(jnp.int32, u.shape, 0) % seg_len
            acc = jnp.zeros_like(u) + cb_ref[:, cols]
            for j in range(CONV_K):
                off = j - CONV_K // 2
                shifted = u if off == 0 else pltpu.roll(u, (-off) % CONV_ROWS, 0)
                valid = jnp.logical_and(pos + off >= 0, pos + off < seg_len)
                acc = acc + jnp.where(valid, shifted, 0.0) * cw_ref[j:j + 1, cols]
            o_ref[pl.ds(r0, CONV_ROWS), col0 + ct * 512:col0 + (ct + 1) * 512] = _silu(acc).astype(BF16)
        return carry
    lax.fori_loop(0, u_ref.shape[0] // CONV_ROWS, body, 0)


def _inproj_kernel(h_ref, w_ref, ws_ref, cw_ref, cb_ref, o_ref, os_ref, u_ref, *, conv_col0, seg_len):
    j = pl.program_id(1)
    last = pl.num_programs(1) - 1

    @pl.when(j == 0)
    def _():
        os_ref[...] = _dot(h_ref[...], ws_ref[...])

    acc = _dot(h_ref[...], w_ref[...])

    @pl.when(j != last)
    def _():
        o_ref[...] = acc.astype(BF16)

    @pl.when(j == last)
    def _():
        o_ref[:, :conv_col0] = acc[:, :conv_col0].astype(BF16)
        u_ref[...] = acc[:, conv_col0:]
        _conv_rows(u_ref, o_ref, cw_ref, cb_ref, conv_col0, seg_len)


def _inproj(h, w_main, w_small, conv_w, conv_b, *, seg_len, tm, tn):
    t = h.shape[0]
    n_tiles = MAIN_W // tn
    conv_col0 = MAIN_W - CONV_CH - (n_tiles - 1) * tn
    assert n_tiles * tn == MAIN_W and conv_col0 >= 0 and conv_col0 % 128 == 0
    assert tm % CONV_ROWS == 0 and CONV_ROWS % seg_len == 0
    return pl.pallas_call(
        functools.partial(_inproj_kernel, conv_col0=conv_col0, seg_len=seg_len),
        grid=(t // tm, n_tiles),
        in_specs=[pl.BlockSpec((tm, D_MODEL), lambda i, j: (i, 0)),
                  pl.BlockSpec((D_MODEL, tn), lambda i, j: (0, j)),
                  pl.BlockSpec((D_MODEL, N_BLK * SMALL_W), lambda i, j: (0, 0)),
                  pl.BlockSpec((CONV_K, CONV_CH), lambda i, j: (0, 0)),
                  pl.BlockSpec((1, CONV_CH), lambda i, j: (0, 0))],
        out_specs=[pl.BlockSpec((tm, tn), lambda i, j: (i, j)),
                   pl.BlockSpec((tm, N_BLK * SMALL_W), lambda i, j: (i, 0))],
        out_shape=[jax.ShapeDtypeStruct((t, MAIN_W), BF16),
                   jax.ShapeDtypeStruct((t, N_BLK * SMALL_W), F32)],
        scratch_shapes=[pltpu.VMEM((tm, CONV_CH), F32)],
        compiler_params=_cparams(2, BIG_VMEM_LIMIT),
        name="inproj",
    )(h, w_main, w_small, conv_w, conv_b)


def _tri_mask(direction):
    row = lax.broadcasted_iota(jnp.int32, (CHUNK, CHUNK), 0)
    col = lax.broadcasted_iota(jnp.int32, (CHUNK, CHUNK), 1)
    return (col <= row) if direction == 0 else (col >= row)


def _scan_specs(nc, width, col):
    return [pl.BlockSpec((CHUNK, width), lambda b, c: (b * nc + c, col)),
            pl.BlockSpec((CHUNK, width), lambda b, c: (b * nc + nc - 1 - c, col))]


def _scan_max(x, d):
    idx = lax.broadcasted_iota(jnp.int32, x.shape, 0)
    k = 1
    while k < CHUNK:
        if d == 0:
            shifted, valid = pltpu.roll(x, k, 0), idx >= k
        else:
            shifted, valid = pltpu.roll(x, CHUNK - k, 0), idx < CHUNK - k
        x = jnp.maximum(x, jnp.where(valid, shifted, -jnp.inf))
        k *= 2
    return x


def _lane_bcast(x, lane):
    return jnp.broadcast_to(x[:, lane:lane + 1], (x.shape[0], SMALL_W))


ML_AUG = ML_DH + SMALL_W
M_ROWS = 8


def _mlstm_chain(q_ref, k_ref, v_ref, ga_ref, gf_ref, bi_ref, bf_ref, cn_s, m_s, d):
    tri = _tri_mask(d)
    scale = ML_DH ** -0.5
    log_i = ga_ref[...] + bi_ref[...]
    log_f = _log_sigmoid(gf_ref[...] + bf_ref[...])
    b = _scan_sum(tri, log_f)
    g = log_i - b
    m_prev = m_s[d:d + 1, :]
    m_inter = b + m_prev
    m_t = jnp.maximum(m_inter, b + _scan_max(g, d))
    c1 = b - m_t
    inter = jnp.exp(m_inter - m_t)
    e_negm = jnp.exp(-m_t)
    last = CHUNK - 1 if d == 0 else 0
    b_last = b[last:last + 1, :]
    log_w = b_last - b + log_i
    m_end = b_last + m_prev
    m_new = jnp.maximum(m_end, jnp.max(log_w, axis=0, keepdims=True))
    w = jnp.exp(log_w - m_new) * scale
    decay = jnp.exp(m_end - m_new)
    g_t = g.T
    ones = jnp.ones((CHUNK, SMALL_W), BF16)

    hs, new_cn = [], []
    for h in range(ML_HEADS):
        lane = d * ML_HEADS + h
        log_d = jnp.where(tri, _lane_bcast(c1, lane) + g_t[lane:lane + 1, :], -jnp.inf)
        dmat = jnp.exp(log_d) * scale
        sl = slice(h * ML_DH, (h + 1) * ML_DH)
        qh = q_ref[:, sl]
        kh = k_ref[:, sl]
        v_aug = jnp.concatenate([v_ref[:, sl], ones], axis=1)
        scores = (_dot_nt(qh, kh) * dmat).astype(BF16)
        cn_old = cn_s[:, lane * ML_AUG:(lane + 1) * ML_AUG]
        inter_rep = _lane_bcast(inter, lane)
        r = (_dot(scores, v_aug)
             + jnp.concatenate([inter_rep] * (ML_AUG // SMALL_W), axis=1) * _dot(qh, cn_old.astype(BF16)))
        rden = 1.0 / jnp.maximum(jnp.abs(r[:, ML_DH:]), _lane_bcast(e_negm, lane))
        hs.append(r[:, :ML_DH] * jnp.concatenate([rden] * (ML_DH // SMALL_W), axis=1))

        w_rep = _lane_bcast(w, lane)
        kw = (kh.astype(F32) * jnp.concatenate([w_rep] * (ML_DH // SMALL_W), axis=1)).astype(BF16)
        new_cn.append(decay[:, lane:lane + 1] * cn_old + _dot_tn(kw, v_aug))
    return jnp.concatenate(hs, axis=1), new_cn, m_new


def _mlstm_finish(hsum, og_ref, mlg_ref):
    outs = []
    for h in range(ML_HEADS):
        sl = slice(h * ML_DH, (h + 1) * ML_DH)
        hs = hsum[:, sl]
        cen = hs - jnp.mean(hs, axis=1, keepdims=True)
        var = jnp.mean(cen * cen, axis=1, keepdims=True)
        hn = cen * lax.rsqrt(var + EPS) * mlg_ref[:, sl]
        outs.append((jax.nn.sigmoid(og_ref[:, sl].astype(F32)) * hn).astype(BF16))
    return jnp.concatenate(outs, axis=1)


def _mlstm_kernel(*refs, zero_init, nc):
    refs = list(refs)
    qf, qb, kf, kb, vf, vb, of, ob, gaf, gab, gff, gfb, bi_ref, bf_ref, mlg_ref = refs[:15]
    pos = 15
    if not zero_init:
        c0_ref, n0_ref, m0_ref = refs[pos:pos + 3]
        pos += 3
    h_out_ref, c_out_ref, n_out_ref, m_out_ref, cn_s, m_s, hcur, stash = refs[pos:]

    c = pl.program_id(1)

    @pl.when(c == 0)
    def _():
        if zero_init:
            cn_s[...] = jnp.zeros_like(cn_s)
            m_s[...] = jnp.zeros_like(m_s)
        else:
            for r in range(2 * ML_HEADS):
                d, h = divmod(r, ML_HEADS)
                cn_s[:, r * ML_AUG:r * ML_AUG + ML_DH] = c0_ref[d, h]
                n_rows = jnp.broadcast_to(n0_ref[d, h:h + 1, :], (SMALL_W, ML_DH))
                cn_s[:, r * ML_AUG + ML_DH:(r + 1) * ML_AUG] = n_rows.T
            m_s[...] = m0_ref[...]

    h_f, cn_f, m_f = _mlstm_chain(qf, kf, vf, gaf, gff, bi_ref, bf_ref, cn_s, m_s, 0)
    h_b, cn_b, m_b = _mlstm_chain(qb, kb, vb, gab, gfb, bi_ref, bf_ref, cn_s, m_s, 1)
    hcur[...] = jnp.concatenate([h_f, h_b], axis=1)
    cn_s[...] = jnp.concatenate(cn_f + cn_b, axis=1)
    m_s[...] = jnp.concatenate([m_f, m_b] + [jnp.zeros_like(m_f)] * (M_ROWS - 2), axis=0)

    half = nc // 2

    @pl.when(c < half)
    def _():
        stash[c] = hcur[...]

    @pl.when(c >= half)
    def _():
        s = nc - 1 - c
        h_out_ref[c] = _mlstm_finish(hcur[:, :ML_W] + stash[s, :, ML_W:], of, mlg_ref)
        h_out_ref[s] = _mlstm_finish(stash[s, :, :ML_W] + hcur[:, ML_W:], ob, mlg_ref)

    @pl.when(c == nc - 1)
    def _():
        for r in range(2 * ML_HEADS):
            d, h = divmod(r, ML_HEADS)
            c_out_ref[d, h] = cn_s[:, r * ML_AUG:r * ML_AUG + ML_DH]
            n_out_ref[d, h:h + 1, :] = cn_s[:, r * ML_AUG + ML_DH:(r + 1) * ML_AUG].T[0:1, :]
        m_out_ref[...] = m_s[...]


def _mlstm(p_main, small, bi_row, bf_row, mlg, state, *, bsz, seq):
    nc = seq // CHUNK
    assert nc % 2 == 0
    zero_init = state is None
    const = lambda b, c: (0, 0)
    in_specs = (_scan_specs(nc, ML_W, COL_Q) + _scan_specs(nc, ML_W, COL_K) + _scan_specs(nc, ML_W, COL_V)
                + _scan_specs(nc, ML_W, COL_O) + _scan_specs(nc, SMALL_W, BLK_IG) + _scan_specs(nc, SMALL_W, BLK_FG)
                + [pl.BlockSpec((1, SMALL_W), const), pl.BlockSpec((1, SMALL_W), const),
                   pl.BlockSpec((1, ML_W), const)])
    args = [p_main] * 8 + [small] * 4 + [bi_row, bf_row, mlg]
    if not zero_init:
        in_specs += [pl.BlockSpec((None, 2, ML_HEADS, ML_DH, ML_DH), lambda b, c: (b, 0, 0, 0, 0)),
                     pl.BlockSpec((None, 2, ML_HEADS, ML_DH), lambda b, c: (b, 0, 0, 0)),
                     pl.BlockSpec((None, M_ROWS, SMALL_W), lambda b, c: (b, 0, 0))]
        args += list(state)
    out_specs = [pl.BlockSpec((nc, CHUNK, ML_W), lambda b, c: (b, 0, 0)),
                 pl.BlockSpec((None, 2, ML_HEADS, ML_DH, ML_DH), lambda b, c: (b, 0, 0, 0, 0)),
                 pl.BlockSpec((None, 2, ML_HEADS, ML_DH), lambda b, c: (b, 0, 0, 0)),
                 pl.BlockSpec((None, M_ROWS, SMALL_W), lambda b, c: (b, 0, 0))]
    out_shape = [jax.ShapeDtypeStruct((bsz * nc, CHUNK, ML_W), BF16),
                 jax.ShapeDtypeStruct((bsz, 2, ML_HEADS, ML_DH, ML_DH), F32),
                 jax.ShapeDtypeStruct((bsz, 2, ML_HEADS, ML_DH), F32),
                 jax.ShapeDtypeStruct((bsz, M_ROWS, SMALL_W), F32)]
    return pl.pallas_call(
        functools.partial(_mlstm_kernel, zero_init=zero_init, nc=nc),
        grid=(bsz, nc),
        in_specs=in_specs,
        out_specs=out_specs,
        out_shape=out_shape,
        scratch_shapes=[pltpu.VMEM((ML_DH, 2 * ML_HEADS * ML_AUG), F32),
                        pltpu.VMEM((M_ROWS, SMALL_W), F32),
                        pltpu.VMEM((CHUNK, 2 * ML_W), F32),
                        pltpu.VMEM((nc // 2, CHUNK, 2 * ML_W), F32)],
        compiler_params=_cparams(2),
        name="mlstm",
    )(*args)


def _ssd_chain(xs_ref, bc_ref, sm_ref, gb_ref, arow_ref, st_s, d):
    ys, new_st = [], []
    tri = _tri_mask(d)
    p = _softplus(sm_ref[...] + gb_ref[...])
    cum = _scan_sum(tri, p * arow_ref[...])
    p_t = p.T
    cum_t = cum.T
    last = CHUNK - 1 if d == 0 else 0
    lane0 = d * SSM_HEADS
    a_last = cum[last:last + 1, :]

    sel_p = _lane_selector(lane0, SSM_HEADS, SSM_P)
    to_end_x = _select_dot(jnp.exp(a_last - cum) * p, sel_p, 2)
    ea_x = _select_dot(jnp.exp(cum), sel_p, 2)
    ea_last_x = _select_dot(jnp.broadcast_to(jnp.exp(a_last), (8, SMALL_W)), sel_p, 3)[0:1, :]

    lo = lax.broadcasted_iota(jnp.int32, (CHUNK, 2 * SSM_P), 1) < SSM_P
    zero_b = jnp.zeros((CHUNK, 2 * SSM_P), BF16)

    for gi in range(SSM_G):
        bg = bc_ref[:, gi * SSM_N:(gi + 1) * SSM_N]
        cg = bc_ref[:, (SSM_G + gi) * SSM_N:(SSM_G + gi + 1) * SSM_N]
        cb = _dot_nt(cg, bg)
        gcols = slice(gi * SSM_R * SSM_P, (gi + 1) * SSM_R * SSM_P)
        st_old = st_s[:, d * SSM_W + gi * SSM_R * SSM_P:d * SSM_W + (gi + 1) * SSM_R * SSM_P]
        y_pairs = []
        for pr in range(SSM_R // 2):
            h0 = gi * SSM_R + 2 * pr
            lhs = []
            for hh in (h0, h0 + 1):
                lane = lane0 + hh
                seg = _lane_bcast(cum, lane) - cum_t[lane:lane + 1, :]
                decay = jnp.exp(jnp.where(tri, seg, -jnp.inf))
                lhs.append((cb * decay * p_t[lane:lane + 1, :]).astype(BF16))
            xp = xs_ref[:, h0 * SSM_P:(h0 + 2) * SSM_P]
            rhs = jnp.concatenate([jnp.where(lo, xp, zero_b), jnp.where(lo, zero_b, xp)], axis=0)
            y_pairs.append(_dot(jnp.concatenate(lhs, axis=1), rhs))
        y_carry = ea_x[:, gcols] * _dot(cg, st_old.astype(BF16))
        ys.append(jnp.concatenate(y_pairs, axis=1) + y_carry)

        xw = (xs_ref[:, gcols].astype(F32) * to_end_x[:, gcols]).astype(BF16)
        new_st.append(st_old * ea_last_x[:, gcols] + _dot_tn(bg, xw))
    return jnp.concatenate(ys, axis=1), jnp.concatenate(new_st, axis=1)


def _ssd_finish(ysum, xs_ref, z_ref, dsk_ref, ng_ref):
    y = ysum + dsk_ref[...] * xs_ref[...].astype(F32)
    yz = y * _silu(z_ref[...].astype(F32))
    return (yz * _rms(yz) * ng_ref[...]).astype(BF16)


def _ssd_kernel(*refs, zero_init, nc):
    refs = list(refs)
    xf, xb, bcf, bcb, zf, zb, smf, smb, gb_ref, arow_ref, dsk_ref, ng_ref = refs[:12]
    pos = 12
    if not zero_init:
        s0_ref = refs[pos]
        pos += 1
    y_out_ref, s_out_ref, st_s, ycur, stash = refs[pos:]

    c = pl.program_id(1)

    @pl.when(c == 0)
    def _():
        if zero_init:
            st_s[...] = jnp.zeros_like(st_s)
        else:
            st_s[:, :SSM_W] = s0_ref[0].T
            st_s[:, SSM_W:] = s0_ref[1].T

    y_f, st_f = _ssd_chain(xf, bcf, smf, gb_ref, arow_ref, st_s, 0)
    y_b, st_b = _ssd_chain(xb, bcb, smb, gb_ref, arow_ref, st_s, 1)
    ycur[...] = jnp.concatenate([y_f, y_b], axis=1)
    st_s[...] = jnp.concatenate([st_f, st_b], axis=1)

    half = nc // 2

    @pl.when(c < half)
    def _():
        stash[c] = ycur[...]

    @pl.when(c >= half)
    def _():
        s = nc - 1 - c
        y_out_ref[c] = _ssd_finish(ycur[:, :SSM_W] + stash[s, :, SSM_W:], xf, zf, dsk_ref, ng_ref)
        y_out_ref[s] = _ssd_finish(stash[s, :, :SSM_W] + ycur[:, SSM_W:], xb, zb, dsk_ref, ng_ref)

    @pl.when(c == nc - 1)
    def _():
        s_out_ref[0] = st_s[:, :SSM_W].T
        s_out_ref[1] = st_s[:, SSM_W:].T


def _ssd(p_main, small, gb_row, a_row, dsk, ng, state, *, bsz, seq):
    nc = seq // CHUNK
    assert nc % 2 == 0
    zero_init = state is None
    const = lambda b, c: (0, 0)
    in_specs = (_scan_specs(nc, SSM_W, COL_XS) + _scan_specs(nc, BC_W, COL_BC) + _scan_specs(nc, SSM_W, COL_Z)
                + _scan_specs(nc, SMALL_W, BLK_DT)
                + [pl.BlockSpec((1, SMALL_W), const), pl.BlockSpec((1, SMALL_W), const),
                   pl.BlockSpec((1, SSM_W), const), pl.BlockSpec((1, SSM_W), const)])
    args = [p_main] * 6 + [small, small, gb_row, a_row, dsk, ng]
    if not zero_init:
        in_specs.append(pl.BlockSpec((None, 2, SSM_W, SSM_N), lambda b, c: (b, 0, 0, 0)))
        args.append(state)
    return pl.pallas_call(
        functools.partial(_ssd_kernel, zero_init=zero_init, nc=nc),
        grid=(bsz, nc),
        in_specs=in_specs,
        out_specs=[pl.BlockSpec((nc, CHUNK, SSM_W), lambda b, c: (b, 0, 0)),
                   pl.BlockSpec((None, 2, SSM_W, SSM_N), lambda b, c: (b, 0, 0, 0))],
        out_shape=[jax.ShapeDtypeStruct((bsz * nc, CHUNK, SSM_W), BF16),
                   jax.ShapeDtypeStruct((bsz, 2, SSM_W, SSM_N), F32)],
        scratch_shapes=[pltpu.VMEM((SSM_N, 2 * SSM_W), F32),
                        pltpu.VMEM((CHUNK, 2 * SSM_W), F32),
                        pltpu.VMEM((nc // 2, CHUNK, 2 * SSM_W), F32)],
        compiler_params=_cparams(2),
        name="ssd",
    )(*args)


def _outproj_kernel(hml_ref, y_ref, x_ref, mod_ref, g_ref, gn_ref, w_ref, o_ref, hn_ref):
    o_ref[...] = _dot(hml_ref[...], w_ref[0:ML_W, :]) + _dot(y_ref[...], w_ref[ML_W:ML_W + SSM_W, :])
    _residual_rows(x_ref, o_ref, o_ref, g_ref, mod_ref, 5, 1.0, hn_ref, gn_ref, 6)


def _outproj(hml, y, x, mod, g, g_next, w, *, tm):
    t = x.shape[0]
    rows_per_mod = t // mod.shape[0]
    row_block = pl.BlockSpec((tm, D_MODEL), lambda i: (i, 0))
    return pl.pallas_call(
        _outproj_kernel,
        grid=(t // tm,),
        in_specs=[pl.BlockSpec((tm, ML_W), lambda i: (i, 0)),
                  pl.BlockSpec((tm, SSM_W), lambda i: (i, 0)),
                  row_block,
                  pl.BlockSpec((None, N_MOD, D_MODEL), lambda i: ((i * tm) // rows_per_mod, 0, 0)),
                  pl.BlockSpec((1, D_MODEL), lambda i: (0, 0)),
                  pl.BlockSpec((1, D_MODEL), lambda i: (0, 0)),
                  pl.BlockSpec((ML_W + SSM_W, D_MODEL), lambda i: (0, 0))],
        out_specs=[row_block, row_block],
        out_shape=[jax.ShapeDtypeStruct((t, D_MODEL), F32), jax.ShapeDtypeStruct((t, D_MODEL), BF16)],
        compiler_params=_cparams(1),
        name="outproj",
    )(hml, y, x, mod, g, g_next, w)


def _prepare_params(norm_g, w_in, gate_bias, dt_bias, a_log, d_skip, conv_w, conv_b, ml_norm_g, ssm_norm_g,
                    w_out, w_gate, w_up, w_down):
    n_gate = 4 * ML_W
    s0 = n_gate + 4 * ML_HEADS
    s1 = s0 + SSM_W + CONV_CH
    w_main = jnp.concatenate([w_in[:, :n_gate], w_in[:, s0:s1]], axis=1).astype(BF16)
    n_ig = 2 * ML_HEADS

    def lane_block(cols):
        return jnp.concatenate([cols, jnp.zeros(cols.shape[:-1] + (SMALL_W - cols.shape[-1],), F32)], axis=-1)

    w_small = jnp.concatenate([lane_block(w_in[:, n_gate:n_gate + n_ig]), lane_block(w_in[:, n_gate + n_ig:s0]),
                               lane_block(w_in[:, s1:])], axis=1).astype(BF16)
    return dict(
        g=[norm_g[i].reshape(1, D_MODEL) for i in range(6)],
        g_first=jnp.stack([norm_g[0], norm_g[2]], axis=0),
        w_main=w_main, w_small=w_small,
        bi_row=lane_block(gate_bias[0].reshape(1, -1)), bf_row=lane_block(gate_bias[1].reshape(1, -1)),
        bd_row=lane_block(dt_bias.reshape(1, -1)), a_row=lane_block(-jnp.exp(a_log.reshape(1, -1))),
        dsk=jnp.repeat(d_skip, SSM_P).reshape(1, SSM_W),
        conv_w=conv_w, conv_b=conv_b.reshape(1, CONV_CH),
        mlg=ml_norm_g.reshape(1, ML_W), ng=ssm_norm_g.reshape(1, SSM_W),
        w_out=_cast_bf16(w_out), wg=_cast_bf16(w_gate), wu=_cast_bf16(w_up), wd=_cast_bf16(w_down),
    )


def _trunk_path(x, mod, p, ml_state, ssm_state, seg_len, *, tm_ffn=512, tf=512, tm_proj=512, tn_proj=3328):
    bsz, seq, _ = x.shape
    t = bsz * seq
    x = x.reshape(t, D_MODEL)
    x, h = _ffn(x, None, mod, p['g_first'], p['g'][1], p['wg'], p['wu'], p['wd'], idx=0, tm=tm_ffn, tf=tf)
    p_main, small = _inproj(h, p['w_main'], p['w_small'], p['conv_w'], p['conv_b'],
                            seg_len=seg_len, tm=tm_proj, tn=tn_proj)
    if ml_state is not None:
        c0, n0, m0 = ml_state
        pad = SMALL_W - 2 * ML_HEADS
        m0 = jnp.stack([jnp.pad(m0[:, 0], ((0, 0), (0, pad + ML_HEADS))),
                        jnp.pad(m0[:, 1], ((0, 0), (ML_HEADS, pad)))], axis=1)
        m0 = jnp.pad(m0, ((0, 0), (0, M_ROWS - 2), (0, 0)))
        ml_state = (c0, n0, m0)
        ssm_state = ssm_state.reshape(bsz, 2, SSM_W, SSM_N)
    hml, new_c, new_n, new_m = _mlstm(p_main, small, p['bi_row'], p['bf_row'], p['mlg'], ml_state, bsz=bsz, seq=seq)
    y, new_s = _ssd(p_main, small, p['bd_row'], p['a_row'], p['dsk'], p['ng'], ssm_state, bsz=bsz, seq=seq)
    x, h = _outproj(hml.reshape(t, ML_W), y.reshape(t, SSM_W), x, mod, p['g'][3], p['g'][4], p['w_out'], tm=tm_proj)
    x = _ffn(x, h, mod, p['g'][4], p['g'][5], p['wg'], p['wu'], p['wd'], idx=1, tm=tm_ffn, tf=tf)
    new_m = new_m[:, :2, :2 * ML_HEADS].reshape(bsz, 2, 2, ML_HEADS)
    new_m = jnp.sum(jnp.where(jnp.eye(2, dtype=bool)[None, :, :, None], new_m, 0.0), axis=2)
    return (x.reshape(bsz, seq, D_MODEL), new_c, new_n, new_m,
            new_s.reshape(bsz, 2, SSM_HEADS, SSM_P, SSM_N))


def kernel(x_prompt, x_sample, state_mlstm_C, state_mlstm_n, state_mlstm_m, state_ssm, c, c_ctx, w_ada, b_ada,
           norm_g, w_in, gate_bias, dt_bias, a_log, d_skip, conv_w, conv_b, ml_norm_g, ssm_norm_g, w_out,
           ffn_w_gate, ffn_w_up, ffn_w_down):
    depth = w_in.shape[0]
    bd = x_sample.shape[0]
    y_p, y_s = x_prompt, x_sample
    out_c, out_n, out_m, out_s = [], [], [], []
    cvec = jnp.concatenate([c_ctx[None], c, jnp.zeros((8 - 1 - bd, D_MODEL), F32)], axis=0)
    for l in range(depth):
        p = _prepare_params(norm_g[l], w_in[l], gate_bias[l], dt_bias[l], a_log[l], d_skip[l], conv_w[l], conv_b[l],
                            ml_norm_g[l], ssm_norm_g[l], w_out[l], ffn_w_gate[l], ffn_w_up[l], ffn_w_down[l])
        mod = _ada(cvec, w_ada[l], b_ada[l].reshape(1, -1)).reshape(8, N_MOD, D_MODEL)
        y_p, cn, nn, mn, sn = _trunk_path(y_p, mod[0:1], p, None, None, x_prompt.shape[1])
        lat_state = (state_mlstm_C[:, l], state_mlstm_n[:, l], state_mlstm_m[:, l])
        y_s, _, _, _, _ = _trunk_path(y_s, mod[1:1 + bd], p, lat_state, state_ssm[:, l], GRID_W)
        out_c.append(cn)
        out_n.append(nn)
        out_m.append(mn)
        out_s.append(sn)
    return (y_p, y_s, jnp.stack(out_c, axis=1), jnp.stack(out_n, axis=1), jnp.stack(out_m, axis=1),
            jnp.stack(out_s, axis=1))
```

```python
import functools

import jax
import jax.numpy as jnp
from jax import lax
from jax.experimental import pallas as pl
from jax.experimental.pallas import tpu as pltpu

F32 = jnp.float32
BF16 = jnp.bfloat16

D_MODEL = 2048
GRID_W = 64
ML_HEADS = 4
ML_W = 1024
ML_DH = ML_W // ML_HEADS
SSM_W = 1024
SSM_P = 64
SSM_HEADS = SSM_W // SSM_P
SSM_N = 128
SSM_G = 2
SSM_R = SSM_HEADS // SSM_G
CONV_K = 5
CONV_CH = SSM_W + 2 * SSM_G * SSM_N
BC_W = 2 * SSM_G * SSM_N
D_FF = 5632
CHUNK = 128
N_MOD = 9
EPS = 1e-6

MAIN_W = 4 * ML_W + SSM_W + CONV_CH
COL_Q, COL_K, COL_V, COL_O, COL_Z, COL_XS = 0, 1, 2, 3, 4, 5
COL_BC = (MAIN_W - BC_W) // BC_W
SMALL_W = 128
BLK_IG, BLK_FG, BLK_DT = 0, 1, 2
N_BLK = 3

VMEM_LIMIT = 48 * 1024 * 1024
BIG_VMEM_LIMIT = 60 * 1024 * 1024
ROW_CHUNK = 128
ROW_GROUP = 256
CONV_ROWS = 256
CONV_TILE = 512


def _cparams(n_axes, vmem=VMEM_LIMIT):
    return pltpu.CompilerParams(dimension_semantics=("arbitrary",) * n_axes, vmem_limit_bytes=vmem)


def _dot(a, b):
    return jnp.dot(a, b, preferred_element_type=F32)


def _dot_nt(a, b):
    return lax.dot_general(a, b, (((1,), (1,)), ((), ())), preferred_element_type=F32)


def _dot_tn(a, b):
    return lax.dot_general(a, b, (((0,), (0,)), ((), ())), preferred_element_type=F32)


def _split_bf16(x, terms):
    parts, rest = [], x
    for i in range(terms):
        piece = rest.astype(BF16)
        parts.append(piece)
        if i + 1 < terms:
            rest = rest - piece.astype(F32)
    return parts


def _select_dot(x, sel, terms):
    return _dot(jnp.concatenate(_split_bf16(x, terms), axis=1), jnp.concatenate([sel] * terms, axis=0))


def _scan_sum(tri, x):
    tri_b = jnp.where(tri, 1.0, 0.0).astype(BF16)
    return _dot(jnp.concatenate([tri_b] * 3, axis=1), jnp.concatenate(_split_bf16(x, 3), axis=0))


def _lane_selector(first_lane, n_blocks, width):
    row = lax.broadcasted_iota(jnp.int32, (SMALL_W, n_blocks * width), 0)
    col = lax.broadcasted_iota(jnp.int32, (SMALL_W, n_blocks * width), 1)
    return jnp.where(row == first_lane + col // width, 1.0, 0.0).astype(BF16)


def _silu(x):
    return x * jax.nn.sigmoid(x)


def _softplus(x):
    return jnp.maximum(x, 0.0) + jnp.log(1.0 + jnp.exp(-jnp.abs(x)))


def _log_sigmoid(x):
    return jnp.minimum(x, 0.0) - jnp.log(1.0 + jnp.exp(-jnp.abs(x)))


def _rms(x):
    return lax.rsqrt(jnp.mean(x * x, axis=-1, keepdims=True) + EPS)


def _adaln_rows(x_ref, h_ref, g_ref, mod_ref, mod_base):
    def body(i, carry):
        r0 = pl.multiple_of(i * ROW_CHUNK, ROW_CHUNK)
        x = x_ref[pl.ds(r0, ROW_CHUNK), :]
        shift = mod_ref[mod_base:mod_base + 1, :]
        scale = mod_ref[mod_base + 1:mod_base + 2, :]
        h = (x * _rms(x)) * g_ref[...] * (1.0 + scale) + shift
        h_ref[pl.ds(r0, ROW_CHUNK), :] = h.astype(BF16)
        return carry
    lax.fori_loop(0, x_ref.shape[0] // ROW_CHUNK, body, 0)


def _residual_rows(x_ref, y_ref, o_ref, g_ref, mod_ref, gate_row, gate_mul, hn_ref=None, gn_ref=None, next_base=None):
    def body(i, carry):
        r0 = pl.multiple_of(i * ROW_CHUNK, ROW_CHUNK)
        y = y_ref[pl.ds(r0, ROW_CHUNK), :]
        gate = mod_ref[gate_row:gate_row + 1, :]
        upd = gate_mul * gate * ((y * _rms(y)) * g_ref[...])
        o = x_ref[pl.ds(r0, ROW_CHUNK), :] + upd
        o_ref[pl.ds(r0, ROW_CHUNK), :] = o
        if hn_ref is not None:
            shift = mod_ref[next_base:next_base + 1, :]
            scale = mod_ref[next_base + 1:next_base + 2, :]
            hn_ref[pl.ds(r0, ROW_CHUNK), :] = ((o * _rms(o)) * gn_ref[...] * (1.0 + scale) + shift).astype(BF16)
        return carry
    lax.fori_loop(0, x_ref.shape[0] // ROW_CHUNK, body, 0)


def _cast_kernel(x_ref, o_ref):
    o_ref[...] = x_ref[...].astype(BF16)


def _cast_bf16(w):
    shape = w.shape
    w2 = w.reshape(-1, shape[-1])
    rows = w2.shape[0]
    tr = max(8, min(rows, (8 * 1024 * 1024) // (4 * shape[-1]) // 256 * 256))
    while rows % tr:
        tr //= 2
    out = pl.pallas_call(
        _cast_kernel,
        grid=(rows // tr,),
        in_specs=[pl.BlockSpec((tr, shape[-1]), lambda i: (i, 0))],
        out_specs=pl.BlockSpec((tr, shape[-1]), lambda i: (i, 0)),
        out_shape=jax.ShapeDtypeStruct(w2.shape, BF16),
        compiler_params=_cparams(1),
        name="cast_bf16",
    )(w2)
    return out.reshape(shape)


def _lane_block(cols):
    return jnp.concatenate([cols, jnp.zeros(cols.shape[:-1] + (SMALL_W - cols.shape[-1],), cols.dtype)], axis=-1)


W_IN_GATES = 4 * ML_W
W_IN_Z = W_IN_GATES + 4 * ML_HEADS
W_IN_DT = W_IN_Z + SSM_W + CONV_CH
W_IN_COLS = W_IN_DT + 2 * SSM_HEADS


def _split_w_in_kernel(w_ref, om_ref, os_ref):
    om_ref[:, :W_IN_GATES] = w_ref[:, :W_IN_GATES].astype(BF16)
    om_ref[:, W_IN_GATES:] = w_ref[:, W_IN_Z:W_IN_DT].astype(BF16)
    n_ig = 2 * ML_HEADS
    os_ref[...] = jnp.concatenate([_lane_block(w_ref[:, W_IN_GATES:W_IN_GATES + n_ig]),
                                   _lane_block(w_ref[:, W_IN_GATES + n_ig:W_IN_Z]),
                                   _lane_block(w_ref[:, W_IN_DT:])], axis=1).astype(BF16)


def _split_w_in(w_in):
    tr = 256
    return pl.pallas_call(
        _split_w_in_kernel,
        grid=(D_MODEL // tr,),
        in_specs=[pl.BlockSpec((tr, W_IN_COLS), lambda i: (i, 0))],
        out_specs=[pl.BlockSpec((tr, MAIN_W), lambda i: (i, 0)),
                   pl.BlockSpec((tr, N_BLK * SMALL_W), lambda i: (i, 0))],
        out_shape=[jax.ShapeDtypeStruct((D_MODEL, MAIN_W), BF16),
                   jax.ShapeDtypeStruct((D_MODEL, N_BLK * SMALL_W), BF16)],
        compiler_params=_cparams(1),
        name="split_w_in",
    )(w_in)


def _ada_kernel(c_ref, w_ref, b_ref, o_ref):
    s = _silu(c_ref[...]).astype(BF16)
    o_ref[...] = _dot(s, w_ref[...].astype(BF16)) + b_ref[...]


def _ada(cv, w, b):
    n = w.shape[1]
    tn = 1024
    return pl.pallas_call(
        _ada_kernel,
        grid=(n // tn,),
        in_specs=[pl.BlockSpec((cv.shape[0], D_MODEL), lambda j: (0, 0)),
                  pl.BlockSpec((D_MODEL, tn), lambda j: (0, j)),
                  pl.BlockSpec((1, tn), lambda j: (0, j))],
        out_specs=pl.BlockSpec((cv.shape[0], tn), lambda j: (0, j)),
        out_shape=jax.ShapeDtypeStruct((cv.shape[0], n), F32),
        compiler_params=_cparams(1),
        name="ada_mod",
    )(cv, w, b)


def _ffn_kernel(x_ref, mod_ref, gin_ref, gout_ref, wg_ref, wu_ref, wd_ref, *rest, mod_base, first):
    if first:
        o_ref, hn_ref, h_ref = rest
    else:
        h_ref, o_ref = rest
    j = pl.program_id(1)
    last = pl.num_programs(1) - 1
    groups = [slice(r * ROW_GROUP, (r + 1) * ROW_GROUP) for r in range(x_ref.shape[0] // ROW_GROUP)]

    def swiglu(h):
        a = (_silu(_dot(h, wg_ref[...])) * _dot(h, wu_ref[...])).astype(BF16)
        return _dot(a, wd_ref[...])

    @pl.when(j == 0)
    def _():
        if first:
            shift = mod_ref[mod_base:mod_base + 1, :]
            scale1 = 1.0 + mod_ref[mod_base + 1:mod_base + 2, :]
            for rows in groups:
                x = x_ref[rows, :]
                h = ((x * _rms(x)) * gin_ref[0:1, :] * scale1 + shift).astype(BF16)
                h_ref[rows, :] = h
                o_ref[rows, :] = swiglu(h)
        else:
            o_ref[...] = swiglu(h_ref[...])

    @pl.when(jnp.logical_and(j > 0, j < last))
    def _():
        o_ref[...] += swiglu(h_ref[...])

    @pl.when(j == last)
    def _():
        gate = 0.5 * mod_ref[mod_base + 2:mod_base + 3, :]
        for rows in groups:
            y = o_ref[rows, :] + swiglu(h_ref[rows, :])
            o = x_ref[rows, :] + gate * ((y * _rms(y)) * gout_ref[...])
            o_ref[rows, :] = o
            if first:
                hn = (o * _rms(o)) * gin_ref[1:2, :] * (1.0 + mod_ref[4:5, :]) + mod_ref[3:4, :]
                hn_ref[rows, :] = hn.astype(BF16)


def _ffn(x, h, mod, g_in, g_out, wg, wu, wd, *, idx, tm, tf):
    t = x.shape[0]
    rows_per_mod = t // mod.shape[0]
    first = h is None
    row_block = pl.BlockSpec((tm, D_MODEL), lambda i, j: (i, 0))
    in_specs = [row_block,
                pl.BlockSpec((None, N_MOD, D_MODEL), lambda i, j: ((i * tm) // rows_per_mod, 0, 0)),
                pl.BlockSpec(g_in.shape, lambda i, j: (0, 0)),
                pl.BlockSpec((1, D_MODEL), lambda i, j: (0, 0)),
                pl.BlockSpec((None, D_MODEL, tf), lambda i, j: (idx, 0, j)),
                pl.BlockSpec((None, D_MODEL, tf), lambda i, j: (idx, 0, j)),
                pl.BlockSpec((None, tf, D_MODEL), lambda i, j: (idx, j, 0))]
    args = [x, mod, g_in, g_out, wg, wu, wd]
    if first:
        out_specs = [row_block, row_block]
        out_shape = [jax.ShapeDtypeStruct((t, D_MODEL), F32), jax.ShapeDtypeStruct((t, D_MODEL), BF16)]
        scratch = [pltpu.VMEM((tm, D_MODEL), BF16)]
    else:
        in_specs.append(row_block)
        args.append(h)
        out_specs = row_block
        out_shape = jax.ShapeDtypeStruct((t, D_MODEL), F32)
        scratch = []
    return pl.pallas_call(
        functools.partial(_ffn_kernel, mod_base=6 * idx, first=first),
        grid=(t // tm, D_FF // tf),
        in_specs=in_specs,
        out_specs=out_specs,
        out_shape=out_shape,
        scratch_shapes=scratch,
        compiler_params=_cparams(2),
        name="ffn",
    )(*args)


def _conv_silu(u, cw_ref, cb_ref, cols, seg_len):
    rows = u.shape[0]
    pos = lax.broadcasted_iota(jnp.int32, u.shape, 0) % seg_len
    acc = jnp.zeros_like(u) + cb_ref[:, cols]
    for j in range(CONV_K):
        off = j - CONV_K // 2
        shifted = u if off == 0 else pltpu.roll(u, (-off) % rows, 0)
        valid = jnp.logical_and(pos + off >= 0, pos + off < seg_len)
        acc = acc + jnp.where(valid, shifted, 0.0) * cw_ref[j:j + 1, cols]
    return _silu(acc)


def _inproj_kernel(h_ref, w_ref, ws_ref, cw_ref, cb_ref, o_ref, os_ref, *, conv_col0, seg_len):
    j = pl.program_id(1)
    last = pl.num_programs(1) - 1

    @pl.when(j != last)
    def _():
        o_ref[...] = _dot(h_ref[...], w_ref[...]).astype(BF16)

    @pl.when(j == last)
    def _():
        os_ref[...] = _dot(h_ref[...], ws_ref[...])
        for r in range(h_ref.shape[0] // CONV_ROWS):
            rows = slice(r * CONV_ROWS, (r + 1) * CONV_ROWS)
            acc = _dot(h_ref[rows, :], w_ref[...])
            o_ref[rows, :conv_col0] = acc[:, :conv_col0].astype(BF16)
            for ct in range(CONV_CH // CONV_TILE):
                cols = slice(ct * CONV_TILE, (ct + 1) * CONV_TILE)
                u = acc[:, conv_col0 + ct * CONV_TILE:conv_col0 + (ct + 1) * CONV_TILE]
                o_ref[rows, conv_col0 + ct * CONV_TILE:conv_col0 + (ct + 1) * CONV_TILE] = (
                    _conv_silu(u, cw_ref, cb_ref, cols, seg_len).astype(BF16))


def _inproj(h, w_main, w_small, conv_w, conv_b, *, seg_len, tm, tn):
    t = h.shape[0]
    n_tiles = MAIN_W // tn
    conv_col0 = MAIN_W - CONV_CH - (n_tiles - 1) * tn
    assert n_tiles * tn == MAIN_W and conv_col0 >= 0 and conv_col0 % 128 == 0
    assert tm % CONV_ROWS == 0 and CONV_ROWS % seg_len == 0
    return pl.pallas_call(
        functools.partial(_inproj_kernel, conv_col0=conv_col0, seg_len=seg_len),
        grid=(t // tm, n_tiles),
        in_specs=[pl.BlockSpec((tm, D_MODEL), lambda i, j: (i, 0)),
                  pl.BlockSpec((D_MODEL, tn), lambda i, j: (0, j)),
                  pl.BlockSpec((D_MODEL, N_BLK * SMALL_W), lambda i, j: (0, 0)),
                  pl.BlockSpec((CONV_K, CONV_CH), lambda i, j: (0, 0)),
                  pl.BlockSpec((1, CONV_CH), lambda i, j: (0, 0))],
        out_specs=[pl.BlockSpec((tm, tn), lambda i, j: (i, j)),
                   pl.BlockSpec((tm, N_BLK * SMALL_W), lambda i, j: (i, 0))],
        out_shape=[jax.ShapeDtypeStruct((t, MAIN_W), BF16),
                   jax.ShapeDtypeStruct((t, N_BLK * SMALL_W), F32)],
        compiler_params=_cparams(2, BIG_VMEM_LIMIT),
        name="inproj",
    )(h, w_main, w_small, conv_w, conv_b)


def _tri_mask(direction):
    row = lax.broadcasted_iota(jnp.int32, (CHUNK, CHUNK), 0)
    col = lax.broadcasted_iota(jnp.int32, (CHUNK, CHUNK), 1)
    return (col <= row) if direction == 0 else (col >= row)


def _scan_specs(nc, width, col):
    return [pl.BlockSpec((CHUNK, width), lambda b, c: (b * nc + c, col)),
            pl.BlockSpec((CHUNK, width), lambda b, c: (b * nc + nc - 1 - c, col))]


def _scan_max(x, d):
    idx = lax.broadcasted_iota(jnp.int32, x.shape, 0)
    k = 1
    while k < CHUNK:
        if d == 0:
            shifted, valid = pltpu.roll(x, k, 0), idx >= k
        else:
            shifted, valid = pltpu.roll(x, CHUNK - k, 0), idx < CHUNK - k
        x = jnp.maximum(x, jnp.where(valid, shifted, -jnp.inf))
        k *= 2
    return x


def _lane_bcast(x, lane):
    return jnp.broadcast_to(x[:, lane:lane + 1], (x.shape[0], SMALL_W))


ML_AUG = ML_DH + SMALL_W
M_ROWS = 8


def _mlstm_chain(q_ref, k_ref, v_ref, ga_ref, gf_ref, bi_ref, bf_ref, cn_s, m_s, d):
    tri = _tri_mask(d)
    scale = ML_DH ** -0.5
    log_i = ga_ref[...] + bi_ref[...]
    log_f = _log_sigmoid(gf_ref[...] + bf_ref[...])
    b = _scan_sum(tri, log_f)
    g = log_i - b
    m_prev = m_s[d:d + 1, :]
    m_inter = b + m_prev
    m_t = jnp.maximum(m_inter, b + _scan_max(g, d))
    c1 = b - m_t
    inter = jnp.exp(m_inter - m_t)
    e_negm = jnp.exp(-m_t)
    last = CHUNK - 1 if d == 0 else 0
    b_last = b[last:last + 1, :]
    log_w = b_last - b + log_i
    m_end = b_last + m_prev
    m_new = jnp.maximum(m_end, jnp.max(log_w, axis=0, keepdims=True))
    w = jnp.exp(log_w - m_new) * scale
    decay = jnp.exp(m_end - m_new)
    g_t = g.T
    ones = jnp.ones((CHUNK, SMALL_W), BF16)

    hs, new_cn = [], []
    for h in range(ML_HEADS):
        lane = d * ML_HEADS + h
        log_d = jnp.where(tri, _lane_bcast(c1, lane) + g_t[lane:lane + 1, :], -jnp.inf)
        dmat = jnp.exp(log_d) * scale
        sl = slice(h * ML_DH, (h + 1) * ML_DH)
        qh = q_ref[:, sl]
        kh = k_ref[:, sl]
        v_aug = jnp.concatenate([v_ref[:, sl], ones], axis=1)
        scores = (_dot_nt(qh, kh) * dmat).astype(BF16)
        cn_old = cn_s[:, lane * ML_AUG:(lane + 1) * ML_AUG]
        inter_rep = _lane_bcast(inter, lane)
        r = (_dot(scores, v_aug)
             + jnp.concatenate([inter_rep] * (ML_AUG // SMALL_W), axis=1) * _dot(qh, cn_old.astype(BF16)))
        rden = 1.0 / jnp.maximum(jnp.abs(r[:, ML_DH:]), _lane_bcast(e_negm, lane))
        hs.append(r[:, :ML_DH] * jnp.concatenate([rden] * (ML_DH // SMALL_W), axis=1))

        w_rep = _lane_bcast(w, lane)
        kw = (kh.astype(F32) * jnp.concatenate([w_rep] * (ML_DH // SMALL_W), axis=1)).astype(BF16)
        new_cn.append(decay[:, lane:lane + 1] * cn_old + _dot_tn(kw, v_aug))
    return jnp.concatenate(hs, axis=1), new_cn, m_new


def _mlstm_finish(hsum, og_ref, mlg_ref):
    outs = []
    for h in range(ML_HEADS):
        sl = slice(h * ML_DH, (h + 1) * ML_DH)
        hs = hsum[:, sl]
        cen = hs - jnp.mean(hs, axis=1, keepdims=True)
        var = jnp.mean(cen * cen, axis=1, keepdims=True)
        hn = cen * lax.rsqrt(var + EPS) * mlg_ref[:, sl]
        outs.append((jax.nn.sigmoid(og_ref[:, sl].astype(F32)) * hn).astype(BF16))
    return jnp.concatenate(outs, axis=1)


def _mlstm_kernel(*refs, zero_init, nc):
    refs = list(refs)
    qf, qb, kf, kb, vf, vb, of, ob, gaf, gab, gff, gfb, bi_ref, bf_ref, mlg_ref = refs[:15]
    pos = 15
    if not zero_init:
        c0_ref, n0_ref, m0_ref = refs[pos:pos + 3]
        pos += 3
    h_out_ref, c_out_ref, n_out_ref, m_out_ref, cn_s, m_s, hcur, stash = refs[pos:]

    c = pl.program_id(1)

    @pl.when(c == 0)
    def _():
        if zero_init:
            cn_s[...] = jnp.zeros_like(cn_s)
            m_s[...] = jnp.zeros_like(m_s)
        else:
            for r in range(2 * ML_HEADS):
                d, h = divmod(r, ML_HEADS)
                cn_s[:, r * ML_AUG:r * ML_AUG + ML_DH] = c0_ref[d, h]
                n_rows = jnp.broadcast_to(n0_ref[d, h:h + 1, :], (SMALL_W, ML_DH))
                cn_s[:, r * ML_AUG + ML_DH:(r + 1) * ML_AUG] = n_rows.T
            m_s[...] = m0_ref[...]

    h_f, cn_f, m_f = _mlstm_chain(qf, kf, vf, gaf, gff, bi_ref, bf_ref, cn_s, m_s, 0)
    h_b, cn_b, m_b = _mlstm_chain(qb, kb, vb, gab, gfb, bi_ref, bf_ref, cn_s, m_s, 1)
    hcur[...] = jnp.concatenate([h_f, h_b], axis=1)
    cn_s[...] = jnp.concatenate(cn_f + cn_b, axis=1)
    m_s[...] = jnp.concatenate([m_f, m_b] + [jnp.zeros_like(m_f)] * (M_ROWS - 2), axis=0)

    half = nc // 2

    @pl.when(c < half)
    def _():
        stash[c] = hcur[...]

    @pl.when(c >= half)
    def _():
        s = nc - 1 - c
        h_out_ref[c] = _mlstm_finish(hcur[:, :ML_W] + stash[s, :, ML_W:], of, mlg_ref)
        h_out_ref[s] = _mlstm_finish(stash[s, :, :ML_W] + hcur[:, ML_W:], ob, mlg_ref)

    @pl.when(c == nc - 1)
    def _():
        for r in range(2 * ML_HEADS):
            d, h = divmod(r, ML_HEADS)
            c_out_ref[d, h] = cn_s[:, r * ML_AUG:r * ML_AUG + ML_DH]
            n_out_ref[d, h:h + 1, :] = cn_s[:, r * ML_AUG + ML_DH:(r + 1) * ML_AUG].T[0:1, :]
        m_out_ref[...] = m_s[...]


def _mlstm(p_main, small, bi_row, bf_row, mlg, state, *, bsz, seq):
    nc = seq // CHUNK
    assert nc % 2 == 0
    zero_init = state is None
    const = lambda b, c: (0, 0)
    in_specs = (_scan_specs(nc, ML_W, COL_Q) + _scan_specs(nc, ML_W, COL_K) + _scan_specs(nc, ML_W, COL_V)
                + _scan_specs(nc, ML_W, COL_O) + _scan_specs(nc, SMALL_W, BLK_IG) + _scan_specs(nc, SMALL_W, BLK_FG)
                + [pl.BlockSpec((1, SMALL_W), const), pl.BlockSpec((1, SMALL_W), const),
                   pl.BlockSpec((1, ML_W), const)])
    args = [p_main] * 8 + [small] * 4 + [bi_row, bf_row, mlg]
    if not zero_init:
        in_specs += [pl.BlockSpec((None, 2, ML_HEADS, ML_DH, ML_DH), lambda b, c: (b, 0, 0, 0, 0)),
                     pl.BlockSpec((None, 2, ML_HEADS, ML_DH), lambda b, c: (b, 0, 0, 0)),
                     pl.BlockSpec((None, M_ROWS, SMALL_W), lambda b, c: (b, 0, 0))]
        args += list(state)
    out_specs = [pl.BlockSpec((nc, CHUNK, ML_W), lambda b, c: (b, 0, 0)),
                 pl.BlockSpec((None, 2, ML_HEADS, ML_DH, ML_DH), lambda b, c: (b, 0, 0, 0, 0)),
                 pl.BlockSpec((None, 2, ML_HEADS, ML_DH), lambda b, c: (b, 0, 0, 0)),
                 pl.BlockSpec((None, M_ROWS, SMALL_W), lambda b, c: (b, 0, 0))]
    out_shape = [jax.ShapeDtypeStruct((bsz * nc, CHUNK, ML_W), BF16),
                 jax.ShapeDtypeStruct((bsz, 2, ML_HEADS, ML_DH, ML_DH), F32),
                 jax.ShapeDtypeStruct((bsz, 2, ML_HEADS, ML_DH), F32),
                 jax.ShapeDtypeStruct((bsz, M_ROWS, SMALL_W), F32)]
    return pl.pallas_call(
        functools.partial(_mlstm_kernel, zero_init=zero_init, nc=nc),
        grid=(bsz, nc),
        in_specs=in_specs,
        out_specs=out_specs,
        out_shape=out_shape,
        scratch_shapes=[pltpu.VMEM((ML_DH, 2 * ML_HEADS * ML_AUG), F32),
                        pltpu.VMEM((M_ROWS, SMALL_W), F32),
                        pltpu.VMEM((CHUNK, 2 * ML_W), F32),
                        pltpu.VMEM((nc // 2, CHUNK, 2 * ML_W), F32)],
        compiler_params=_cparams(2),
        name="mlstm",
    )(*args)


def _ssd_chain(xs_ref, bc_ref, sm_ref, gb_ref, arow_ref, st_s, d):
    ys, new_st = [], []
    tri = _tri_mask(d)
    p = _softplus(sm_ref[...] + gb_ref[...])
    cum = _scan_sum(tri, p * arow_ref[...])
    p_t = p.T
    cum_t = cum.T
    last = CHUNK - 1 if d == 0 else 0
    lane0 = d * SSM_HEADS
    a_last = cum[last:last + 1, :]

    sel_p = _lane_selector(lane0, SSM_HEADS, SSM_P)
    to_end_x = _select_dot(jnp.exp(a_last - cum) * p, sel_p, 2)
    ea_x = _select_dot(jnp.exp(cum), sel_p, 2)
    ea_last_x = _select_dot(jnp.broadcast_to(jnp.exp(a_last), (8, SMALL_W)), sel_p, 3)[0:1, :]

    lo = lax.broadcasted_iota(jnp.int32, (CHUNK, 2 * SSM_P), 1) < SSM_P
    zero_b = jnp.zeros((CHUNK, 2 * SSM_P), BF16)

    for gi in range(SSM_G):
        bg = bc_ref[:, gi * SSM_N:(gi + 1) * SSM_N]
        cg = bc_ref[:, (SSM_G + gi) * SSM_N:(SSM_G + gi + 1) * SSM_N]
        cb = _dot_nt(cg, bg)
        gcols = slice(gi * SSM_R * SSM_P, (gi + 1) * SSM_R * SSM_P)
        st_old = st_s[:, d * SSM_W + gi * SSM_R * SSM_P:d * SSM_W + (gi + 1) * SSM_R * SSM_P]
        y_pairs = []
        for pr in range(SSM_R // 2):
            h0 = gi * SSM_R + 2 * pr
            lhs = []
            for hh in (h0, h0 + 1):
                lane = lane0 + hh
                seg = _lane_bcast(cum, lane) - cum_t[lane:lane + 1, :]
                decay = jnp.exp(jnp.where(tri, seg, -jnp.inf))
                lhs.append((cb * decay * p_t[lane:lane + 1, :]).astype(BF16))
            xp = xs_ref[:, h0 * SSM_P:(h0 + 2) * SSM_P]
            rhs = jnp.concatenate([jnp.where(lo, xp, zero_b), jnp.where(lo, zero_b, xp)], axis=0)
            y_pairs.append(_dot(jnp.concatenate(lhs, axis=1), rhs))
        y_carry = ea_x[:, gcols] * _dot(cg, st_old.astype(BF16))
        ys.append(jnp.concatenate(y_pairs, axis=1) + y_carry)

        xw = (xs_ref[:, gcols].astype(F32) * to_end_x[:, gcols]).astype(BF16)
        new_st.append(st_old * ea_last_x[:, gcols] + _dot_tn(bg, xw))
    return jnp.concatenate(ys, axis=1), jnp.concatenate(new_st, axis=1)


def _ssd_finish(ysum, xs_ref, z_ref, dsk_ref, ng_ref):
    y = ysum + dsk_ref[...] * xs_ref[...].astype(F32)
    yz = y * _silu(z_ref[...].astype(F32))
    return (yz * _rms(yz) * ng_ref[...]).astype(BF16)


def _ssd_kernel(*refs, zero_init, nc):
    refs = list(refs)
    xf, xb, bcf, bcb, zf, zb, smf, smb, gb_ref, arow_ref, dsk_ref, ng_ref = refs[:12]
    pos = 12
    if not zero_init:
        s0_ref = refs[pos]
        pos += 1
    y_out_ref, s_out_ref, st_s, ycur, stash = refs[pos:]

    c = pl.program_id(1)

    @pl.when(c == 0)
    def _():
        if zero_init:
            st_s[...] = jnp.zeros_like(st_s)
        else:
            st_s[:, :SSM_W] = s0_ref[0].T
            st_s[:, SSM_W:] = s0_ref[1].T

    y_f, st_f = _ssd_chain(xf, bcf, smf, gb_ref, arow_ref, st_s, 0)
    y_b, st_b = _ssd_chain(xb, bcb, smb, gb_ref, arow_ref, st_s, 1)
    ycur[...] = jnp.concatenate([y_f, y_b], axis=1)
    st_s[...] = jnp.concatenate([st_f, st_b], axis=1)

    half = nc // 2

    @pl.when(c < half)
    def _():
        stash[c] = ycur[...]

    @pl.when(c >= half)
    def _():
        s = nc - 1 - c
        y_out_ref[c] = _ssd_finish(ycur[:, :SSM_W] + stash[s, :, SSM_W:], xf, zf, dsk_ref, ng_ref)
        y_out_ref[s] = _ssd_finish(stash[s, :, :SSM_W] + ycur[:, SSM_W:], xb, zb, dsk_ref, ng_ref)

    @pl.when(c == nc - 1)
    def _():
        s_out_ref[0] = st_s[:, :SSM_W].T
        s_out_ref[1] = st_s[:, SSM_W:].T


def _ssd(p_main, small, gb_row, a_row, dsk, ng, state, *, bsz, seq):
    nc = seq // CHUNK
    assert nc % 2 == 0
    zero_init = state is None
    const = lambda b, c: (0, 0)
    in_specs = (_scan_specs(nc, SSM_W, COL_XS) + _scan_specs(nc, BC_W, COL_BC) + _scan_specs(nc, SSM_W, COL_Z)
                + _scan_specs(nc, SMALL_W, BLK_DT)
                + [pl.BlockSpec((1, SMALL_W), const), pl.BlockSpec((1, SMALL_W), const),
                   pl.BlockSpec((1, SSM_W), const), pl.BlockSpec((1, SSM_W), const)])
    args = [p_main] * 6 + [small, small, gb_row, a_row, dsk, ng]
    if not zero_init:
        in_specs.append(pl.BlockSpec((None, 2, SSM_W, SSM_N), lambda b, c: (b, 0, 0, 0)))
        args.append(state)
    return pl.pallas_call(
        functools.partial(_ssd_kernel, zero_init=zero_init, nc=nc),
        grid=(bsz, nc),
        in_specs=in_specs,
        out_specs=[pl.BlockSpec((nc, CHUNK, SSM_W), lambda b, c: (b, 0, 0)),
                   pl.BlockSpec((None, 2, SSM_W, SSM_N), lambda b, c: (b, 0, 0, 0))],
        out_shape=[jax.ShapeDtypeStruct((bsz * nc, CHUNK, SSM_W), BF16),
                   jax.ShapeDtypeStruct((bsz, 2, SSM_W, SSM_N), F32)],
        scratch_shapes=[pltpu.VMEM((SSM_N, 2 * SSM_W), F32),
                        pltpu.VMEM((CHUNK, 2 * SSM_W), F32),
                        pltpu.VMEM((nc // 2, CHUNK, 2 * SSM_W), F32)],
        compiler_params=_cparams(2),
        name="ssd",
    )(*args)


def _outproj_kernel(hml_ref, y_ref, x_ref, mod_ref, g_ref, gn_ref, w_ref, o_ref, hn_ref):
    gate = mod_ref[5:6, :]
    shift = mod_ref[6:7, :]
    scale1 = 1.0 + mod_ref[7:8, :]
    for r in range(x_ref.shape[0] // ROW_GROUP):
        rows = slice(r * ROW_GROUP, (r + 1) * ROW_GROUP)
        mix = _dot(hml_ref[rows, :], w_ref[0:ML_W, :]) + _dot(y_ref[rows, :], w_ref[ML_W:ML_W + SSM_W, :])
        o = x_ref[rows, :] + gate * ((mix * _rms(mix)) * g_ref[...])
        o_ref[rows, :] = o
        hn_ref[rows, :] = ((o * _rms(o)) * gn_ref[...] * scale1 + shift).astype(BF16)


def _outproj(hml, y, x, mod, g, g_next, w, *, tm):
    t = x.shape[0]
    rows_per_mod = t // mod.shape[0]
    row_block = pl.BlockSpec((tm, D_MODEL), lambda i: (i, 0))
    return pl.pallas_call(
        _outproj_kernel,
        grid=(t // tm,),
        in_specs=[pl.BlockSpec((tm, ML_W), lambda i: (i, 0)),
                  pl.BlockSpec((tm, SSM_W), lambda i: (i, 0)),
                  row_block,
                  pl.BlockSpec((None, N_MOD, D_MODEL), lambda i: ((i * tm) // rows_per_mod, 0, 0)),
                  pl.BlockSpec((1, D_MODEL), lambda i: (0, 0)),
                  pl.BlockSpec((1, D_MODEL), lambda i: (0, 0)),
                  pl.BlockSpec((ML_W + SSM_W, D_MODEL), lambda i: (0, 0))],
        out_specs=[row_block, row_block],
        out_shape=[jax.ShapeDtypeStruct((t, D_MODEL), F32), jax.ShapeDtypeStruct((t, D_MODEL), BF16)],
        compiler_params=_cparams(1),
        name="outproj",
    )(hml, y, x, mod, g, g_next, w)


def _prepare_params(norm_g, w_in, gate_bias, dt_bias, a_log, d_skip, conv_w, conv_b, ml_norm_g, ssm_norm_g,
                    w_out, w_gate, w_up, w_down):
    w_main, w_small = _split_w_in(w_in)
    return dict(
        g=[norm_g[i].reshape(1, D_MODEL) for i in range(6)],
        g_first=jnp.stack([norm_g[0], norm_g[2]], axis=0),
        w_main=w_main, w_small=w_small,
        bi_row=_lane_block(gate_bias[0].reshape(1, -1)), bf_row=_lane_block(gate_bias[1].reshape(1, -1)),
        bd_row=_lane_block(dt_bias.reshape(1, -1)), a_row=_lane_block(-jnp.exp(a_log.reshape(1, -1))),
        dsk=jnp.repeat(d_skip, SSM_P).reshape(1, SSM_W),
        conv_w=conv_w, conv_b=conv_b.reshape(1, CONV_CH),
        mlg=ml_norm_g.reshape(1, ML_W), ng=ssm_norm_g.reshape(1, SSM_W),
        w_out=_cast_bf16(w_out), wg=_cast_bf16(w_gate), wu=_cast_bf16(w_up), wd=_cast_bf16(w_down),
    )


def _trunk_path(x, mod, p, ml_state, ssm_state, seg_len, *, tm_ffn=512, tf=512, tm_proj=512, tn_proj=3328):
    bsz, seq, _ = x.shape
    t = bsz * seq
    x = x.reshape(t, D_MODEL)
    x, h = _ffn(x, None, mod, p['g_first'], p['g'][1], p['wg'], p['wu'], p['wd'], idx=0, tm=tm_ffn, tf=tf)
    p_main, small = _inproj(h, p['w_main'], p['w_small'], p['conv_w'], p['conv_b'],
                            seg_len=seg_len, tm=tm_proj, tn=tn_proj)
    if ml_state is not None:
        c0, n0, m0 = ml_state
        pad = SMALL_W - 2 * ML_HEADS
        m0 = jnp.stack([jnp.pad(m0[:, 0], ((0, 0), (0, pad + ML_HEADS))),
                        jnp.pad(m0[:, 1], ((0, 0), (ML_HEADS, pad)))], axis=1)
        m0 = jnp.pad(m0, ((0, 0), (0, M_ROWS - 2), (0, 0)))
        ml_state = (c0, n0, m0)
        ssm_state = ssm_state.reshape(bsz, 2, SSM_W, SSM_N)
    hml, new_c, new_n, new_m = _mlstm(p_main, small, p['bi_row'], p['bf_row'], p['mlg'], ml_state, bsz=bsz, seq=seq)
    y, new_s = _ssd(p_main, small, p['bd_row'], p['a_row'], p['dsk'], p['ng'], ssm_state, bsz=bsz, seq=seq)
    x, h = _outproj(hml.reshape(t, ML_W), y.reshape(t, SSM_W), x, mod, p['g'][3], p['g'][4], p['w_out'], tm=tm_proj)
    x = _ffn(x, h, mod, p['g'][4], p['g'][5], p['wg'], p['wu'], p['wd'], idx=1, tm=tm_ffn, tf=tf)
    new_m = new_m[:, :2, :2 * ML_HEADS].reshape(bsz, 2, 2, ML_HEADS)
    new_m = jnp.sum(jnp.where(jnp.eye(2, dtype=bool)[None, :, :, None], new_m, 0.0), axis=2)
    return (x.reshape(bsz, seq, D_MODEL), new_c, new_n, new_m,
            new_s.reshape(bsz, 2, SSM_HEADS, SSM_P, SSM_N))


def kernel(x_prompt, x_sample, state_mlstm_C, state_mlstm_n, state_mlstm_m, state_ssm, c, c_ctx, w_ada, b_ada,
           norm_g, w_in, gate_bias, dt_bias, a_log, d_skip, conv_w, conv_b, ml_norm_g, ssm_norm_g, w_out,
           ffn_w_gate, ffn_w_up, ffn_w_down):
    depth = w_in.shape[0]
    bd = x_sample.shape[0]
    y_p, y_s = x_prompt, x_sample
    out_c, out_n, out_m, out_s = [], [], [], []
    cvec = jnp.concatenate([c_ctx[None], c, jnp.zeros((8 - 1 - bd, D_MODEL), F32)], axis=0)
    for l in range(depth):
        p = _prepare_params(norm_g[l], w_in[l], gate_bias[l], dt_bias[l], a_log[l], d_skip[l], conv_w[l], conv_b[l],
                            ml_norm_g[l], ssm_norm_g[l], w_out[l], ffn_w_gate[l], ffn_w_up[l], ffn_w_down[l])
        mod = _ada(cvec, w_ada[l], b_ada[l].reshape(1, -1)).reshape(8, N_MOD, D_MODEL)
        y_p, cn, nn, mn, sn = _trunk_path(y_p, mod[0:1], p, None, None, x_prompt.shape[1])
        lat_state = (state_mlstm_C[:, l], state_mlstm_n[:, l], state_mlstm_m[:, l])
        y_s, _, _, _, _ = _trunk_path(y_s, mod[1:1 + bd], p, lat_state, state_ssm[:, l], GRID_W)
        out_c.append(cn)
        out_n.append(nn)
        out_m.append(mn)
        out_s.append(sn)
    return (y_p, y_s, jnp.stack(out_c, axis=1), jnp.stack(out_n, axis=1), jnp.stack(out_m, axis=1),
            jnp.stack(out_s, axis=1))
```

```python
import functools

import jax
import jax.numpy as jnp
from jax import lax
from jax.experimental import pallas as pl
from jax.experimental.pallas import tpu as pltpu

F32 = jnp.float32
BF16 = jnp.bfloat16

D_MODEL = 2048
GRID_W = 64
ML_HEADS = 4
ML_W = 1024
ML_DH = ML_W // ML_HEADS
SSM_W = 1024
SSM_P = 64
SSM_HEADS = SSM_W // SSM_P
SSM_N = 128
SSM_G = 2
SSM_R = SSM_HEADS // SSM_G
CONV_K = 5
CONV_CH = SSM_W + 2 * SSM_G * SSM_N
BC_W = 2 * SSM_G * SSM_N
D_FF = 5632
CHUNK = 128
N_MOD = 9
EPS = 1e-6

MAIN_W = 4 * ML_W + SSM_W + CONV_CH
COL_Q, COL_K, COL_V, COL_O, COL_Z, COL_XS = 0, 1, 2, 3, 4, 5
COL_BC = (MAIN_W - BC_W) // BC_W
SMALL_W = 128
BLK_IG, BLK_FG, BLK_DT = 0, 1, 2
N_BLK = 3

VMEM_LIMIT = 48 * 1024 * 1024
BIG_VMEM_LIMIT = 60 * 1024 * 1024
ROW_CHUNK = 128
ROW_GROUP = 256
CONV_ROWS = 256
CONV_TILE = 512


def _cparams(n_axes, vmem=VMEM_LIMIT):
    return pltpu.CompilerParams(dimension_semantics=("arbitrary",) * n_axes, vmem_limit_bytes=vmem)


def _dot(a, b):
    return jnp.dot(a, b, preferred_element_type=F32)


def _dot_nt(a, b):
    return lax.dot_general(a, b, (((1,), (1,)), ((), ())), preferred_element_type=F32)


def _dot_tn(a, b):
    return lax.dot_general(a, b, (((0,), (0,)), ((), ())), preferred_element_type=F32)


def _split_bf16(x, terms):
    parts, rest = [], x
    for i in range(terms):
        piece = rest.astype(BF16)
        parts.append(piece)
        if i + 1 < terms:
            rest = rest - piece.astype(F32)
    return parts


def _select_dot(x, sel, terms):
    return _dot(jnp.concatenate(_split_bf16(x, terms), axis=1), jnp.concatenate([sel] * terms, axis=0))


def _scan_sum(tri, x):
    tri_b = jnp.where(tri, 1.0, 0.0).astype(BF16)
    return _dot(jnp.concatenate([tri_b] * 3, axis=1), jnp.concatenate(_split_bf16(x, 3), axis=0))


def _lane_selector(first_lane, n_blocks, width):
    row = lax.broadcasted_iota(jnp.int32, (SMALL_W, n_blocks * width), 0)
    col = lax.broadcasted_iota(jnp.int32, (SMALL_W, n_blocks * width), 1)
    return jnp.where(row == first_lane + col // width, 1.0, 0.0).astype(BF16)


def _silu(x):
    return x * jax.nn.sigmoid(x)


def _softplus(x):
    return jnp.maximum(x, 0.0) + jnp.log(1.0 + jnp.exp(-jnp.abs(x)))


def _log_sigmoid(x):
    return jnp.minimum(x, 0.0) - jnp.log(1.0 + jnp.exp(-jnp.abs(x)))


def _rms(x):
    return lax.rsqrt(jnp.mean(x * x, axis=-1, keepdims=True) + EPS)


def _adaln_rows(x_ref, h_ref, g_ref, mod_ref, mod_base):
    def body(i, carry):
        r0 = pl.multiple_of(i * ROW_CHUNK, ROW_CHUNK)
        x = x_ref[pl.ds(r0, ROW_CHUNK), :]
        shift = mod_ref[mod_base:mod_base + 1, :]
        scale = mod_ref[mod_base + 1:mod_base + 2, :]
        h = (x * _rms(x)) * g_ref[...] * (1.0 + scale) + shift
        h_ref[pl.ds(r0, ROW_CHUNK), :] = h.astype(BF16)
        return carry
    lax.fori_loop(0, x_ref.shape[0] // ROW_CHUNK, body, 0)


def _residual_rows(x_ref, y_ref, o_ref, g_ref, mod_ref, gate_row, gate_mul, hn_ref=None, gn_ref=None, next_base=None):
    def body(i, carry):
        r0 = pl.multiple_of(i * ROW_CHUNK, ROW_CHUNK)
        y = y_ref[pl.ds(r0, ROW_CHUNK), :]
        gate = mod_ref[gate_row:gate_row + 1, :]
        upd = gate_mul * gate * ((y * _rms(y)) * g_ref[...])
        o = x_ref[pl.ds(r0, ROW_CHUNK), :] + upd
        o_ref[pl.ds(r0, ROW_CHUNK), :] = o
        if hn_ref is not None:
            shift = mod_ref[next_base:next_base + 1, :]
            scale = mod_ref[next_base + 1:next_base + 2, :]
            hn_ref[pl.ds(r0, ROW_CHUNK), :] = ((o * _rms(o)) * gn_ref[...] * (1.0 + scale) + shift).astype(BF16)
        return carry
    lax.fori_loop(0, x_ref.shape[0] // ROW_CHUNK, body, 0)


def _cast_kernel(x_ref, o_ref):
    o_ref[...] = x_ref[...].astype(BF16)


def _cast_bf16(w, idx=None):
    rows, cols = w.shape[-2:]
    first = 0 if idx is None else idx
    w2 = w.reshape(-1, cols)
    tr = max(8, min(rows, (8 * 1024 * 1024) // (4 * cols) // 256 * 256))
    while rows % tr:
        tr //= 2
    n_blocks = rows // tr
    return pl.pallas_call(
        _cast_kernel,
        grid=(n_blocks,),
        in_specs=[pl.BlockSpec((tr, cols), lambda i: (first * n_blocks + i, 0))],
        out_specs=pl.BlockSpec((tr, cols), lambda i: (i, 0)),
        out_shape=jax.ShapeDtypeStruct((rows, cols), BF16),
        compiler_params=_cparams(1),
        name="cast_bf16",
    )(w2)


def _lane_block(cols):
    return jnp.concatenate([cols, jnp.zeros(cols.shape[:-1] + (SMALL_W - cols.shape[-1],), cols.dtype)], axis=-1)


W_IN_GATES = 4 * ML_W
W_IN_Z = W_IN_GATES + 4 * ML_HEADS
W_IN_DT = W_IN_Z + SSM_W + CONV_CH
W_IN_COLS = W_IN_DT + 2 * SSM_HEADS


def _split_w_in_kernel(w_ref, om_ref, os_ref):
    om_ref[:, :W_IN_GATES] = w_ref[:, :W_IN_GATES].astype(BF16)
    om_ref[:, W_IN_GATES:] = w_ref[:, W_IN_Z:W_IN_DT].astype(BF16)
    n_ig = 2 * ML_HEADS
    os_ref[...] = jnp.concatenate([_lane_block(w_ref[:, W_IN_GATES:W_IN_GATES + n_ig]),
                                   _lane_block(w_ref[:, W_IN_GATES + n_ig:W_IN_Z]),
                                   _lane_block(w_ref[:, W_IN_DT:])], axis=1).astype(BF16)


def _split_w_in(w_in):
    tr = 256
    return pl.pallas_call(
        _split_w_in_kernel,
        grid=(D_MODEL // tr,),
        in_specs=[pl.BlockSpec((tr, W_IN_COLS), lambda i: (i, 0))],
        out_specs=[pl.BlockSpec((tr, MAIN_W), lambda i: (i, 0)),
                   pl.BlockSpec((tr, N_BLK * SMALL_W), lambda i: (i, 0))],
        out_shape=[jax.ShapeDtypeStruct((D_MODEL, MAIN_W), BF16),
                   jax.ShapeDtypeStruct((D_MODEL, N_BLK * SMALL_W), BF16)],
        compiler_params=_cparams(1),
        name="split_w_in",
    )(w_in)


def _ada_kernel(c_ref, w_ref, b_ref, o_ref):
    s = _silu(c_ref[...]).astype(BF16)
    o_ref[...] = _dot(s, w_ref[...].astype(BF16)) + b_ref[...]


def _ada(cv, w, b):
    n = w.shape[1]
    tn = 1024
    return pl.pallas_call(
        _ada_kernel,
        grid=(n // tn,),
        in_specs=[pl.BlockSpec((cv.shape[0], D_MODEL), lambda j: (0, 0)),
                  pl.BlockSpec((D_MODEL, tn), lambda j: (0, j)),
                  pl.BlockSpec((1, tn), lambda j: (0, j))],
        out_specs=pl.BlockSpec((cv.shape[0], tn), lambda j: (0, j)),
        out_shape=jax.ShapeDtypeStruct((cv.shape[0], n), F32),
        compiler_params=_cparams(1),
        name="ada_mod",
    )(cv, w, b)


def _ffn_kernel(x_ref, mod_ref, gin_ref, gout_ref, wg_ref, wu_ref, wd_ref, *rest, mod_base, first, side_cast):
    rest = list(rest)
    if side_cast:
        if first:
            src, rest, dst = rest[:3], rest[3:5] + rest[8:], rest[5:8]
        else:
            src, rest, dst = rest[1:4], rest[:1] + rest[4:5], rest[5:8]
        for s_ref, d_ref in zip(src, dst):
            d_ref[...] = s_ref[...].astype(BF16)
    if first:
        o_ref, hn_ref, h_ref = rest
    else:
        h_ref, o_ref = rest
    j = pl.program_id(1)
    last = pl.num_programs(1) - 1
    groups = [slice(r * ROW_GROUP, (r + 1) * ROW_GROUP) for r in range(x_ref.shape[0] // ROW_GROUP)]

    def swiglu(h):
        a = (_silu(_dot(h, wg_ref[...])) * _dot(h, wu_ref[...])).astype(BF16)
        return _dot(a, wd_ref[...])

    @pl.when(j == 0)
    def _():
        if first:
            shift = mod_ref[mod_base:mod_base + 1, :]
            scale1 = 1.0 + mod_ref[mod_base + 1:mod_base + 2, :]
            for rows in groups:
                x = x_ref[rows, :]
                h = ((x * _rms(x)) * gin_ref[0:1, :] * scale1 + shift).astype(BF16)
                h_ref[rows, :] = h
                o_ref[rows, :] = swiglu(h)
        else:
            o_ref[...] = swiglu(h_ref[...])

    @pl.when(jnp.logical_and(j > 0, j < last))
    def _():
        o_ref[...] += swiglu(h_ref[...])

    @pl.when(j == last)
    def _():
        gate = 0.5 * mod_ref[mod_base + 2:mod_base + 3, :]
        for rows in groups:
            y = o_ref[rows, :] + swiglu(h_ref[rows, :])
            o = x_ref[rows, :] + gate * ((y * _rms(y)) * gout_ref[...])
            o_ref[rows, :] = o
            if first:
                hn = (o * _rms(o)) * gin_ref[1:2, :] * (1.0 + mod_ref[4:5, :]) + mod_ref[3:4, :]
                hn_ref[rows, :] = hn.astype(BF16)


def _ffn(x, h, mod, g_in, g_out, wg, wu, wd, *, idx, tm, tf, cast_next=None):
    t = x.shape[0]
    rows_per_mod = t // mod.shape[0]
    first = h is None
    ni, nj = t // tm, D_FF // tf
    row_block = pl.BlockSpec((tm, D_MODEL), lambda i, j: (i, 0))
    in_specs = [row_block,
                pl.BlockSpec((None, N_MOD, D_MODEL), lambda i, j: ((i * tm) // rows_per_mod, 0, 0)),
                pl.BlockSpec(g_in.shape, lambda i, j: (0, 0)),
                pl.BlockSpec((1, D_MODEL), lambda i, j: (0, 0)),
                pl.BlockSpec((D_MODEL, tf), lambda i, j: (0, j)),
                pl.BlockSpec((D_MODEL, tf), lambda i, j: (0, j)),
                pl.BlockSpec((tf, D_MODEL), lambda i, j: (j, 0))]
    args = [x, mod, g_in, g_out, wg, wu, wd]
    if first:
        out_specs = [row_block, row_block]
        out_shape = [jax.ShapeDtypeStruct((t, D_MODEL), F32), jax.ShapeDtypeStruct((t, D_MODEL), BF16)]
        scratch = [pltpu.VMEM((tm, D_MODEL), BF16)]
    else:
        in_specs.append(row_block)
        args.append(h)
        out_specs = [row_block]
        out_shape = [jax.ShapeDtypeStruct((t, D_MODEL), F32)]
        scratch = []
    if cast_next is not None:
        k = cast_next[3]
        dr = D_MODEL // ni
        in_specs += [pl.BlockSpec((None, dr, tf), lambda i, j: (k, i, j)),
                     pl.BlockSpec((None, dr, tf), lambda i, j: (k, i, j)),
                     pl.BlockSpec((None, tf, dr), lambda i, j: (k, j, i))]
        args += list(cast_next[:3])
        out_specs += [pl.BlockSpec((dr, tf), lambda i, j: (i, j)),
                      pl.BlockSpec((dr, tf), lambda i, j: (i, j)),
                      pl.BlockSpec((tf, dr), lambda i, j: (j, i))]
        out_shape += [jax.ShapeDtypeStruct((D_MODEL, D_FF), BF16), jax.ShapeDtypeStruct((D_MODEL, D_FF), BF16),
                      jax.ShapeDtypeStruct((D_FF, D_MODEL), BF16)]
    out = pl.pallas_call(
        functools.partial(_ffn_kernel, mod_base=6 * idx, first=first, side_cast=cast_next is not None),
        grid=(ni, nj),
        in_specs=in_specs,
        out_specs=out_specs,
        out_shape=out_shape,
        scratch_shapes=scratch,
        compiler_params=_cparams(2),
        name="ffn",
    )(*args)
    return out if len(out) > 1 else out[0]


def _conv_silu(u, cw_ref, cb_ref, cols, seg_len):
    rows = u.shape[0]
    pos = lax.broadcasted_iota(jnp.int32, u.shape, 0) % seg_len
    acc = jnp.zeros_like(u) + cb_ref[:, cols]
    for j in range(CONV_K):
        off = j - CONV_K // 2
        shifted = u if off == 0 else pltpu.roll(u, (-off) % rows, 0)
        valid = jnp.logical_and(pos + off >= 0, pos + off < seg_len)
        acc = acc + jnp.where(valid, shifted, 0.0) * cw_ref[j:j + 1, cols]
    return _silu(acc)


def _inproj_kernel(h_ref, w_ref, ws_ref, cw_ref, cb_ref, o_ref, os_ref, *, conv_col0, seg_len):
    j = pl.program_id(1)
    last = pl.num_programs(1) - 1

    @pl.when(j != last)
    def _():
        o_ref[...] = _dot(h_ref[...], w_ref[...]).astype(BF16)

    @pl.when(j == last)
    def _():
        os_ref[...] = _dot(h_ref[...], ws_ref[...])
        for r in range(h_ref.shape[0] // CONV_ROWS):
            rows = slice(r * CONV_ROWS, (r + 1) * CONV_ROWS)
            acc = _dot(h_ref[rows, :], w_ref[...])
            o_ref[rows, :conv_col0] = acc[:, :conv_col0].astype(BF16)
            for ct in range(CONV_CH // CONV_TILE):
                cols = slice(ct * CONV_TILE, (ct + 1) * CONV_TILE)
                u = acc[:, conv_col0 + ct * CONV_TILE:conv_col0 + (ct + 1) * CONV_TILE]
                o_ref[rows, conv_col0 + ct * CONV_TILE:conv_col0 + (ct + 1) * CONV_TILE] = (
                    _conv_silu(u, cw_ref, cb_ref, cols, seg_len).astype(BF16))


def _inproj(h, w_main, w_small, conv_w, conv_b, *, seg_len, tm, tn):
    t = h.shape[0]
    n_tiles = MAIN_W // tn
    conv_col0 = MAIN_W - CONV_CH - (n_tiles - 1) * tn
    assert n_tiles * tn == MAIN_W and conv_col0 >= 0 and conv_col0 % 128 == 0
    assert tm % CONV_ROWS == 0 and CONV_ROWS % seg_len == 0
    return pl.pallas_call(
        functools.partial(_inproj_kernel, conv_col0=conv_col0, seg_len=seg_len),
        grid=(t // tm, n_tiles),
        in_specs=[pl.BlockSpec((tm, D_MODEL), lambda i, j: (i, 0)),
                  pl.BlockSpec((D_MODEL, tn), lambda i, j: (0, j)),
                  pl.BlockSpec((D_MODEL, N_BLK * SMALL_W), lambda i, j: (0, 0)),
                  pl.BlockSpec((CONV_K, CONV_CH), lambda i, j: (0, 0)),
                  pl.BlockSpec((1, CONV_CH), lambda i, j: (0, 0))],
        out_specs=[pl.BlockSpec((tm, tn), lambda i, j: (i, j)),
                   pl.BlockSpec((tm, N_BLK * SMALL_W), lambda i, j: (i, 0))],
        out_shape=[jax.ShapeDtypeStruct((t, MAIN_W), BF16),
                   jax.ShapeDtypeStruct((t, N_BLK * SMALL_W), F32)],
        compiler_params=_cparams(2, BIG_VMEM_LIMIT),
        name="inproj",
    )(h, w_main, w_small, conv_w, conv_b)


def _tri_mask(direction):
    row = lax.broadcasted_iota(jnp.int32, (CHUNK, CHUNK), 0)
    col = lax.broadcasted_iota(jnp.int32, (CHUNK, CHUNK), 1)
    return (col <= row) if direction == 0 else (col >= row)


def _scan_specs(nc, width, col):
    return [pl.BlockSpec((CHUNK, width), lambda b, c: (b * nc + c, col)),
            pl.BlockSpec((CHUNK, width), lambda b, c: (b * nc + nc - 1 - c, col))]


def _scan_max(x, d):
    idx = lax.broadcasted_iota(jnp.int32, x.shape, 0)
    k = 1
    while k < CHUNK:
        if d == 0:
            shifted, valid = pltpu.roll(x, k, 0), idx >= k
        else:
            shifted, valid = pltpu.roll(x, CHUNK - k, 0), idx < CHUNK - k
        x = jnp.maximum(x, jnp.where(valid, shifted, -jnp.inf))
        k *= 2
    return x


def _lane_bcast(x, lane):
    return jnp.broadcast_to(x[:, lane:lane + 1], (x.shape[0], SMALL_W))


ML_AUG = ML_DH + SMALL_W
M_ROWS = 8


def _mlstm_chain(q_ref, k_ref, v_ref, ga_ref, gf_ref, bi_ref, bf_ref, cn_s, m_s, d):
    tri = _tri_mask(d)
    scale = ML_DH ** -0.5
    log_i = ga_ref[...] + bi_ref[...]
    log_f = _log_sigmoid(gf_ref[...] + bf_ref[...])
    b = _scan_sum(tri, log_f)
    g = log_i - b
    m_prev = m_s[d:d + 1, :]
    m_inter = b + m_prev
    m_t = jnp.maximum(m_inter, b + _scan_max(g, d))
    c1 = b - m_t
    inter = jnp.exp(m_inter - m_t)
    e_negm = jnp.exp(-m_t)
    last = CHUNK - 1 if d == 0 else 0
    b_last = b[last:last + 1, :]
    log_w = b_last - b + log_i
    m_end = b_last + m_prev
    m_new = jnp.maximum(m_end, jnp.max(log_w, axis=0, keepdims=True))
    w = jnp.exp(log_w - m_new) * scale
    decay = jnp.exp(m_end - m_new)
    g_t = g.T
    ones = jnp.ones((CHUNK, SMALL_W), BF16)

    hs, new_cn = [], []
    for h in range(ML_HEADS):
        lane = d * ML_HEADS + h
        log_d = jnp.where(tri, _lane_bcast(c1, lane) + g_t[lane:lane + 1, :], -jnp.inf)
        dmat = jnp.exp(log_d) * scale
        sl = slice(h * ML_DH, (h + 1) * ML_DH)
        qh = q_ref[:, sl]
        kh = k_ref[:, sl]
        v_aug = jnp.concatenate([v_ref[:, sl], ones], axis=1)
        scores = (_dot_nt(qh, kh) * dmat).astype(BF16)
        cn_old = cn_s[:, lane * ML_AUG:(lane + 1) * ML_AUG]
        inter_rep = _lane_bcast(inter, lane)
        r = (_dot(scores, v_aug)
             + jnp.concatenate([inter_rep] * (ML_AUG // SMALL_W), axis=1) * _dot(qh, cn_old.astype(BF16)))
        rden = 1.0 / jnp.maximum(jnp.abs(r[:, ML_DH:]), _lane_bcast(e_negm, lane))
        hs.append(r[:, :ML_DH] * jnp.concatenate([rden] * (ML_DH // SMALL_W), axis=1))

        w_rep = _lane_bcast(w, lane)
        kw = (kh.astype(F32) * jnp.concatenate([w_rep] * (ML_DH // SMALL_W), axis=1)).astype(BF16)
        new_cn.append(decay[:, lane:lane + 1] * cn_old + _dot_tn(kw, v_aug))
    return jnp.concatenate(hs, axis=1), new_cn, m_new


def _mlstm_finish(hsum, og_ref, mlg_ref):
    outs = []
    for h in range(ML_HEADS):
        sl = slice(h * ML_DH, (h + 1) * ML_DH)
        hs = hsum[:, sl]
        cen = hs - jnp.mean(hs, axis=1, keepdims=True)
        var = jnp.mean(cen * cen, axis=1, keepdims=True)
        hn = cen * lax.rsqrt(var + EPS) * mlg_ref[:, sl]
        outs.append((jax.nn.sigmoid(og_ref[:, sl].astype(F32)) * hn).astype(BF16))
    return jnp.concatenate(outs, axis=1)


def _mlstm_kernel(*refs, zero_init, nc):
    refs = list(refs)
    qf, qb, kf, kb, vf, vb, of, ob, gaf, gab, gff, gfb, bi_ref, bf_ref, mlg_ref = refs[:15]
    pos = 15
    if not zero_init:
        c0_ref, n0_ref, m0_ref = refs[pos:pos + 3]
        pos += 3
    h_out_ref, c_out_ref, n_out_ref, m_out_ref, cn_s, m_s, hcur, stash = refs[pos:]

    c = pl.program_id(1)

    @pl.when(c == 0)
    def _():
        if zero_init:
            cn_s[...] = jnp.zeros_like(cn_s)
            m_s[...] = jnp.zeros_like(m_s)
        else:
            for r in range(2 * ML_HEADS):
                d, h = divmod(r, ML_HEADS)
                cn_s[:, r * ML_AUG:r * ML_AUG + ML_DH] = c0_ref[d, h]
                n_rows = jnp.broadcast_to(n0_ref[d, h:h + 1, :], (SMALL_W, ML_DH))
                cn_s[:, r * ML_AUG + ML_DH:(r + 1) * ML_AUG] = n_rows.T
            m_s[...] = m0_ref[...]

    h_f, cn_f, m_f = _mlstm_chain(qf, kf, vf, gaf, gff, bi_ref, bf_ref, cn_s, m_s, 0)
    h_b, cn_b, m_b = _mlstm_chain(qb, kb, vb, gab, gfb, bi_ref, bf_ref, cn_s, m_s, 1)
    hcur[...] = jnp.concatenate([h_f, h_b], axis=1)
    cn_s[...] = jnp.concatenate(cn_f + cn_b, axis=1)
    m_s[...] = jnp.concatenate([m_f, m_b] + [jnp.zeros_like(m_f)] * (M_ROWS - 2), axis=0)

    half = nc // 2

    @pl.when(c < half)
    def _():
        stash[c] = hcur[...]

    @pl.when(c >= half)
    def _():
        s = nc - 1 - c
        h_out_ref[c] = _mlstm_finish(hcur[:, :ML_W] + stash[s, :, ML_W:], of, mlg_ref)
        h_out_ref[s] = _mlstm_finish(stash[s, :, :ML_W] + hcur[:, ML_W:], ob, mlg_ref)

    @pl.when(c == nc - 1)
    def _():
        for r in range(2 * ML_HEADS):
            d, h = divmod(r, ML_HEADS)
            c_out_ref[d, h] = cn_s[:, r * ML_AUG:r * ML_AUG + ML_DH]
            n_out_ref[d, h:h + 1, :] = cn_s[:, r * ML_AUG + ML_DH:(r + 1) * ML_AUG].T[0:1, :]
        m_out_ref[...] = m_s[...]


def _mlstm(p_main, small, bi_row, bf_row, mlg, state, layer, *, bsz, seq):
    nc = seq // CHUNK
    assert nc % 2 == 0
    zero_init = state is None
    const = lambda b, c: (0, 0)
    in_specs = (_scan_specs(nc, ML_W, COL_Q) + _scan_specs(nc, ML_W, COL_K) + _scan_specs(nc, ML_W, COL_V)
                + _scan_specs(nc, ML_W, COL_O) + _scan_specs(nc, SMALL_W, BLK_IG) + _scan_specs(nc, SMALL_W, BLK_FG)
                + [pl.BlockSpec((1, SMALL_W), const), pl.BlockSpec((1, SMALL_W), const),
                   pl.BlockSpec((1, ML_W), const)])
    args = [p_main] * 8 + [small] * 4 + [bi_row, bf_row, mlg]
    if not zero_init:
        in_specs += [pl.BlockSpec((None, None, 2, ML_HEADS, ML_DH, ML_DH), lambda b, c: (b, layer, 0, 0, 0, 0)),
                     pl.BlockSpec((None, None, 2, ML_HEADS, ML_DH), lambda b, c: (b, layer, 0, 0, 0)),
                     pl.BlockSpec((None, M_ROWS, SMALL_W), lambda b, c: (b, 0, 0))]
        args += list(state)
    out_specs = [pl.BlockSpec((nc, CHUNK, ML_W), lambda b, c: (b, 0, 0)),
                 pl.BlockSpec((None, 2, ML_HEADS, ML_DH, ML_DH), lambda b, c: (b, 0, 0, 0, 0)),
                 pl.BlockSpec((None, 2, ML_HEADS, ML_DH), lambda b, c: (b, 0, 0, 0)),
                 pl.BlockSpec((None, M_ROWS, SMALL_W), lambda b, c: (b, 0, 0))]
    out_shape = [jax.ShapeDtypeStruct((bsz * nc, CHUNK, ML_W), BF16),
                 jax.ShapeDtypeStruct((bsz, 2, ML_HEADS, ML_DH, ML_DH), F32),
                 jax.ShapeDtypeStruct((bsz, 2, ML_HEADS, ML_DH), F32),
                 jax.ShapeDtypeStruct((bsz, M_ROWS, SMALL_W), F32)]
    return pl.pallas_call(
        functools.partial(_mlstm_kernel, zero_init=zero_init, nc=nc),
        grid=(bsz, nc),
        in_specs=in_specs,
        out_specs=out_specs,
        out_shape=out_shape,
        scratch_shapes=[pltpu.VMEM((ML_DH, 2 * ML_HEADS * ML_AUG), F32),
                        pltpu.VMEM((M_ROWS, SMALL_W), F32),
                        pltpu.VMEM((CHUNK, 2 * ML_W), F32),
                        pltpu.VMEM((nc // 2, CHUNK, 2 * ML_W), F32)],
        compiler_params=_cparams(2),
        name="mlstm",
    )(*args)


def _ssd_chain(xs_ref, bc_ref, sm_ref, gb_ref, arow_ref, st_s, d):
    ys, new_st = [], []
    tri = _tri_mask(d)
    p = _softplus(sm_ref[...] + gb_ref[...])
    cum = _scan_sum(tri, p * arow_ref[...])
    p_t = p.T
    cum_t = cum.T
    last = CHUNK - 1 if d == 0 else 0
    lane0 = d * SSM_HEADS
    a_last = cum[last:last + 1, :]

    sel_p = _lane_selector(lane0, SSM_HEADS, SSM_P)
    to_end_x = _select_dot(jnp.exp(a_last - cum) * p, sel_p, 2)
    ea_x = _select_dot(jnp.exp(cum), sel_p, 2)
    ea_last_x = _select_dot(jnp.broadcast_to(jnp.exp(a_last), (8, SMALL_W)), sel_p, 3)[0:1, :]

    lo = lax.broadcasted_iota(jnp.int32, (CHUNK, 2 * SSM_P), 1) < SSM_P
    zero_b = jnp.zeros((CHUNK, 2 * SSM_P), BF16)

    for gi in range(SSM_G):
        bg = bc_ref[:, gi * SSM_N:(gi + 1) * SSM_N]
        cg = bc_ref[:, (SSM_G + gi) * SSM_N:(SSM_G + gi + 1) * SSM_N]
        cb = _dot_nt(cg, bg)
        gcols = slice(gi * SSM_R * SSM_P, (gi + 1) * SSM_R * SSM_P)
        st_old = st_s[:, d * SSM_W + gi * SSM_R * SSM_P:d * SSM_W + (gi + 1) * SSM_R * SSM_P]
        y_pairs = []
        for pr in range(SSM_R // 2):
            h0 = gi * SSM_R + 2 * pr
            lhs = []
            for hh in (h0, h0 + 1):
                lane = lane0 + hh
                seg = _lane_bcast(cum, lane) - cum_t[lane:lane + 1, :]
                decay = jnp.exp(jnp.where(tri, seg, -jnp.inf))
                lhs.append((cb * decay * p_t[lane:lane + 1, :]).astype(BF16))
            xp = xs_ref[:, h0 * SSM_P:(h0 + 2) * SSM_P]
            rhs = jnp.concatenate([jnp.where(lo, xp, zero_b), jnp.where(lo, zero_b, xp)], axis=0)
            y_pairs.append(_dot(jnp.concatenate(lhs, axis=1), rhs))
        y_carry = ea_x[:, gcols] * _dot(cg, st_old.astype(BF16))
        ys.append(jnp.concatenate(y_pairs, axis=1) + y_carry)

        xw = (xs_ref[:, gcols].astype(F32) * to_end_x[:, gcols]).astype(BF16)
        new_st.append(st_old * ea_last_x[:, gcols] + _dot_tn(bg, xw))
    return jnp.concatenate(ys, axis=1), jnp.concatenate(new_st, axis=1)


def _ssd_finish(ysum, xs_ref, z_ref, dsk_ref, ng_ref):
    y = ysum + dsk_ref[...] * xs_ref[...].astype(F32)
    yz = y * _silu(z_ref[...].astype(F32))
    return (yz * _rms(yz) * ng_ref[...]).astype(BF16)


def _ssd_kernel(*refs, zero_init, nc):
    refs = list(refs)
    xf, xb, bcf, bcb, zf, zb, smf, smb, gb_ref, arow_ref, dsk_ref, ng_ref = refs[:12]
    pos = 12
    if not zero_init:
        s0_ref = refs[pos]
        pos += 1
    y_out_ref, s_out_ref, st_s, ycur, stash = refs[pos:]

    c = pl.program_id(1)

    @pl.when(c == 0)
    def _():
        if zero_init:
            st_s[...] = jnp.zeros_like(st_s)
        else:
            st_s[:, :SSM_W] = s0_ref[0].T
            st_s[:, SSM_W:] = s0_ref[1].T

    y_f, st_f = _ssd_chain(xf, bcf, smf, gb_ref, arow_ref, st_s, 0)
    y_b, st_b = _ssd_chain(xb, bcb, smb, gb_ref, arow_ref, st_s, 1)
    ycur[...] = jnp.concatenate([y_f, y_b], axis=1)
    st_s[...] = jnp.concatenate([st_f, st_b], axis=1)

    half = nc // 2

    @pl.when(c < half)
    def _():
        stash[c] = ycur[...]

    @pl.when(c >= half)
    def _():
        s = nc - 1 - c
        y_out_ref[c] = _ssd_finish(ycur[:, :SSM_W] + stash[s, :, SSM_W:], xf, zf, dsk_ref, ng_ref)
        y_out_ref[s] = _ssd_finish(stash[s, :, :SSM_W] + ycur[:, SSM_W:], xb, zb, dsk_ref, ng_ref)

    @pl.when(c == nc - 1)
    def _():
        s_out_ref[0] = st_s[:, :SSM_W].T
        s_out_ref[1] = st_s[:, SSM_W:].T


def _ssd(p_main, small, gb_row, a_row, dsk, ng, state, layer, *, bsz, seq):
    nc = seq // CHUNK
    assert nc % 2 == 0
    zero_init = state is None
    const = lambda b, c: (0, 0)
    in_specs = (_scan_specs(nc, SSM_W, COL_XS) + _scan_specs(nc, BC_W, COL_BC) + _scan_specs(nc, SSM_W, COL_Z)
                + _scan_specs(nc, SMALL_W, BLK_DT)
                + [pl.BlockSpec((1, SMALL_W), const), pl.BlockSpec((1, SMALL_W), const),
                   pl.BlockSpec((1, SSM_W), const), pl.BlockSpec((1, SSM_W), const)])
    args = [p_main] * 6 + [small, small, gb_row, a_row, dsk, ng]
    if not zero_init:
        in_specs.append(pl.BlockSpec((None, None, 2, SSM_W, SSM_N), lambda b, c: (b, layer, 0, 0, 0)))
        args.append(state)
    return pl.pallas_call(
        functools.partial(_ssd_kernel, zero_init=zero_init, nc=nc),
        grid=(bsz, nc),
        in_specs=in_specs,
        out_specs=[pl.BlockSpec((nc, CHUNK, SSM_W), lambda b, c: (b, 0, 0)),
                   pl.BlockSpec((None, 2, SSM_W, SSM_N), lambda b, c: (b, 0, 0, 0))],
        out_shape=[jax.ShapeDtypeStruct((bsz * nc, CHUNK, SSM_W), BF16),
                   jax.ShapeDtypeStruct((bsz, 2, SSM_W, SSM_N), F32)],
        scratch_shapes=[pltpu.VMEM((SSM_N, 2 * SSM_W), F32),
                        pltpu.VMEM((CHUNK, 2 * SSM_W), F32),
                        pltpu.VMEM((nc // 2, CHUNK, 2 * SSM_W), F32)],
        compiler_params=_cparams(2),
        name="ssd",
    )(*args)


def _outproj_kernel(hml_ref, y_ref, x_ref, mod_ref, g_ref, gn_ref, w_ref, o_ref, hn_ref):
    gate = mod_ref[5:6, :]
    shift = mod_ref[6:7, :]
    scale1 = 1.0 + mod_ref[7:8, :]
    for r in range(x_ref.shape[0] // ROW_GROUP):
        rows = slice(r * ROW_GROUP, (r + 1) * ROW_GROUP)
        mix = _dot(hml_ref[rows, :], w_ref[0:ML_W, :]) + _dot(y_ref[rows, :], w_ref[ML_W:ML_W + SSM_W, :])
        o = x_ref[rows, :] + gate * ((mix * _rms(mix)) * g_ref[...])
        o_ref[rows, :] = o
        hn_ref[rows, :] = ((o * _rms(o)) * gn_ref[...] * scale1 + shift).astype(BF16)


def _outproj(hml, y, x, mod, g, g_next, w, *, tm):
    t = x.shape[0]
    rows_per_mod = t // mod.shape[0]
    row_block = pl.BlockSpec((tm, D_MODEL), lambda i: (i, 0))
    return pl.pallas_call(
        _outproj_kernel,
        grid=(t // tm,),
        in_specs=[pl.BlockSpec((tm, ML_W), lambda i: (i, 0)),
                  pl.BlockSpec((tm, SSM_W), lambda i: (i, 0)),
                  row_block,
                  pl.BlockSpec((None, N_MOD, D_MODEL), lambda i: ((i * tm) // rows_per_mod, 0, 0)),
                  pl.BlockSpec((1, D_MODEL), lambda i: (0, 0)),
                  pl.BlockSpec((1, D_MODEL), lambda i: (0, 0)),
                  pl.BlockSpec((ML_W + SSM_W, D_MODEL), lambda i: (0, 0))],
        out_specs=[row_block, row_block],
        out_shape=[jax.ShapeDtypeStruct((t, D_MODEL), F32), jax.ShapeDtypeStruct((t, D_MODEL), BF16)],
        compiler_params=_cparams(1),
        name="outproj",
    )(hml, y, x, mod, g, g_next, w)


def _prepare_params(norm_g, w_in, gate_bias, dt_bias, a_log, d_skip, conv_w, conv_b, ml_norm_g, ssm_norm_g,
                    w_out, w_gate, w_up, w_down):
    w_main, w_small = _split_w_in(w_in)
    return dict(
        g=[norm_g[i].reshape(1, D_MODEL) for i in range(6)],
        g_first=jnp.stack([norm_g[0], norm_g[2]], axis=0),
        w_main=w_main, w_small=w_small,
        bi_row=_lane_block(gate_bias[0].reshape(1, -1)), bf_row=_lane_block(gate_bias[1].reshape(1, -1)),
        bd_row=_lane_block(dt_bias.reshape(1, -1)), a_row=_lane_block(-jnp.exp(a_log.reshape(1, -1))),
        dsk=jnp.repeat(d_skip, SSM_P).reshape(1, SSM_W),
        conv_w=conv_w, conv_b=conv_b.reshape(1, CONV_CH),
        mlg=ml_norm_g.reshape(1, ML_W), ng=ssm_norm_g.reshape(1, SSM_W),
        w_out=_cast_bf16(w_out),
        ffn1_w=(_cast_bf16(w_gate, 0), _cast_bf16(w_up, 0), _cast_bf16(w_down, 0)),
        ffn_w_f32=(w_gate, w_up, w_down),
    )


def _trunk_path(x, mod, p, ml_state, ssm_state, layer, seg_len, ffn2_w=None, *,
                tm_ffn=512, tf=512, tm_proj=512, tn_proj=3328):
    bsz, seq, _ = x.shape
    t = bsz * seq
    x = x.reshape(t, D_MODEL)
    cast_next = None if ffn2_w is not None else p['ffn_w_f32'] + (1,)
    res = _ffn(x, None, mod, p['g_first'], p['g'][1], *p['ffn1_w'], idx=0, tm=tm_ffn, tf=tf, cast_next=cast_next)
    x, h = res[:2]
    if ffn2_w is None:
        ffn2_w = tuple(res[2:])
    p_main, small = _inproj(h, p['w_main'], p['w_small'], p['conv_w'], p['conv_b'],
                            seg_len=seg_len, tm=tm_proj, tn=tn_proj)
    if ml_state is not None:
        c0, n0, m0 = ml_state
        m0 = m0[:, layer]
        pad = SMALL_W - 2 * ML_HEADS
        m0 = jnp.stack([jnp.pad(m0[:, 0], ((0, 0), (0, pad + ML_HEADS))),
                        jnp.pad(m0[:, 1], ((0, 0), (ML_HEADS, pad)))], axis=1)
        m0 = jnp.pad(m0, ((0, 0), (0, M_ROWS - 2), (0, 0)))
        ml_state = (c0, n0, m0)
        ssm_state = ssm_state.reshape(ssm_state.shape[:3] + (SSM_W, SSM_N))
    hml, new_c, new_n, new_m = _mlstm(p_main, small, p['bi_row'], p['bf_row'], p['mlg'], ml_state, layer,
                                      bsz=bsz, seq=seq)
    y, new_s = _ssd(p_main, small, p['bd_row'], p['a_row'], p['dsk'], p['ng'], ssm_state, layer, bsz=bsz, seq=seq)
    x, h = _outproj(hml.reshape(t, ML_W), y.reshape(t, SSM_W), x, mod, p['g'][3], p['g'][4], p['w_out'], tm=tm_proj)
    x = _ffn(x, h, mod, p['g'][4], p['g'][5], *ffn2_w, idx=1, tm=tm_ffn, tf=tf)
    new_m = new_m[:, :2, :2 * ML_HEADS].reshape(bsz, 2, 2, ML_HEADS)
    new_m = jnp.sum(jnp.where(jnp.eye(2, dtype=bool)[None, :, :, None], new_m, 0.0), axis=2)
    states = (new_c, new_n, new_m, new_s.reshape(bsz, 2, SSM_HEADS, SSM_P, SSM_N))
    return x.reshape(bsz, seq, D_MODEL), states, ffn2_w


def _stack_layers(parts):
    return parts[0][:, None] if len(parts) == 1 else jnp.stack(parts, axis=1)


def kernel(x_prompt, x_sample, state_mlstm_C, state_mlstm_n, state_mlstm_m, state_ssm, c, c_ctx, w_ada, b_ada,
           norm_g, w_in, gate_bias, dt_bias, a_log, d_skip, conv_w, conv_b, ml_norm_g, ssm_norm_g, w_out,
           ffn_w_gate, ffn_w_up, ffn_w_down):
    depth = w_in.shape[0]
    bd = x_sample.shape[0]
    y_p, y_s = x_prompt, x_sample
    new_states = []
    cvec = jnp.concatenate([c_ctx[None], c, jnp.zeros((8 - 1 - bd, D_MODEL), F32)], axis=0)
    for l in range(depth):
        p = _prepare_params(norm_g[l], w_in[l], gate_bias[l], dt_bias[l], a_log[l], d_skip[l], conv_w[l], conv_b[l],
                            ml_norm_g[l], ssm_norm_g[l], w_out[l], ffn_w_gate[l], ffn_w_up[l], ffn_w_down[l])
        mod = _ada(cvec, w_ada[l], b_ada[l].reshape(1, -1)).reshape(8, N_MOD, D_MODEL)
        y_p, states, ffn2_w = _trunk_path(y_p, mod[0:1], p, None, None, l, x_prompt.shape[1])
        lat_state = (state_mlstm_C, state_mlstm_n, state_mlstm_m)
        y_s, _, _ = _trunk_path(y_s, mod[1:1 + bd], p, lat_state, state_ssm, l, GRID_W, ffn2_w)
        new_states.append(states)
    return (y_p, y_s) + tuple(_stack_layers([st[k] for st in new_states]) for k in range(4))
```

```python
import functools

import jax
import jax.numpy as jnp
from jax import lax
from jax.experimental import pallas as pl
from jax.experimental.pallas import tpu as pltpu

F32 = jnp.float32
BF16 = jnp.bfloat16

D_MODEL = 2048
GRID_W = 64
ML_HEADS = 4
ML_W = 1024
ML_DH = ML_W // ML_HEADS
SSM_W = 1024
SSM_P = 64
SSM_HEADS = SSM_W // SSM_P
SSM_N = 128
SSM_G = 2
SSM_R = SSM_HEADS // SSM_G
CONV_K = 5
CONV_CH = SSM_W + 2 * SSM_G * SSM_N
BC_W = 2 * SSM_G * SSM_N
D_FF = 5632
CHUNK = 128
N_MOD = 9
EPS = 1e-6

MAIN_W = 4 * ML_W + SSM_W + CONV_CH
COL_Q, COL_K, COL_V, COL_O, COL_Z, COL_XS = 0, 1, 2, 3, 4, 5
COL_BC = (MAIN_W - BC_W) // BC_W
SMALL_W = 128
BLK_IG, BLK_FG, BLK_DT = 0, 1, 2
N_BLK = 3

VMEM_LIMIT = 48 * 1024 * 1024
BIG_VMEM_LIMIT = 60 * 1024 * 1024
ROW_GROUP = 256
CONV_ROWS = 256
CONV_TILE = 512


def _cparams(n_axes, vmem=VMEM_LIMIT):
    return pltpu.CompilerParams(dimension_semantics=("arbitrary",) * n_axes, vmem_limit_bytes=vmem)


def _dot(a, b):
    return jnp.dot(a, b, preferred_element_type=F32)


def _dot_nt(a, b):
    return lax.dot_general(a, b, (((1,), (1,)), ((), ())), preferred_element_type=F32)


def _dot_tn(a, b):
    return lax.dot_general(a, b, (((0,), (0,)), ((), ())), preferred_element_type=F32)


def _split_bf16(x, terms):
    parts, rest = [], x
    for i in range(terms):
        piece = rest.astype(BF16)
        parts.append(piece)
        if i + 1 < terms:
            rest = rest - piece.astype(F32)
    return parts


def _select_dot(x, sel, terms):
    return _dot(jnp.concatenate(_split_bf16(x, terms), axis=1), jnp.concatenate([sel] * terms, axis=0))


def _scan_sum(tri, x):
    tri_b = jnp.where(tri, 1.0, 0.0).astype(BF16)
    return _dot(jnp.concatenate([tri_b] * 3, axis=1), jnp.concatenate(_split_bf16(x, 3), axis=0))


def _lane_selector(first_lane, n_blocks, width):
    row = lax.broadcasted_iota(jnp.int32, (SMALL_W, n_blocks * width), 0)
    col = lax.broadcasted_iota(jnp.int32, (SMALL_W, n_blocks * width), 1)
    return jnp.where(row == first_lane + col // width, 1.0, 0.0).astype(BF16)


def _silu(x):
    return x * jax.nn.sigmoid(x)


def _softplus(x):
    return jnp.maximum(x, 0.0) + jnp.log(1.0 + jnp.exp(-jnp.abs(x)))


def _log_sigmoid(x):
    return jnp.minimum(x, 0.0) - jnp.log(1.0 + jnp.exp(-jnp.abs(x)))


def _rms(x):
    return lax.rsqrt(jnp.mean(x * x, axis=-1, keepdims=True) + EPS)


def _cast_kernel(x_ref, o_ref):
    o_ref[...] = x_ref[...].astype(BF16)


def _cast_bf16(w, idx=None):
    rows, cols = w.shape[-2:]
    first = 0 if idx is None else idx
    w2 = w.reshape(-1, cols)
    tr = max(8, min(rows, (8 * 1024 * 1024) // (4 * cols) // 256 * 256))
    while rows % tr:
        tr //= 2
    n_blocks = rows // tr
    return pl.pallas_call(
        _cast_kernel,
        grid=(n_blocks,),
        in_specs=[pl.BlockSpec((tr, cols), lambda i: (first * n_blocks + i, 0))],
        out_specs=pl.BlockSpec((tr, cols), lambda i: (i, 0)),
        out_shape=jax.ShapeDtypeStruct((rows, cols), BF16),
        compiler_params=_cparams(1),
        name="cast_bf16",
    )(w2)


def _lane_block(cols):
    return jnp.concatenate([cols, jnp.zeros(cols.shape[:-1] + (SMALL_W - cols.shape[-1],), cols.dtype)], axis=-1)


W_IN_GATES = 4 * ML_W
W_IN_Z = W_IN_GATES + 4 * ML_HEADS
W_IN_DT = W_IN_Z + SSM_W + CONV_CH
W_IN_COLS = W_IN_DT + 2 * SSM_HEADS


def _split_w_in_kernel(w_ref, om_ref, os_ref):
    om_ref[:, :W_IN_GATES] = w_ref[:, :W_IN_GATES].astype(BF16)
    om_ref[:, W_IN_GATES:] = w_ref[:, W_IN_Z:W_IN_DT].astype(BF16)
    n_ig = 2 * ML_HEADS
    os_ref[...] = jnp.concatenate([_lane_block(w_ref[:, W_IN_GATES:W_IN_GATES + n_ig]),
                                   _lane_block(w_ref[:, W_IN_GATES + n_ig:W_IN_Z]),
                                   _lane_block(w_ref[:, W_IN_DT:])], axis=1).astype(BF16)


def _split_w_in(w_in):
    tr = 256
    return pl.pallas_call(
        _split_w_in_kernel,
        grid=(D_MODEL // tr,),
        in_specs=[pl.BlockSpec((tr, W_IN_COLS), lambda i: (i, 0))],
        out_specs=[pl.BlockSpec((tr, MAIN_W), lambda i: (i, 0)),
                   pl.BlockSpec((tr, N_BLK * SMALL_W), lambda i: (i, 0))],
        out_shape=[jax.ShapeDtypeStruct((D_MODEL, MAIN_W), BF16),
                   jax.ShapeDtypeStruct((D_MODEL, N_BLK * SMALL_W), BF16)],
        compiler_params=_cparams(1),
        name="split_w_in",
    )(w_in)


def _ada_kernel(c_ref, w_ref, b_ref, o_ref):
    s = _silu(c_ref[...]).astype(BF16)
    o_ref[...] = _dot(s, w_ref[...].astype(BF16)) + b_ref[...]


def _ada(cv, w, b):
    n = w.shape[1]
    tn = 1024
    return pl.pallas_call(
        _ada_kernel,
        grid=(n // tn,),
        in_specs=[pl.BlockSpec((cv.shape[0], D_MODEL), lambda j: (0, 0)),
                  pl.BlockSpec((D_MODEL, tn), lambda j: (0, j)),
                  pl.BlockSpec((1, tn), lambda j: (0, j))],
        out_specs=pl.BlockSpec((cv.shape[0], tn), lambda j: (0, j)),
        out_shape=jax.ShapeDtypeStruct((cv.shape[0], n), F32),
        compiler_params=_cparams(1),
        name="ada_mod",
    )(cv, w, b)


def _ffn_kernel(x_ref, mod_ref, gin_ref, gout_ref, wg_ref, wu_ref, wd_ref, *rest, mod_base, first, side_cast):
    rest = list(rest)
    if side_cast:
        if first:
            src, rest, dst = rest[:3], rest[3:5] + rest[8:], rest[5:8]
        else:
            src, rest, dst = rest[1:4], rest[:1] + rest[4:5], rest[5:8]
        for s_ref, d_ref in zip(src, dst):
            d_ref[...] = s_ref[...].astype(BF16)
    if first:
        o_ref, hn_ref, h_ref = rest
    else:
        h_ref, o_ref = rest
    j = pl.program_id(1)
    last = pl.num_programs(1) - 1
    groups = [slice(r * ROW_GROUP, (r + 1) * ROW_GROUP) for r in range(x_ref.shape[0] // ROW_GROUP)]

    def swiglu(h):
        a = (_silu(_dot(h, wg_ref[...])) * _dot(h, wu_ref[...])).astype(BF16)
        return _dot(a, wd_ref[...])

    @pl.when(j == 0)
    def _():
        if first:
            shift = mod_ref[mod_base:mod_base + 1, :]
            scale1 = 1.0 + mod_ref[mod_base + 1:mod_base + 2, :]
            for rows in groups:
                x = x_ref[rows, :]
                h = ((x * _rms(x)) * gin_ref[0:1, :] * scale1 + shift).astype(BF16)
                h_ref[rows, :] = h
                o_ref[rows, :] = swiglu(h)
        else:
            o_ref[...] = swiglu(h_ref[...])

    @pl.when(jnp.logical_and(j > 0, j < last))
    def _():
        o_ref[...] += swiglu(h_ref[...])

    @pl.when(j == last)
    def _():
        gate = 0.5 * mod_ref[mod_base + 2:mod_base + 3, :]
        for rows in groups:
            y = o_ref[rows, :] + swiglu(h_ref[rows, :])
            o = x_ref[rows, :] + gate * ((y * _rms(y)) * gout_ref[...])
            o_ref[rows, :] = o
            if first:
                hn = (o * _rms(o)) * gin_ref[1:2, :] * (1.0 + mod_ref[4:5, :]) + mod_ref[3:4, :]
                hn_ref[rows, :] = hn.astype(BF16)


def _ffn(x, h, mod, g_in, g_out, wg, wu, wd, *, idx, tm, tf, cast_next=None):
    t = x.shape[0]
    rows_per_mod = t // mod.shape[0]
    first = h is None
    ni, nj = t // tm, D_FF // tf
    row_block = pl.BlockSpec((tm, D_MODEL), lambda i, j: (i, 0))
    in_specs = [row_block,
                pl.BlockSpec((None, N_MOD, D_MODEL), lambda i, j: ((i * tm) // rows_per_mod, 0, 0)),
                pl.BlockSpec(g_in.shape, lambda i, j: (0, 0)),
                pl.BlockSpec((1, D_MODEL), lambda i, j: (0, 0)),
                pl.BlockSpec((D_MODEL, tf), lambda i, j: (0, j)),
                pl.BlockSpec((D_MODEL, tf), lambda i, j: (0, j)),
                pl.BlockSpec((tf, D_MODEL), lambda i, j: (j, 0))]
    args = [x, mod, g_in, g_out, wg, wu, wd]
    if first:
        out_specs = [row_block, row_block]
        out_shape = [jax.ShapeDtypeStruct((t, D_MODEL), F32), jax.ShapeDtypeStruct((t, D_MODEL), BF16)]
        scratch = [pltpu.VMEM((tm, D_MODEL), BF16)]
    else:
        in_specs.append(row_block)
        args.append(h)
        out_specs = [row_block]
        out_shape = [jax.ShapeDtypeStruct((t, D_MODEL), F32)]
        scratch = []
    if cast_next is not None:
        k = cast_next[3]
        dr = D_MODEL // ni
        in_specs += [pl.BlockSpec((None, dr, tf), lambda i, j: (k, i, j)),
                     pl.BlockSpec((None, dr, tf), lambda i, j: (k, i, j)),
                     pl.BlockSpec((None, tf, dr), lambda i, j: (k, j, i))]
        args += list(cast_next[:3])
        out_specs += [pl.BlockSpec((dr, tf), lambda i, j: (i, j)),
                      pl.BlockSpec((dr, tf), lambda i, j: (i, j)),
                      pl.BlockSpec((tf, dr), lambda i, j: (j, i))]
        out_shape += [jax.ShapeDtypeStruct((D_MODEL, D_FF), BF16), jax.ShapeDtypeStruct((D_MODEL, D_FF), BF16),
                      jax.ShapeDtypeStruct((D_FF, D_MODEL), BF16)]
    out = pl.pallas_call(
        functools.partial(_ffn_kernel, mod_base=6 * idx, first=first, side_cast=cast_next is not None),
        grid=(ni, nj),
        in_specs=in_specs,
        out_specs=out_specs,
        out_shape=out_shape,
        scratch_shapes=scratch,
        compiler_params=_cparams(2),
        name="ffn",
    )(*args)
    return out if len(out) > 1 else out[0]


def _conv_silu(u, cw_ref, cb_ref, cols, seg_len):
    rows = u.shape[0]
    pos = lax.broadcasted_iota(jnp.int32, u.shape, 0) % seg_len
    acc = jnp.zeros_like(u) + cb_ref[:, cols]
    for j in range(CONV_K):
        off = j - CONV_K // 2
        shifted = u if off == 0 else pltpu.roll(u, (-off) % rows, 0)
        valid = jnp.logical_and(pos + off >= 0, pos + off < seg_len)
        acc = acc + jnp.where(valid, shifted, 0.0) * cw_ref[j:j + 1, cols]
    return _silu(acc)


def _inproj_kernel(h_ref, w_ref, ws_ref, cw_ref, cb_ref, o_ref, os_ref, *, conv_col0, seg_len):
    j = pl.program_id(1)
    last = pl.num_programs(1) - 1

    @pl.when(j != last)
    def _():
        o_ref[...] = _dot(h_ref[...], w_ref[...]).astype(BF16)

    @pl.when(j == last)
    def _():
        os_ref[...] = _dot(h_ref[...], ws_ref[...])
        for r in range(h_ref.shape[0] // CONV_ROWS):
            rows = slice(r * CONV_ROWS, (r + 1) * CONV_ROWS)
            acc = _dot(h_ref[rows, :], w_ref[...])
            o_ref[rows, :conv_col0] = acc[:, :conv_col0].astype(BF16)
            for ct in range(CONV_CH // CONV_TILE):
                cols = slice(ct * CONV_TILE, (ct + 1) * CONV_TILE)
                u = acc[:, conv_col0 + ct * CONV_TILE:conv_col0 + (ct + 1) * CONV_TILE]
                o_ref[rows, conv_col0 + ct * CONV_TILE:conv_col0 + (ct + 1) * CONV_TILE] = (
                    _conv_silu(u, cw_ref, cb_ref, cols, seg_len).astype(BF16))


def _inproj(h, w_main, w_small, conv_w, conv_b, *, seg_len, tm, tn):
    t = h.shape[0]
    n_tiles = MAIN_W // tn
    conv_col0 = MAIN_W - CONV_CH - (n_tiles - 1) * tn
    assert n_tiles * tn == MAIN_W and conv_col0 >= 0 and conv_col0 % 128 == 0
    assert tm % CONV_ROWS == 0 and CONV_ROWS % seg_len == 0
    return pl.pallas_call(
        functools.partial(_inproj_kernel, conv_col0=conv_col0, seg_len=seg_len),
        grid=(t // tm, n_tiles),
        in_specs=[pl.BlockSpec((tm, D_MODEL), lambda i, j: (i, 0)),
                  pl.BlockSpec((D_MODEL, tn), lambda i, j: (0, j)),
                  pl.BlockSpec((D_MODEL, N_BLK * SMALL_W), lambda i, j: (0, 0)),
                  pl.BlockSpec((CONV_K, CONV_CH), lambda i, j: (0, 0)),
                  pl.BlockSpec((1, CONV_CH), lambda i, j: (0, 0))],
        out_specs=[pl.BlockSpec((tm, tn), lambda i, j: (i, j)),
                   pl.BlockSpec((tm, N_BLK * SMALL_W), lambda i, j: (i, 0))],
        out_shape=[jax.ShapeDtypeStruct((t, MAIN_W), BF16),
                   jax.ShapeDtypeStruct((t, N_BLK * SMALL_W), F32)],
        compiler_params=_cparams(2, BIG_VMEM_LIMIT),
        name="inproj",
    )(h, w_main, w_small, conv_w, conv_b)


def _tri_mask(direction):
    row = lax.broadcasted_iota(jnp.int32, (CHUNK, CHUNK), 0)
    col = lax.broadcasted_iota(jnp.int32, (CHUNK, CHUNK), 1)
    return (col <= row) if direction == 0 else (col >= row)


def _scan_specs(nc, width, col):
    return [pl.BlockSpec((CHUNK, width), lambda b, c: (b * nc + c, col)),
            pl.BlockSpec((CHUNK, width), lambda b, c: (b * nc + nc - 1 - c, col))]


def _scan_max(x, d):
    idx = lax.broadcasted_iota(jnp.int32, x.shape, 0)
    k = 1
    while k < CHUNK:
        if d == 0:
            shifted, valid = pltpu.roll(x, k, 0), idx >= k
        else:
            shifted, valid = pltpu.roll(x, CHUNK - k, 0), idx < CHUNK - k
        x = jnp.maximum(x, jnp.where(valid, shifted, -jnp.inf))
        k *= 2
    return x


def _lane_bcast(x, lane):
    return jnp.broadcast_to(x[:, lane:lane + 1], (x.shape[0], SMALL_W))


ML_AUG = ML_DH + SMALL_W
M_ROWS = 8


def _mlstm_chain(q_ref, k_ref, v_ref, ga_ref, gf_ref, bi_ref, bf_ref, cn_s, m_s, d, fresh):
    tri = _tri_mask(d)
    scale = ML_DH ** -0.5
    log_i = ga_ref[...] + bi_ref[...]
    log_f = _log_sigmoid(gf_ref[...] + bf_ref[...])
    b = _scan_sum(tri, log_f)
    g = log_i - b
    m_prev = jnp.zeros((1, SMALL_W), F32) if fresh else m_s[d:d + 1, :]
    m_inter = b + m_prev
    m_t = jnp.maximum(m_inter, b + _scan_max(g, d))
    c1 = b - m_t
    inter = jnp.exp(m_inter - m_t)
    e_negm = jnp.exp(-m_t)
    last = CHUNK - 1 if d == 0 else 0
    b_last = b[last:last + 1, :]
    log_w = b_last - b + log_i
    m_end = b_last + m_prev
    m_new = jnp.maximum(m_end, jnp.max(log_w, axis=0, keepdims=True))
    w = jnp.exp(log_w - m_new) * scale
    decay = jnp.exp(m_end - m_new)
    g_t = g.T
    ones = jnp.ones((CHUNK, SMALL_W), BF16)

    hs, new_cn = [], []
    for h in range(ML_HEADS):
        lane = d * ML_HEADS + h
        log_d = jnp.where(tri, _lane_bcast(c1, lane) + g_t[lane:lane + 1, :], -jnp.inf)
        dmat = jnp.exp(log_d) * scale
        sl = slice(h * ML_DH, (h + 1) * ML_DH)
        qh = q_ref[:, sl]
        kh = k_ref[:, sl]
        v_aug = jnp.concatenate([v_ref[:, sl], ones], axis=1)
        scores = (_dot_nt(qh, kh) * dmat).astype(BF16)
        r = _dot(scores, v_aug)
        if not fresh:
            cn_old = cn_s[:, lane * ML_AUG:(lane + 1) * ML_AUG]
            inter_rep = _lane_bcast(inter, lane)
            r = r + jnp.concatenate([inter_rep] * (ML_AUG // SMALL_W), axis=1) * _dot(qh, cn_old.astype(BF16))
        rden = 1.0 / jnp.maximum(jnp.abs(r[:, ML_DH:]), _lane_bcast(e_negm, lane))
        hs.append(r[:, :ML_DH] * jnp.concatenate([rden] * (ML_DH // SMALL_W), axis=1))

        w_rep = _lane_bcast(w, lane)
        kw = (kh.astype(F32) * jnp.concatenate([w_rep] * (ML_DH // SMALL_W), axis=1)).astype(BF16)
        upd = _dot_tn(kw, v_aug)
        new_cn.append(upd if fresh else decay[:, lane:lane + 1] * cn_old + upd)
    return jnp.concatenate(hs, axis=1), new_cn, m_new


def _mlstm_finish(hsum, og_ref, mlg_ref):
    outs = []
    for h in range(ML_HEADS):
        sl = slice(h * ML_DH, (h + 1) * ML_DH)
        hs = hsum[:, sl]
        cen = hs - jnp.mean(hs, axis=1, keepdims=True)
        var = jnp.mean(cen * cen, axis=1, keepdims=True)
        hn = cen * lax.rsqrt(var + EPS) * mlg_ref[:, sl]
        outs.append((jax.nn.sigmoid(og_ref[:, sl].astype(F32)) * hn).astype(BF16))
    return jnp.concatenate(outs, axis=1)


def _mlstm_kernel(*refs, zero_init, nc):
    refs = list(refs)
    qf, qb, kf, kb, vf, vb, of, ob, gaf, gab, gff, gfb, bi_ref, bf_ref, mlg_ref = refs[:15]
    pos = 15
    if not zero_init:
        c0_ref, n0_ref, m0_ref = refs[pos:pos + 3]
        pos += 3
    h_out_ref, c_out_ref, n_out_ref, m_out_ref, cn_s, m_s, hcur, stash = refs[pos:]

    c = pl.program_id(1)

    def scan_step(fresh):
        h_f, cn_f, m_f = _mlstm_chain(qf, kf, vf, gaf, gff, bi_ref, bf_ref, cn_s, m_s, 0, fresh)
        h_b, cn_b, m_b = _mlstm_chain(qb, kb, vb, gab, gfb, bi_ref, bf_ref, cn_s, m_s, 1, fresh)
        hcur[...] = jnp.concatenate([h_f, h_b], axis=1)
        cn_s[...] = jnp.concatenate(cn_f + cn_b, axis=1)
        m_s[...] = jnp.concatenate([m_f, m_b] + [jnp.zeros_like(m_f)] * (M_ROWS - 2), axis=0)

    if zero_init:
        pl.when(c == 0)(functools.partial(scan_step, True))
        pl.when(c > 0)(functools.partial(scan_step, False))
    else:
        @pl.when(c == 0)
        def _():
            for r in range(2 * ML_HEADS):
                d, h = divmod(r, ML_HEADS)
                cn_s[:, r * ML_AUG:r * ML_AUG + ML_DH] = c0_ref[d, h]
                n_rows = jnp.broadcast_to(n0_ref[d, h:h + 1, :], (SMALL_W, ML_DH))
                cn_s[:, r * ML_AUG + ML_DH:(r + 1) * ML_AUG] = n_rows.T
            m_s[...] = m0_ref[...]

        scan_step(False)

    half = nc // 2

    @pl.when(c < half)
    def _():
        stash[c] = hcur[...]

    @pl.when(c >= half)
    def _():
        s = nc - 1 - c
        h_out_ref[c] = _mlstm_finish(hcur[:, :ML_W] + stash[s, :, ML_W:], of, mlg_ref)
        h_out_ref[s] = _mlstm_finish(stash[s, :, :ML_W] + hcur[:, ML_W:], ob, mlg_ref)

    @pl.when(c == nc - 1)
    def _():
        for r in range(2 * ML_HEADS):
            d, h = divmod(r, ML_HEADS)
            c_out_ref[d, h] = cn_s[:, r * ML_AUG:r * ML_AUG + ML_DH]
            n_out_ref[d, h:h + 1, :] = cn_s[:, r * ML_AUG + ML_DH:(r + 1) * ML_AUG].T[0:1, :]
        m_out_ref[...] = m_s[...]


def _mlstm(p_main, small, bi_row, bf_row, mlg, state, layer, *, bsz, seq):
    nc = seq // CHUNK
    assert nc % 2 == 0
    zero_init = state is None
    const = lambda b, c: (0, 0)
    in_specs = (_scan_specs(nc, ML_W, COL_Q) + _scan_specs(nc, ML_W, COL_K) + _scan_specs(nc, ML_W, COL_V)
                + _scan_specs(nc, ML_W, COL_O) + _scan_specs(nc, SMALL_W, BLK_IG) + _scan_specs(nc, SMALL_W, BLK_FG)
                + [pl.BlockSpec((1, SMALL_W), const), pl.BlockSpec((1, SMALL_W), const),
                   pl.BlockSpec((1, ML_W), const)])
    args = [p_main] * 8 + [small] * 4 + [bi_row, bf_row, mlg]
    if not zero_init:
        in_specs += [pl.BlockSpec((None, None, 2, ML_HEADS, ML_DH, ML_DH), lambda b, c: (b, layer, 0, 0, 0, 0)),
                     pl.BlockSpec((None, None, 2, ML_HEADS, ML_DH), lambda b, c: (b, layer, 0, 0, 0)),
                     pl.BlockSpec((None, M_ROWS, SMALL_W), lambda b, c: (b, 0, 0))]
        args += list(state)
    out_specs = [pl.BlockSpec((nc, CHUNK, ML_W), lambda b, c: (b, 0, 0)),
                 pl.BlockSpec((None, 2, ML_HEADS, ML_DH, ML_DH), lambda b, c: (b, 0, 0, 0, 0)),
                 pl.BlockSpec((None, 2, ML_HEADS, ML_DH), lambda b, c: (b, 0, 0, 0)),
                 pl.BlockSpec((None, M_ROWS, SMALL_W), lambda b, c: (b, 0, 0))]
    out_shape = [jax.ShapeDtypeStruct((bsz * nc, CHUNK, ML_W), BF16),
                 jax.ShapeDtypeStruct((bsz, 2, ML_HEADS, ML_DH, ML_DH), F32),
                 jax.ShapeDtypeStruct((bsz, 2, ML_HEADS, ML_DH), F32),
                 jax.ShapeDtypeStruct((bsz, M_ROWS, SMALL_W), F32)]
    return pl.pallas_call(
        functools.partial(_mlstm_kernel, zero_init=zero_init, nc=nc),
        grid=(bsz, nc),
        in_specs=in_specs,
        out_specs=out_specs,
        out_shape=out_shape,
        scratch_shapes=[pltpu.VMEM((ML_DH, 2 * ML_HEADS * ML_AUG), F32),
                        pltpu.VMEM((M_ROWS, SMALL_W), F32),
                        pltpu.VMEM((CHUNK, 2 * ML_W), F32),
                        pltpu.VMEM((nc // 2, CHUNK, 2 * ML_W), F32)],
        compiler_params=_cparams(2),
        name="mlstm",
    )(*args)


def _ssd_chain(xs_ref, bc_ref, sm_ref, gb_ref, arow_ref, st_s, d, fresh):
    ys, new_st = [], []
    tri = _tri_mask(d)
    p = _softplus(sm_ref[...] + gb_ref[...])
    cum = _scan_sum(tri, p * arow_ref[...])
    p_t = p.T
    cum_t = cum.T
    last = CHUNK - 1 if d == 0 else 0
    lane0 = d * SSM_HEADS
    a_last = cum[last:last + 1, :]

    sel_p = _lane_selector(lane0, SSM_HEADS, SSM_P)
    to_end_x = _select_dot(jnp.exp(a_last - cum) * p, sel_p, 2)
    if not fresh:
        ea_x = _select_dot(jnp.exp(cum), sel_p, 2)
        ea_last_x = _select_dot(jnp.broadcast_to(jnp.exp(a_last), (8, SMALL_W)), sel_p, 3)[0:1, :]

    lo = lax.broadcasted_iota(jnp.int32, (CHUNK, 2 * SSM_P), 1) < SSM_P
    zero_b = jnp.zeros((CHUNK, 2 * SSM_P), BF16)

    for gi in range(SSM_G):
        bg = bc_ref[:, gi * SSM_N:(gi + 1) * SSM_N]
        cg = bc_ref[:, (SSM_G + gi) * SSM_N:(SSM_G + gi + 1) * SSM_N]
        cb = _dot_nt(cg, bg)
        gcols = slice(gi * SSM_R * SSM_P, (gi + 1) * SSM_R * SSM_P)
        y_pairs = []
        for pr in range(SSM_R // 2):
            h0 = gi * SSM_R + 2 * pr
            lhs = []
            for hh in (h0, h0 + 1):
                lane = lane0 + hh
                seg = _lane_bcast(cum, lane) - cum_t[lane:lane + 1, :]
                decay = jnp.exp(jnp.where(tri, seg, -jnp.inf))
                lhs.append((cb * decay * p_t[lane:lane + 1, :]).astype(BF16))
            xp = xs_ref[:, h0 * SSM_P:(h0 + 2) * SSM_P]
            rhs = jnp.concatenate([jnp.where(lo, xp, zero_b), jnp.where(lo, zero_b, xp)], axis=0)
            y_pairs.append(_dot(jnp.concatenate(lhs, axis=1), rhs))
        y_g = jnp.concatenate(y_pairs, axis=1)
        xw = (xs_ref[:, gcols].astype(F32) * to_end_x[:, gcols]).astype(BF16)
        upd = _dot_tn(bg, xw)
        if not fresh:
            st_old = st_s[:, d * SSM_W + gi * SSM_R * SSM_P:d * SSM_W + (gi + 1) * SSM_R * SSM_P]
            y_g = y_g + ea_x[:, gcols] * _dot(cg, st_old.astype(BF16))
            upd = st_old * ea_last_x[:, gcols] + upd
        ys.append(y_g)
        new_st.append(upd)
    return jnp.concatenate(ys, axis=1), jnp.concatenate(new_st, axis=1)


def _ssd_finish(ysum, xs_ref, z_ref, dsk_ref, ng_ref):
    y = ysum + dsk_ref[...] * xs_ref[...].astype(F32)
    yz = y * _silu(z_ref[...].astype(F32))
    return (yz * _rms(yz) * ng_ref[...]).astype(BF16)


def _ssd_kernel(*refs, zero_init, nc):
    refs = list(refs)
    xf, xb, bcf, bcb, zf, zb, smf, smb, gb_ref, arow_ref, dsk_ref, ng_ref = refs[:12]
    pos = 12
    if not zero_init:
        s0_ref = refs[pos]
        pos += 1
    y_out_ref, s_out_ref, st_s, ycur, stash = refs[pos:]

    c = pl.program_id(1)

    def scan_step(fresh):
        y_f, st_f = _ssd_chain(xf, bcf, smf, gb_ref, arow_ref, st_s, 0, fresh)
        y_b, st_b = _ssd_chain(xb, bcb, smb, gb_ref, arow_ref, st_s, 1, fresh)
        ycur[...] = jnp.concatenate([y_f, y_b], axis=1)
        st_s[...] = jnp.concatenate([st_f, st_b], axis=1)

    if zero_init:
        pl.when(c == 0)(functools.partial(scan_step, True))
        pl.when(c > 0)(functools.partial(scan_step, False))
    else:
        @pl.when(c == 0)
        def _():
            st_s[:, :SSM_W] = s0_ref[0].T
            st_s[:, SSM_W:] = s0_ref[1].T

        scan_step(False)

    half = nc // 2

    @pl.when(c < half)
    def _():
        stash[c] = ycur[...]

    @pl.when(c >= half)
    def _():
        s = nc - 1 - c
        y_out_ref[c] = _ssd_finish(ycur[:, :SSM_W] + stash[s, :, SSM_W:], xf, zf, dsk_ref, ng_ref)
        y_out_ref[s] = _ssd_finish(stash[s, :, :SSM_W] + ycur[:, SSM_W:], xb, zb, dsk_ref, ng_ref)

    @pl.when(c == nc - 1)
    def _():
        s_out_ref[0] = st_s[:, :SSM_W].T
        s_out_ref[1] = st_s[:, SSM_W:].T


def _ssd(p_main, small, gb_row, a_row, dsk, ng, state, layer, *, bsz, seq):
    nc = seq // CHUNK
    assert nc % 2 == 0
    zero_init = state is None
    const = lambda b, c: (0, 0)
    in_specs = (_scan_specs(nc, SSM_W, COL_XS) + _scan_specs(nc, BC_W, COL_BC) + _scan_specs(nc, SSM_W, COL_Z)
                + _scan_specs(nc, SMALL_W, BLK_DT)
                + [pl.BlockSpec((1, SMALL_W), const), pl.BlockSpec((1, SMALL_W), const),
                   pl.BlockSpec((1, SSM_W), const), pl.BlockSpec((1, SSM_W), const)])
    args = [p_main] * 6 + [small, small, gb_row, a_row, dsk, ng]
    if not zero_init:
        in_specs.append(pl.BlockSpec((None, None, 2, SSM_W, SSM_N), lambda b, c: (b, layer, 0, 0, 0)))
        args.append(state)
    return pl.pallas_call(
        functools.partial(_ssd_kernel, zero_init=zero_init, nc=nc),
        grid=(bsz, nc),
        in_specs=in_specs,
        out_specs=[pl.BlockSpec((nc, CHUNK, SSM_W), lambda b, c: (b, 0, 0)),
                   pl.BlockSpec((None, 2, SSM_W, SSM_N), lambda b, c: (b, 0, 0, 0))],
        out_shape=[jax.ShapeDtypeStruct((bsz * nc, CHUNK, SSM_W), BF16),
                   jax.ShapeDtypeStruct((bsz, 2, SSM_W, SSM_N), F32)],
        scratch_shapes=[pltpu.VMEM((SSM_N, 2 * SSM_W), F32),
                        pltpu.VMEM((CHUNK, 2 * SSM_W), F32),
                        pltpu.VMEM((nc // 2, CHUNK, 2 * SSM_W), F32)],
        compiler_params=_cparams(2),
        name="ssd",
    )(*args)


def _outproj_kernel(hml_ref, y_ref, x_ref, mod_ref, g_ref, gn_ref, w_ref, o_ref, hn_ref):
    gate = mod_ref[5:6, :]
    shift = mod_ref[6:7, :]
    scale1 = 1.0 + mod_ref[7:8, :]
    for r in range(x_ref.shape[0] // ROW_GROUP):
        rows = slice(r * ROW_GROUP, (r + 1) * ROW_GROUP)
        mix = _dot(hml_ref[rows, :], w_ref[0:ML_W, :]) + _dot(y_ref[rows, :], w_ref[ML_W:ML_W + SSM_W, :])
        o = x_ref[rows, :] + gate * ((mix * _rms(mix)) * g_ref[...])
        o_ref[rows, :] = o
        hn_ref[rows, :] = ((o * _rms(o)) * gn_ref[...] * scale1 + shift).astype(BF16)


def _outproj(hml, y, x, mod, g, g_next, w, *, tm):
    t = x.shape[0]
    rows_per_mod = t // mod.shape[0]
    row_block = pl.BlockSpec((tm, D_MODEL), lambda i: (i, 0))
    return pl.pallas_call(
        _outproj_kernel,
        grid=(t // tm,),
        in_specs=[pl.BlockSpec((tm, ML_W), lambda i: (i, 0)),
                  pl.BlockSpec((tm, SSM_W), lambda i: (i, 0)),
                  row_block,
                  pl.BlockSpec((None, N_MOD, D_MODEL), lambda i: ((i * tm) // rows_per_mod, 0, 0)),
                  pl.BlockSpec((1, D_MODEL), lambda i: (0, 0)),
                  pl.BlockSpec((1, D_MODEL), lambda i: (0, 0)),
                  pl.BlockSpec((ML_W + SSM_W, D_MODEL), lambda i: (0, 0))],
        out_specs=[row_block, row_block],
        out_shape=[jax.ShapeDtypeStruct((t, D_MODEL), F32), jax.ShapeDtypeStruct((t, D_MODEL), BF16)],
        compiler_params=_cparams(1),
        name="outproj",
    )(hml, y, x, mod, g, g_next, w)


def _prepare_params(norm_g, w_in, gate_bias, dt_bias, a_log, d_skip, conv_w, conv_b, ml_norm_g, ssm_norm_g,
                    w_out, w_gate, w_up, w_down):
    w_main, w_small = _split_w_in(w_in)
    return dict(
        g=[norm_g[i].reshape(1, D_MODEL) for i in range(6)],
        g_first=jnp.stack([norm_g[0], norm_g[2]], axis=0),
        w_main=w_main, w_small=w_small,
        bi_row=_lane_block(gate_bias[0].reshape(1, -1)), bf_row=_lane_block(gate_bias[1].reshape(1, -1)),
        bd_row=_lane_block(dt_bias.reshape(1, -1)), a_row=_lane_block(-jnp.exp(a_log.reshape(1, -1))),
        dsk=jnp.repeat(d_skip, SSM_P).reshape(1, SSM_W),
        conv_w=conv_w, conv_b=conv_b.reshape(1, CONV_CH),
        mlg=ml_norm_g.reshape(1, ML_W), ng=ssm_norm_g.reshape(1, SSM_W),
        w_out=_cast_bf16(w_out),
        ffn1_w=(_cast_bf16(w_gate, 0), _cast_bf16(w_up, 0), _cast_bf16(w_down, 0)),
        ffn_w_f32=(w_gate, w_up, w_down),
    )


def _trunk_path(x, mod, p, ml_state, ssm_state, layer, seg_len, ffn2_w=None, *,
                tm_ffn=512, tf=512, tm_proj=512, tn_proj=3328):
    bsz, seq, _ = x.shape
    t = bsz * seq
    x = x.reshape(t, D_MODEL)
    cast_next = None if ffn2_w is not None else p['ffn_w_f32'] + (1,)
    res = _ffn(x, None, mod, p['g_first'], p['g'][1], *p['ffn1_w'], idx=0, tm=tm_ffn, tf=tf, cast_next=cast_next)
    x, h = res[:2]
    if ffn2_w is None:
        ffn2_w = tuple(res[2:])
    p_main, small = _inproj(h, p['w_main'], p['w_small'], p['conv_w'], p['conv_b'],
                            seg_len=seg_len, tm=tm_proj, tn=tn_proj)
    if ml_state is not None:
        c0, n0, m0 = ml_state
        m0 = m0[:, layer]
        pad = SMALL_W - 2 * ML_HEADS
        m0 = jnp.stack([jnp.pad(m0[:, 0], ((0, 0), (0, pad + ML_HEADS))),
                        jnp.pad(m0[:, 1], ((0, 0), (ML_HEADS, pad)))], axis=1)
        m0 = jnp.pad(m0, ((0, 0), (0, M_ROWS - 2), (0, 0)))
        ml_state = (c0, n0, m0)
        ssm_state = ssm_state.reshape(ssm_state.shape[:3] + (SSM_W, SSM_N))
    hml, new_c, new_n, new_m = _mlstm(p_main, small, p['bi_row'], p['bf_row'], p['mlg'], ml_state, layer,
                                      bsz=bsz, seq=seq)
    y, new_s = _ssd(p_main, small, p['bd_row'], p['a_row'], p['dsk'], p['ng'], ssm_state, layer, bsz=bsz, seq=seq)
    x, h = _outproj(hml.reshape(t, ML_W), y.reshape(t, SSM_W), x, mod, p['g'][3], p['g'][4], p['w_out'], tm=tm_proj)
    x = _ffn(x, h, mod, p['g'][4], p['g'][5], *ffn2_w, idx=1, tm=tm_ffn, tf=tf)
    new_m = new_m[:, :2, :2 * ML_HEADS].reshape(bsz, 2, 2, ML_HEADS)
    new_m = jnp.sum(jnp.where(jnp.eye(2, dtype=bool)[None, :, :, None], new_m, 0.0), axis=2)
    states = (new_c, new_n, new_m, new_s.reshape(bsz, 2, SSM_HEADS, SSM_P, SSM_N))
    return x.reshape(bsz, seq, D_MODEL), states, ffn2_w


def _stack_layers(parts):
    return parts[0][:, None] if len(parts) == 1 else jnp.stack(parts, axis=1)


def kernel(x_prompt, x_sample, state_mlstm_C, state_mlstm_n, state_mlstm_m, state_ssm, c, c_ctx, w_ada, b_ada,
           norm_g, w_in, gate_bias, dt_bias, a_log, d_skip, conv_w, conv_b, ml_norm_g, ssm_norm_g, w_out,
           ffn_w_gate, ffn_w_up, ffn_w_down):
    depth = w_in.shape[0]
    bd = x_sample.shape[0]
    y_p, y_s = x_prompt, x_sample
    new_states = []
    cvec = jnp.concatenate([c_ctx[None], c, jnp.zeros((8 - 1 - bd, D_MODEL), F32)], axis=0)
    for l in range(depth):
        p = _prepare_params(norm_g[l], w_in[l], gate_bias[l], dt_bias[l], a_log[l], d_skip[l], conv_w[l], conv_b[l],
                            ml_norm_g[l], ssm_norm_g[l], w_out[l], ffn_w_gate[l], ffn_w_up[l], ffn_w_down[l])
        mod = _ada(cvec, w_ada[l], b_ada[l].reshape(1, -1)).reshape(8, N_MOD, D_MODEL)
        y_p, states, ffn2_w = _trunk_path(y_p, mod[0:1], p, None, None, l, x_prompt.shape[1])
        lat_state = (state_mlstm_C, state_mlstm_n, state_mlstm_m)
        y_s, _, _ = _trunk_path(y_s, mod[1:1 + bd], p, lat_state, state_ssm, l, GRID_W, ffn2_w)
        new_states.append(states)
    return (y_p, y_s) + tuple(_stack_layers([st[k] for st in new_states]) for k in range(4))
```

```python
import functools

import jax
import jax.numpy as jnp
from jax import lax
from jax.experimental import pallas as pl
from jax.experimental.pallas import tpu as pltpu

F32 = jnp.float32
BF16 = jnp.bfloat16

D_MODEL = 2048
GRID_W = 64
ML_HEADS = 4
ML_W = 1024
ML_DH = ML_W // ML_HEADS
SSM_W = 1024
SSM_P = 64
SSM_HEADS = SSM_W // SSM_P
SSM_N = 128
SSM_G = 2
SSM_R = SSM_HEADS // SSM_G
CONV_K = 5
CONV_CH = SSM_W + 2 * SSM_G * SSM_N
BC_W = 2 * SSM_G * SSM_N
D_FF = 5632
CHUNK = 128
N_MOD = 9
EPS = 1e-6

MAIN_W = 4 * ML_W + SSM_W + CONV_CH
COL_Q, COL_K, COL_V, COL_O, COL_Z, COL_XS = 0, 1, 2, 3, 4, 5
COL_BC = (MAIN_W - BC_W) // BC_W
SMALL_W = 128
BLK_IG, BLK_FG, BLK_DT = 0, 1, 2
N_BLK = 3

VMEM_LIMIT = 48 * 1024 * 1024
BIG_VMEM_LIMIT = 60 * 1024 * 1024
ROW_GROUP = 256
CONV_ROWS = 256
CONV_TILE = 512


def _cparams(n_axes, vmem=VMEM_LIMIT):
    return pltpu.CompilerParams(dimension_semantics=("arbitrary",) * n_axes, vmem_limit_bytes=vmem)


def _dot(a, b):
    return jnp.dot(a, b, preferred_element_type=F32)


def _dot_nt(a, b):
    return lax.dot_general(a, b, (((1,), (1,)), ((), ())), preferred_element_type=F32)


def _dot_tn(a, b):
    return lax.dot_general(a, b, (((0,), (0,)), ((), ())), preferred_element_type=F32)


def _split_bf16(x, terms):
    parts, rest = [], x
    for i in range(terms):
        piece = rest.astype(BF16)
        parts.append(piece)
        if i + 1 < terms:
            rest = rest - piece.astype(F32)
    return parts


def _select_dot(x, sel, terms):
    return _dot(jnp.concatenate(_split_bf16(x, terms), axis=1), jnp.concatenate([sel] * terms, axis=0))


def _scan_sum(tri, x):
    tri_b = jnp.where(tri, 1.0, 0.0).astype(BF16)
    return _dot(jnp.concatenate([tri_b] * 3, axis=1), jnp.concatenate(_split_bf16(x, 3), axis=0))


def _lane_selector(first_lane, n_blocks, width):
    row = lax.broadcasted_iota(jnp.int32, (SMALL_W, n_blocks * width), 0)
    col = lax.broadcasted_iota(jnp.int32, (SMALL_W, n_blocks * width), 1)
    return jnp.where(row == first_lane + col // width, 1.0, 0.0).astype(BF16)


def _silu(x):
    return x * jax.nn.sigmoid(x)


def _softplus(x):
    return jnp.maximum(x, 0.0) + jnp.log(1.0 + jnp.exp(-jnp.abs(x)))


def _log_sigmoid(x):
    return jnp.minimum(x, 0.0) - jnp.log(1.0 + jnp.exp(-jnp.abs(x)))


def _rms(x):
    return lax.rsqrt(jnp.mean(x * x, axis=-1, keepdims=True) + EPS)


def _cast_kernel(x_ref, o_ref):
    o_ref[...] = x_ref[...].astype(BF16)


def _cast_bf16(w, idx=None):
    rows, cols = w.shape[-2:]
    first = 0 if idx is None else idx
    w2 = w.reshape(-1, cols)
    tr = max(8, min(rows, (8 * 1024 * 1024) // (4 * cols) // 256 * 256))
    while rows % tr:
        tr //= 2
    n_blocks = rows // tr
    return pl.pallas_call(
        _cast_kernel,
        grid=(n_blocks,),
        in_specs=[pl.BlockSpec((tr, cols), lambda i: (first * n_blocks + i, 0))],
        out_specs=pl.BlockSpec((tr, cols), lambda i: (i, 0)),
        out_shape=jax.ShapeDtypeStruct((rows, cols), BF16),
        compiler_params=_cparams(1),
        name="cast_bf16",
    )(w2)


def _lane_block(cols):
    return jnp.concatenate([cols, jnp.zeros(cols.shape[:-1] + (SMALL_W - cols.shape[-1],), cols.dtype)], axis=-1)


W_IN_GATES = 4 * ML_W
W_IN_Z = W_IN_GATES + 4 * ML_HEADS
W_IN_DT = W_IN_Z + SSM_W + CONV_CH
W_IN_COLS = W_IN_DT + 2 * SSM_HEADS


def _split_w_in_kernel(w_ref, om_ref, os_ref):
    om_ref[:W_IN_GATES, :] = w_ref[:W_IN_GATES, :].astype(BF16)
    om_ref[W_IN_GATES:, :] = w_ref[W_IN_Z:W_IN_DT, :].astype(BF16)
    n_ig = 2 * ML_HEADS

    def row_block(rows):
        return jnp.concatenate([rows, jnp.zeros((SMALL_W - rows.shape[0], rows.shape[1]), F32)], axis=0)

    os_ref[...] = jnp.concatenate([row_block(w_ref[W_IN_GATES:W_IN_GATES + n_ig, :]),
                                   row_block(w_ref[W_IN_GATES + n_ig:W_IN_Z, :]),
                                   row_block(w_ref[W_IN_DT:, :])], axis=0).astype(BF16)


def _split_w_in(w_in_t):
    tc = 256
    return pl.pallas_call(
        _split_w_in_kernel,
        grid=(D_MODEL // tc,),
        in_specs=[pl.BlockSpec((W_IN_COLS, tc), lambda i: (0, i))],
        out_specs=[pl.BlockSpec((MAIN_W, tc), lambda i: (0, i)),
                   pl.BlockSpec((N_BLK * SMALL_W, tc), lambda i: (0, i))],
        out_shape=[jax.ShapeDtypeStruct((MAIN_W, D_MODEL), BF16),
                   jax.ShapeDtypeStruct((N_BLK * SMALL_W, D_MODEL), BF16)],
        compiler_params=_cparams(1),
        name="split_w_in",
    )(w_in_t)


def _ada_kernel(c_ref, w_ref, b_ref, o_ref):
    s = _silu(c_ref[...]).astype(BF16)
    o_ref[...] = _dot(s, w_ref[...].astype(BF16)) + b_ref[...]


def _ada(cv, w, b):
    n = w.shape[1]
    tn = 1024
    return pl.pallas_call(
        _ada_kernel,
        grid=(n // tn,),
        in_specs=[pl.BlockSpec((cv.shape[0], D_MODEL), lambda j: (0, 0)),
                  pl.BlockSpec((D_MODEL, tn), lambda j: (0, j)),
                  pl.BlockSpec((1, tn), lambda j: (0, j))],
        out_specs=pl.BlockSpec((cv.shape[0], tn), lambda j: (0, j)),
        out_shape=jax.ShapeDtypeStruct((cv.shape[0], n), F32),
        compiler_params=_cparams(1),
        name="ada_mod",
    )(cv, w, b)


def _ffn_kernel(x_ref, mod_ref, gin_ref, gout_ref, wg_ref, wu_ref, wd_ref, *rest, mod_base, first, side_cast):
    rest = list(rest)
    if side_cast:
        if first:
            src, rest, dst = rest[:3], rest[3:5] + rest[8:], rest[5:8]
        else:
            src, rest, dst = rest[1:4], rest[:1] + rest[4:5], rest[5:8]
        for s_ref, d_ref in zip(src, dst):
            d_ref[...] = s_ref[...].astype(BF16)
    if first:
        o_ref, hn_ref, h_ref = rest
    else:
        h_ref, o_ref = rest
    j = pl.program_id(1)
    last = pl.num_programs(1) - 1
    groups = [slice(r * ROW_GROUP, (r + 1) * ROW_GROUP) for r in range(x_ref.shape[0] // ROW_GROUP)]

    def swiglu(h):
        a = (_silu(_dot(h, wg_ref[...])) * _dot(h, wu_ref[...])).astype(BF16)
        return _dot(a, wd_ref[...])

    @pl.when(j == 0)
    def _():
        if first:
            shift = mod_ref[mod_base:mod_base + 1, :]
            scale1 = 1.0 + mod_ref[mod_base + 1:mod_base + 2, :]
            for rows in groups:
                x = x_ref[rows, :]
                h = ((x * _rms(x)) * gin_ref[0:1, :] * scale1 + shift).astype(BF16)
                h_ref[rows, :] = h
                o_ref[rows, :] = swiglu(h)
        else:
            o_ref[...] = swiglu(h_ref[...])

    @pl.when(jnp.logical_and(j > 0, j < last))
    def _():
        o_ref[...] += swiglu(h_ref[...])

    @pl.when(j == last)
    def _():
        gate = 0.5 * mod_ref[mod_base + 2:mod_base + 3, :]
        for rows in groups:
            y = o_ref[rows, :] + swiglu(h_ref[rows, :])
            o = x_ref[rows, :] + gate * ((y * _rms(y)) * gout_ref[...])
            o_ref[rows, :] = o
            if first:
                hn = (o * _rms(o)) * gin_ref[1:2, :] * (1.0 + mod_ref[4:5, :]) + mod_ref[3:4, :]
                hn_ref[rows, :] = hn.astype(BF16)


def _ffn(x, h, mod, g_in, g_out, wg, wu, wd, *, idx, tm, tf, cast_next=None):
    t = x.shape[0]
    rows_per_mod = t // mod.shape[0]
    first = h is None
    ni, nj = t // tm, D_FF // tf
    row_block = pl.BlockSpec((tm, D_MODEL), lambda i, j: (i, 0))
    in_specs = [row_block,
                pl.BlockSpec((None, N_MOD, D_MODEL), lambda i, j: ((i * tm) // rows_per_mod, 0, 0)),
                pl.BlockSpec(g_in.shape, lambda i, j: (0, 0)),
                pl.BlockSpec((1, D_MODEL), lambda i, j: (0, 0)),
                pl.BlockSpec((D_MODEL, tf), lambda i, j: (0, j)),
                pl.BlockSpec((D_MODEL, tf), lambda i, j: (0, j)),
                pl.BlockSpec((tf, D_MODEL), lambda i, j: (j, 0))]
    args = [x, mod, g_in, g_out, wg, wu, wd]
    if first:
        out_specs = [row_block, row_block]
        out_shape = [jax.ShapeDtypeStruct((t, D_MODEL), F32), jax.ShapeDtypeStruct((t, D_MODEL), BF16)]
        scratch = [pltpu.VMEM((tm, D_MODEL), BF16)]
    else:
        in_specs.append(row_block)
        args.append(h)
        out_specs = [row_block]
        out_shape = [jax.ShapeDtypeStruct((t, D_MODEL), F32)]
        scratch = []
    if cast_next is not None:
        k = cast_next[3]
        dr = D_MODEL // ni
        in_specs += [pl.BlockSpec((None, dr, tf), lambda i, j: (k, i, j)),
                     pl.BlockSpec((None, dr, tf), lambda i, j: (k, i, j)),
                     pl.BlockSpec((None, tf, dr), lambda i, j: (k, j, i))]
        args += list(cast_next[:3])
        out_specs += [pl.BlockSpec((dr, tf), lambda i, j: (i, j)),
                      pl.BlockSpec((dr, tf), lambda i, j: (i, j)),
                      pl.BlockSpec((tf, dr), lambda i, j: (j, i))]
        out_shape += [jax.ShapeDtypeStruct((D_MODEL, D_FF), BF16), jax.ShapeDtypeStruct((D_MODEL, D_FF), BF16),
                      jax.ShapeDtypeStruct((D_FF, D_MODEL), BF16)]
    out = pl.pallas_call(
        functools.partial(_ffn_kernel, mod_base=6 * idx, first=first, side_cast=cast_next is not None),
        grid=(ni, nj),
        in_specs=in_specs,
        out_specs=out_specs,
        out_shape=out_shape,
        scratch_shapes=scratch,
        compiler_params=_cparams(2),
        name="ffn",
    )(*args)
    return out if len(out) > 1 else out[0]


def _conv_silu(u, cw_ref, cb_ref, cols, seg_len):
    rows = u.shape[0]
    pos = lax.broadcasted_iota(jnp.int32, u.shape, 0) % seg_len
    acc = jnp.zeros_like(u) + cb_ref[:, cols]
    for j in range(CONV_K):
        off = j - CONV_K // 2
        shifted = u if off == 0 else pltpu.roll(u, (-off) % rows, 0)
        valid = jnp.logical_and(pos + off >= 0, pos + off < seg_len)
        acc = acc + jnp.where(valid, shifted, 0.0) * cw_ref[j:j + 1, cols]
    return _silu(acc)


def _inproj_kernel(h_ref, w_ref, ws_ref, cw_ref, cb_ref, o_ref, os_ref, *, conv_col0, seg_len):
    j = pl.program_id(1)
    last = pl.num_programs(1) - 1

    @pl.when(j != last)
    def _():
        o_ref[...] = _dot_nt(h_ref[...], w_ref[...]).astype(BF16)

    @pl.when(j == last)
    def _():
        os_ref[...] = _dot_nt(h_ref[...], ws_ref[...])
        for r in range(h_ref.shape[0] // CONV_ROWS):
            rows = slice(r * CONV_ROWS, (r + 1) * CONV_ROWS)
            acc = _dot_nt(h_ref[rows, :], w_ref[...])
            o_ref[rows, :conv_col0] = acc[:, :conv_col0].astype(BF16)
            for ct in range(CONV_CH // CONV_TILE):
                cols = slice(ct * CONV_TILE, (ct + 1) * CONV_TILE)
                u = acc[:, conv_col0 + ct * CONV_TILE:conv_col0 + (ct + 1) * CONV_TILE]
                o_ref[rows, conv_col0 + ct * CONV_TILE:conv_col0 + (ct + 1) * CONV_TILE] = (
                    _conv_silu(u, cw_ref, cb_ref, cols, seg_len).astype(BF16))


def _inproj(h, w_main, w_small, conv_w, conv_b, *, seg_len, tm, tn):
    t = h.shape[0]
    n_tiles = MAIN_W // tn
    conv_col0 = MAIN_W - CONV_CH - (n_tiles - 1) * tn
    assert n_tiles * tn == MAIN_W and conv_col0 >= 0 and conv_col0 % 128 == 0
    assert tm % CONV_ROWS == 0 and CONV_ROWS % seg_len == 0
    return pl.pallas_call(
        functools.partial(_inproj_kernel, conv_col0=conv_col0, seg_len=seg_len),
        grid=(t // tm, n_tiles),
        in_specs=[pl.BlockSpec((tm, D_MODEL), lambda i, j: (i, 0)),
                  pl.BlockSpec((tn, D_MODEL), lambda i, j: (j, 0)),
                  pl.BlockSpec((N_BLK * SMALL_W, D_MODEL), lambda i, j: (0, 0)),
                  pl.BlockSpec((CONV_K, CONV_CH), lambda i, j: (0, 0)),
                  pl.BlockSpec((1, CONV_CH), lambda i, j: (0, 0))],
        out_specs=[pl.BlockSpec((tm, tn), lambda i, j: (i, j)),
                   pl.BlockSpec((tm, N_BLK * SMALL_W), lambda i, j: (i, 0))],
        out_shape=[jax.ShapeDtypeStruct((t, MAIN_W), BF16),
                   jax.ShapeDtypeStruct((t, N_BLK * SMALL_W), F32)],
        compiler_params=_cparams(2, BIG_VMEM_LIMIT),
        name="inproj",
    )(h, w_main, w_small, conv_w, conv_b)


def _tri_mask(direction):
    row = lax.broadcasted_iota(jnp.int32, (CHUNK, CHUNK), 0)
    col = lax.broadcasted_iota(jnp.int32, (CHUNK, CHUNK), 1)
    return (col <= row) if direction == 0 else (col >= row)


def _scan_specs(nc, width, col):
    return [pl.BlockSpec((CHUNK, width), lambda b, c: (b * nc + c, col)),
            pl.BlockSpec((CHUNK, width), lambda b, c: (b * nc + nc - 1 - c, col))]


def _scan_max(x, d):
    idx = lax.broadcasted_iota(jnp.int32, x.shape, 0)
    k = 1
    while k < CHUNK:
        if d == 0:
            shifted, valid = pltpu.roll(x, k, 0), idx >= k
        else:
            shifted, valid = pltpu.roll(x, CHUNK - k, 0), idx < CHUNK - k
        x = jnp.maximum(x, jnp.where(valid, shifted, -jnp.inf))
        k *= 2
    return x


def _lane_bcast(x, lane):
    return jnp.broadcast_to(x[:, lane:lane + 1], (x.shape[0], SMALL_W))


ML_AUG = ML_DH + SMALL_W
M_ROWS = 8


def _mlstm_chain(q_ref, k_ref, v_ref, ga_ref, gf_ref, bi_ref, bf_ref, cn_s, m_s, d, fresh):
    tri = _tri_mask(d)
    scale = ML_DH ** -0.5
    log_i = ga_ref[...] + bi_ref[...]
    log_f = _log_sigmoid(gf_ref[...] + bf_ref[...])
    b = _scan_sum(tri, log_f)
    g = log_i - b
    m_prev = jnp.zeros((1, SMALL_W), F32) if fresh else m_s[d:d + 1, :]
    m_inter = b + m_prev
    m_t = jnp.maximum(m_inter, b + _scan_max(g, d))
    c1 = b - m_t
    inter = jnp.exp(m_inter - m_t)
    e_negm = jnp.exp(-m_t)
    last = CHUNK - 1 if d == 0 else 0
    b_last = b[last:last + 1, :]
    log_w = b_last - b + log_i
    m_end = b_last + m_prev
    m_new = jnp.maximum(m_end, jnp.max(log_w, axis=0, keepdims=True))
    w = jnp.exp(log_w - m_new) * scale
    decay = jnp.exp(m_end - m_new)
    g_t = g.T
    ones = jnp.ones((CHUNK, SMALL_W), BF16)

    hs, new_cn = [], []
    for h in range(ML_HEADS):
        lane = d * ML_HEADS + h
        log_d = jnp.where(tri, _lane_bcast(c1, lane) + g_t[lane:lane + 1, :], -jnp.inf)
        dmat = jnp.exp(log_d) * scale
        sl = slice(h * ML_DH, (h + 1) * ML_DH)
        qh = q_ref[:, sl]
        kh = k_ref[:, sl]
        v_aug = jnp.concatenate([v_ref[:, sl], ones], axis=1)
        scores = (_dot_nt(qh, kh) * dmat).astype(BF16)
        r = _dot(scores, v_aug)
        if not fresh:
            cn_old = cn_s[:, lane * ML_AUG:(lane + 1) * ML_AUG]
            inter_rep = _lane_bcast(inter, lane)
            r = r + jnp.concatenate([inter_rep] * (ML_AUG // SMALL_W), axis=1) * _dot(qh, cn_old.astype(BF16))
        rden = 1.0 / jnp.maximum(jnp.abs(r[:, ML_DH:]), _lane_bcast(e_negm, lane))
        hs.append(r[:, :ML_DH] * jnp.concatenate([rden] * (ML_DH // SMALL_W), axis=1))

        w_rep = _lane_bcast(w, lane)
        kw = (kh.astype(F32) * jnp.concatenate([w_rep] * (ML_DH // SMALL_W), axis=1)).astype(BF16)
        upd = _dot_tn(kw, v_aug)
        new_cn.append(upd if fresh else decay[:, lane:lane + 1] * cn_old + upd)
    return jnp.concatenate(hs, axis=1), new_cn, m_new


def _mlstm_finish(hsum, og_ref, mlg_ref):
    outs = []
    for h in range(ML_HEADS):
        sl = slice(h * ML_DH, (h + 1) * ML_DH)
        hs = hsum[:, sl]
        cen = hs - jnp.mean(hs, axis=1, keepdims=True)
        var = jnp.mean(cen * cen, axis=1, keepdims=True)
        hn = cen * lax.rsqrt(var + EPS) * mlg_ref[:, sl]
        outs.append((jax.nn.sigmoid(og_ref[:, sl].astype(F32)) * hn).astype(BF16))
    return jnp.concatenate(outs, axis=1)


def _mlstm_kernel(*refs, zero_init, nc):
    refs = list(refs)
    qf, qb, kf, kb, vf, vb, of, ob, gaf, gab, gff, gfb, bi_ref, bf_ref, mlg_ref = refs[:15]
    pos = 15
    if not zero_init:
        c0_ref, n0_ref, m0_ref = refs[pos:pos + 3]
        pos += 3
    h_out_ref, c_out_ref, n_out_ref, m_out_ref, cn_s, m_s, hcur, stash = refs[pos:]

    c = pl.program_id(1)

    def scan_step(fresh):
        h_f, cn_f, m_f = _mlstm_chain(qf, kf, vf, gaf, gff, bi_ref, bf_ref, cn_s, m_s, 0, fresh)
        h_b, cn_b, m_b = _mlstm_chain(qb, kb, vb, gab, gfb, bi_ref, bf_ref, cn_s, m_s, 1, fresh)
        hcur[...] = jnp.concatenate([h_f, h_b], axis=1)
        cn_s[...] = jnp.concatenate(cn_f + cn_b, axis=1)
        m_s[...] = jnp.concatenate([m_f, m_b] + [jnp.zeros_like(m_f)] * (M_ROWS - 2), axis=0)

    if zero_init:
        pl.when(c == 0)(functools.partial(scan_step, True))
        pl.when(c > 0)(functools.partial(scan_step, False))
    else:
        @pl.when(c == 0)
        def _():
            for r in range(2 * ML_HEADS):
                d, h = divmod(r, ML_HEADS)
                cn_s[:, r * ML_AUG:r * ML_AUG + ML_DH] = c0_ref[d, h]
                n_rows = jnp.broadcast_to(n0_ref[d, h:h + 1, :], (SMALL_W, ML_DH))
                cn_s[:, r * ML_AUG + ML_DH:(r + 1) * ML_AUG] = n_rows.T
            m_s[...] = m0_ref[...]

        scan_step(False)

    half = nc // 2

    @pl.when(c < half)
    def _():
        stash[c] = hcur[...]

    @pl.when(c >= half)
    def _():
        s = nc - 1 - c
        h_out_ref[c] = _mlstm_finish(hcur[:, :ML_W] + stash[s, :, ML_W:], of, mlg_ref)
        h_out_ref[s] = _mlstm_finish(stash[s, :, :ML_W] + hcur[:, ML_W:], ob, mlg_ref)

    @pl.when(c == nc - 1)
    def _():
        for r in range(2 * ML_HEADS):
            d, h = divmod(r, ML_HEADS)
            c_out_ref[d, h] = cn_s[:, r * ML_AUG:r * ML_AUG + ML_DH]
            n_out_ref[d, h:h + 1, :] = cn_s[:, r * ML_AUG + ML_DH:(r + 1) * ML_AUG].T[0:1, :]
        m_out_ref[...] = m_s[...]


def _mlstm(p_main, small, bi_row, bf_row, mlg, state, layer, *, bsz, seq):
    nc = seq // CHUNK
    assert nc % 2 == 0
    zero_init = state is None
    const = lambda b, c: (0, 0)
    in_specs = (_scan_specs(nc, ML_W, COL_Q) + _scan_specs(nc, ML_W, COL_K) + _scan_specs(nc, ML_W, COL_V)
                + _scan_specs(nc, ML_W, COL_O) + _scan_specs(nc, SMALL_W, BLK_IG) + _scan_specs(nc, SMALL_W, BLK_FG)
                + [pl.BlockSpec((1, SMALL_W), const), pl.BlockSpec((1, SMALL_W), const),
                   pl.BlockSpec((1, ML_W), const)])
    args = [p_main] * 8 + [small] * 4 + [bi_row, bf_row, mlg]
    if not zero_init:
        in_specs += [pl.BlockSpec((None, None, 2, ML_HEADS, ML_DH, ML_DH), lambda b, c: (b, layer, 0, 0, 0, 0)),
                     pl.BlockSpec((None, None, 2, ML_HEADS, ML_DH), lambda b, c: (b, layer, 0, 0, 0)),
                     pl.BlockSpec((None, M_ROWS, SMALL_W), lambda b, c: (b, 0, 0))]
        args += list(state)
    out_specs = [pl.BlockSpec((nc, CHUNK, ML_W), lambda b, c: (b, 0, 0)),
                 pl.BlockSpec((None, 2, ML_HEADS, ML_DH, ML_DH), lambda b, c: (b, 0, 0, 0, 0)),
                 pl.BlockSpec((None, 2, ML_HEADS, ML_DH), lambda b, c: (b, 0, 0, 0)),
                 pl.BlockSpec((None, M_ROWS, SMALL_W), lambda b, c: (b, 0, 0))]
    out_shape = [jax.ShapeDtypeStruct((bsz * nc, CHUNK, ML_W), BF16),
                 jax.ShapeDtypeStruct((bsz, 2, ML_HEADS, ML_DH, ML_DH), F32),
                 jax.ShapeDtypeStruct((bsz, 2, ML_HEADS, ML_DH), F32),
                 jax.ShapeDtypeStruct((bsz, M_ROWS, SMALL_W), F32)]
    return pl.pallas_call(
        functools.partial(_mlstm_kernel, zero_init=zero_init, nc=nc),
        grid=(bsz, nc),
        in_specs=in_specs,
        out_specs=out_specs,
        out_shape=out_shape,
        scratch_shapes=[pltpu.VMEM((ML_DH, 2 * ML_HEADS * ML_AUG), F32),
                        pltpu.VMEM((M_ROWS, SMALL_W), F32),
                        pltpu.VMEM((CHUNK, 2 * ML_W), F32),
                        pltpu.VMEM((nc // 2, CHUNK, 2 * ML_W), F32)],
        compiler_params=_cparams(2),
        name="mlstm",
    )(*args)


def _ssd_chain(xs_ref, bc_ref, sm_ref, gb_ref, arow_ref, st_s, d, fresh):
    ys, new_st = [], []
    tri = _tri_mask(d)
    p = _softplus(sm_ref[...] + gb_ref[...])
    cum = _scan_sum(tri, p * arow_ref[...])
    p_t = p.T
    cum_t = cum.T
    last = CHUNK - 1 if d == 0 else 0
    lane0 = d * SSM_HEADS
    a_last = cum[last:last + 1, :]

    sel_p = _lane_selector(lane0, SSM_HEADS, SSM_P)
    to_end_x = _select_dot(jnp.exp(a_last - cum) * p, sel_p, 2)
    if not fresh:
        ea_x = _select_dot(jnp.exp(cum), sel_p, 2)
        ea_last_x = _select_dot(jnp.broadcast_to(jnp.exp(a_last), (8, SMALL_W)), sel_p, 3)[0:1, :]

    lo = lax.broadcasted_iota(jnp.int32, (CHUNK, 2 * SSM_P), 1) < SSM_P
    zero_b = jnp.zeros((CHUNK, 2 * SSM_P), BF16)

    for gi in range(SSM_G):
        bg = bc_ref[:, gi * SSM_N:(gi + 1) * SSM_N]
        cg = bc_ref[:, (SSM_G + gi) * SSM_N:(SSM_G + gi + 1) * SSM_N]
        cb = _dot_nt(cg, bg)
        gcols = slice(gi * SSM_R * SSM_P, (gi + 1) * SSM_R * SSM_P)
        y_pairs = []
        for pr in range(SSM_R // 2):
            h0 = gi * SSM_R + 2 * pr
            lhs = []
            for hh in (h0, h0 + 1):
                lane = lane0 + hh
                seg = _lane_bcast(cum, lane) - cum_t[lane:lane + 1, :]
                decay = jnp.exp(jnp.where(tri, seg, -jnp.inf))
                lhs.append((cb * decay * p_t[lane:lane + 1, :]).astype(BF16))
            xp = xs_ref[:, h0 * SSM_P:(h0 + 2) * SSM_P]
            rhs = jnp.concatenate([jnp.where(lo, xp, zero_b), jnp.where(lo, zero_b, xp)], axis=0)
            y_pairs.append(_dot(jnp.concatenate(lhs, axis=1), rhs))
        y_g = jnp.concatenate(y_pairs, axis=1)
        xw = (xs_ref[:, gcols].astype(F32) * to_end_x[:, gcols]).astype(BF16)
        upd = _dot_tn(bg, xw)
        if not fresh:
            st_old = st_s[:, d * SSM_W + gi * SSM_R * SSM_P:d * SSM_W + (gi + 1) * SSM_R * SSM_P]
            y_g = y_g + ea_x[:, gcols] * _dot(cg, st_old.astype(BF16))
            upd = st_old * ea_last_x[:, gcols] + upd
        ys.append(y_g)
        new_st.append(upd)
    return jnp.concatenate(ys, axis=1), jnp.concatenate(new_st, axis=1)


def _ssd_finish(ysum, xs_ref, z_ref, dsk_ref, ng_ref):
    y = ysum + dsk_ref[...] * xs_ref[...].astype(F32)
    yz = y * _silu(z_ref[...].astype(F32))
    return (yz * _rms(yz) * ng_ref[...]).astype(BF16)


def _ssd_kernel(*refs, zero_init, nc):
    refs = list(refs)
    xf, xb, bcf, bcb, zf, zb, smf, smb, gb_ref, arow_ref, dsk_ref, ng_ref = refs[:12]
    pos = 12
    if not zero_init:
        s0_ref = refs[pos]
        pos += 1
    y_out_ref, s_out_ref, st_s, ycur, stash = refs[pos:]

    c = pl.program_id(1)

    def scan_step(fresh):
        y_f, st_f = _ssd_chain(xf, bcf, smf, gb_ref, arow_ref, st_s, 0, fresh)
        y_b, st_b = _ssd_chain(xb, bcb, smb, gb_ref, arow_ref, st_s, 1, fresh)
        ycur[...] = jnp.concatenate([y_f, y_b], axis=1)
        st_s[...] = jnp.concatenate([st_f, st_b], axis=1)

    if zero_init:
        pl.when(c == 0)(functools.partial(scan_step, True))
        pl.when(c > 0)(functools.partial(scan_step, False))
    else:
        @pl.when(c == 0)
        def _():
            st_s[:, :SSM_W] = s0_ref[0].T
            st_s[:, SSM_W:] = s0_ref[1].T

        scan_step(False)

    half = nc // 2

    @pl.when(c < half)
    def _():
        stash[c] = ycur[...]

    @pl.when(c >= half)
    def _():
        s = nc - 1 - c
        y_out_ref[c] = _ssd_finish(ycur[:, :SSM_W] + stash[s, :, SSM_W:], xf, zf, dsk_ref, ng_ref)
        y_out_ref[s] = _ssd_finish(stash[s, :, :SSM_W] + ycur[:, SSM_W:], xb, zb, dsk_ref, ng_ref)

    @pl.when(c == nc - 1)
    def _():
        s_out_ref[0] = st_s[:, :SSM_W].T
        s_out_ref[1] = st_s[:, SSM_W:].T


def _ssd(p_main, small, gb_row, a_row, dsk, ng, state, layer, *, bsz, seq):
    nc = seq // CHUNK
    assert nc % 2 == 0
    zero_init = state is None
    const = lambda b, c: (0, 0)
    in_specs = (_scan_specs(nc, SSM_W, COL_XS) + _scan_specs(nc, BC_W, COL_BC) + _scan_specs(nc, SSM_W, COL_Z)
                + _scan_specs(nc, SMALL_W, BLK_DT)
                + [pl.BlockSpec((1, SMALL_W), const), pl.BlockSpec((1, SMALL_W), const),
                   pl.BlockSpec((1, SSM_W), const), pl.BlockSpec((1, SSM_W), const)])
    args = [p_main] * 6 + [small, small, gb_row, a_row, dsk, ng]
    if not zero_init:
        in_specs.append(pl.BlockSpec((None, None, 2, SSM_W, SSM_N), lambda b, c: (b, layer, 0, 0, 0)))
        args.append(state)
    return pl.pallas_call(
        functools.partial(_ssd_kernel, zero_init=zero_init, nc=nc),
        grid=(bsz, nc),
        in_specs=in_specs,
        out_specs=[pl.BlockSpec((nc, CHUNK, SSM_W), lambda b, c: (b, 0, 0)),
                   pl.BlockSpec((None, 2, SSM_W, SSM_N), lambda b, c: (b, 0, 0, 0))],
        out_shape=[jax.ShapeDtypeStruct((bsz * nc, CHUNK, SSM_W), BF16),
                   jax.ShapeDtypeStruct((bsz, 2, SSM_W, SSM_N), F32)],
        scratch_shapes=[pltpu.VMEM((SSM_N, 2 * SSM_W), F32),
                        pltpu.VMEM((CHUNK, 2 * SSM_W), F32),
                        pltpu.VMEM((nc // 2, CHUNK, 2 * SSM_W), F32)],
        compiler_params=_cparams(2),
        name="ssd",
    )(*args)


def _outproj_kernel(hml_ref, y_ref, x_ref, mod_ref, g_ref, gn_ref, w_ref, o_ref, hn_ref):
    gate = mod_ref[5:6, :]
    shift = mod_ref[6:7, :]
    scale1 = 1.0 + mod_ref[7:8, :]
    for r in range(x_ref.shape[0] // ROW_GROUP):
        rows = slice(r * ROW_GROUP, (r + 1) * ROW_GROUP)
        mix = _dot(hml_ref[rows, :], w_ref[0:ML_W, :]) + _dot(y_ref[rows, :], w_ref[ML_W:ML_W + SSM_W, :])
        o = x_ref[rows, :] + gate * ((mix * _rms(mix)) * g_ref[...])
        o_ref[rows, :] = o
        hn_ref[rows, :] = ((o * _rms(o)) * gn_ref[...] * scale1 + shift).astype(BF16)


def _outproj(hml, y, x, mod, g, g_next, w, *, tm):
    t = x.shape[0]
    rows_per_mod = t // mod.shape[0]
    row_block = pl.BlockSpec((tm, D_MODEL), lambda i: (i, 0))
    return pl.pallas_call(
        _outproj_kernel,
        grid=(t // tm,),
        in_specs=[pl.BlockSpec((tm, ML_W), lambda i: (i, 0)),
                  pl.BlockSpec((tm, SSM_W), lambda i: (i, 0)),
                  row_block,
                  pl.BlockSpec((None, N_MOD, D_MODEL), lambda i: ((i * tm) // rows_per_mod, 0, 0)),
                  pl.BlockSpec((1, D_MODEL), lambda i: (0, 0)),
                  pl.BlockSpec((1, D_MODEL), lambda i: (0, 0)),
                  pl.BlockSpec((ML_W + SSM_W, D_MODEL), lambda i: (0, 0))],
        out_specs=[row_block, row_block],
        out_shape=[jax.ShapeDtypeStruct((t, D_MODEL), F32), jax.ShapeDtypeStruct((t, D_MODEL), BF16)],
        compiler_params=_cparams(1),
        name="outproj",
    )(hml, y, x, mod, g, g_next, w)


def _prepare_params(norm_g, w_in, gate_bias, dt_bias, a_log, d_skip, conv_w, conv_b, ml_norm_g, ssm_norm_g,
                    w_out, w_gate, w_up, w_down):
    w_main, w_small = _split_w_in(jnp.swapaxes(w_in, 0, 1))
    return dict(
        g=[norm_g[i].reshape(1, D_MODEL) for i in range(6)],
        g_first=jnp.stack([norm_g[0], norm_g[2]], axis=0),
        w_main=w_main, w_small=w_small,
        bi_row=_lane_block(gate_bias[0].reshape(1, -1)), bf_row=_lane_block(gate_bias[1].reshape(1, -1)),
        bd_row=_lane_block(dt_bias.reshape(1, -1)), a_row=_lane_block(-jnp.exp(a_log.reshape(1, -1))),
        dsk=jnp.repeat(d_skip, SSM_P).reshape(1, SSM_W),
        conv_w=conv_w, conv_b=conv_b.reshape(1, CONV_CH),
        mlg=ml_norm_g.reshape(1, ML_W), ng=ssm_norm_g.reshape(1, SSM_W),
        w_out=_cast_bf16(w_out),
        ffn1_w=(_cast_bf16(w_gate, 0), _cast_bf16(w_up, 0), _cast_bf16(w_down, 0)),
        ffn_w_f32=(w_gate, w_up, w_down),
    )


def _trunk_path(x, mod, p, ml_state, ssm_state, layer, seg_len, ffn2_w=None, *,
                tm_ffn=512, tf=512, tm_proj=512, tn_proj=3328):
    bsz, seq, _ = x.shape
    t = bsz * seq
    x = x.reshape(t, D_MODEL)
    cast_next = None if ffn2_w is not None else p['ffn_w_f32'] + (1,)
    res = _ffn(x, None, mod, p['g_first'], p['g'][1], *p['ffn1_w'], idx=0, tm=tm_ffn, tf=tf, cast_next=cast_next)
    x, h = res[:2]
    if ffn2_w is None:
        ffn2_w = tuple(res[2:])
    p_main, small = _inproj(h, p['w_main'], p['w_small'], p['conv_w'], p['conv_b'],
                            seg_len=seg_len, tm=tm_proj, tn=tn_proj)
    if ml_state is not None:
        c0, n0, m0 = ml_state
        m0 = m0[:, layer]
        pad = SMALL_W - 2 * ML_HEADS
        m0 = jnp.stack([jnp.pad(m0[:, 0], ((0, 0), (0, pad + ML_HEADS))),
                        jnp.pad(m0[:, 1], ((0, 0), (ML_HEADS, pad)))], axis=1)
        m0 = jnp.pad(m0, ((0, 0), (0, M_ROWS - 2), (0, 0)))
        ml_state = (c0, n0, m0)
        ssm_state = ssm_state.reshape(ssm_state.shape[:3] + (SSM_W, SSM_N))
    hml, new_c, new_n, new_m = _mlstm(p_main, small, p['bi_row'], p['bf_row'], p['mlg'], ml_state, layer,
                                      bsz=bsz, seq=seq)
    y, new_s = _ssd(p_main, small, p['bd_row'], p['a_row'], p['dsk'], p['ng'], ssm_state, layer, bsz=bsz, seq=seq)
    x, h = _outproj(hml.reshape(t, ML_W), y.reshape(t, SSM_W), x, mod, p['g'][3], p['g'][4], p['w_out'], tm=tm_proj)
    x = _ffn(x, h, mod, p['g'][4], p['g'][5], *ffn2_w, idx=1, tm=tm_ffn, tf=tf)
    new_m = new_m[:, :2, :2 * ML_HEADS].reshape(bsz, 2, 2, ML_HEADS)
    new_m = jnp.sum(jnp.where(jnp.eye(2, dtype=bool)[None, :, :, None], new_m, 0.0), axis=2)
    states = (new_c, new_n, new_m, new_s.reshape(bsz, 2, SSM_HEADS, SSM_P, SSM_N))
    return x.reshape(bsz, seq, D_MODEL), states, ffn2_w


def _stack_layers(parts):
    return parts[0][:, None] if len(parts) == 1 else jnp.stack(parts, axis=1)


def kernel(x_prompt, x_sample, state_mlstm_C, state_mlstm_n, state_mlstm_m, state_ssm, c, c_ctx, w_ada, b_ada,
           norm_g, w_in, gate_bias, dt_bias, a_log, d_skip, conv_w, conv_b, ml_norm_g, ssm_norm_g, w_out,
           ffn_w_gate, ffn_w_up, ffn_w_down):
    depth = w_in.shape[0]
    bd = x_sample.shape[0]
    y_p, y_s = x_prompt, x_sample
    new_states = []
    cvec = jnp.concatenate([c_ctx[None], c, jnp.zeros((8 - 1 - bd, D_MODEL), F32)], axis=0)
    for l in range(depth):
        p = _prepare_params(norm_g[l], w_in[l], gate_bias[l], dt_bias[l], a_log[l], d_skip[l], conv_w[l], conv_b[l],
                            ml_norm_g[l], ssm_norm_g[l], w_out[l], ffn_w_gate[l], ffn_w_up[l], ffn_w_down[l])
        mod = _ada(cvec, w_ada[l], b_ada[l].reshape(1, -1)).reshape(8, N_MOD, D_MODEL)
        y_p, states, ffn2_w = _trunk_path(y_p, mod[0:1], p, None, None, l, x_prompt.shape[1])
        lat_state = (state_mlstm_C, state_mlstm_n, state_mlstm_m)
        y_s, _, _ = _trunk_path(y_s, mod[1:1 + bd], p, lat_state, state_ssm, l, GRID_W, ffn2_w)
        new_states.append(states)
    return (y_p, y_s) + tuple(_stack_layers([st[k] for st in new_states]) for k in range(4))
```

```python
import functools

import jax
import jax.numpy as jnp
from jax import lax
from jax.experimental import pallas as pl
from jax.experimental.pallas import tpu as pltpu

F32 = jnp.float32
BF16 = jnp.bfloat16

D_MODEL = 2048
GRID_W = 64
ML_HEADS = 4
ML_W = 1024
ML_DH = ML_W // ML_HEADS
SSM_W = 1024
SSM_P = 64
SSM_HEADS = SSM_W // SSM_P
SSM_N = 128
SSM_G = 2
SSM_R = SSM_HEADS // SSM_G
CONV_K = 5
CONV_CH = SSM_W + 2 * SSM_G * SSM_N
BC_W = 2 * SSM_G * SSM_N
D_FF = 5632
CHUNK = 128
N_MOD = 9
EPS = 1e-6

MAIN_W = 4 * ML_W + SSM_W + CONV_CH
COL_Q, COL_K, COL_V, COL_O, COL_Z, COL_XS = 0, 1, 2, 3, 4, 5
COL_BC = (MAIN_W - BC_W) // BC_W
SMALL_W = 128
BLK_IG, BLK_FG, BLK_DT = 0, 1, 2
N_BLK = 3

VMEM_LIMIT = 48 * 1024 * 1024
BIG_VMEM_LIMIT = 60 * 1024 * 1024
ROW_GROUP = 256
CONV_ROWS = 256
CONV_TILE = 512


def _cparams(n_axes, vmem=VMEM_LIMIT):
    return pltpu.CompilerParams(dimension_semantics=("arbitrary",) * n_axes, vmem_limit_bytes=vmem)


def _dot(a, b):
    return jnp.dot(a, b, preferred_element_type=F32)


def _dot_nt(a, b):
    return lax.dot_general(a, b, (((1,), (1,)), ((), ())), preferred_element_type=F32)


def _dot_tn(a, b):
    return lax.dot_general(a, b, (((0,), (0,)), ((), ())), preferred_element_type=F32)


def _split_bf16(x, terms):
    parts, rest = [], x
    for i in range(terms):
        piece = rest.astype(BF16)
        parts.append(piece)
        if i + 1 < terms:
            rest = rest - piece.astype(F32)
    return parts


def _select_dot(x, sel, terms):
    return _dot(jnp.concatenate(_split_bf16(x, terms), axis=1), jnp.concatenate([sel] * terms, axis=0))


def _scan_sum(tri, x):
    tri_b = jnp.where(tri, 1.0, 0.0).astype(BF16)
    return _dot(jnp.concatenate([tri_b] * 3, axis=1), jnp.concatenate(_split_bf16(x, 3), axis=0))


def _lane_selector(first_lane, n_blocks, width):
    row = lax.broadcasted_iota(jnp.int32, (SMALL_W, n_blocks * width), 0)
    col = lax.broadcasted_iota(jnp.int32, (SMALL_W, n_blocks * width), 1)
    return jnp.where(row == first_lane + col // width, 1.0, 0.0).astype(BF16)


def _silu(x):
    return x * jax.nn.sigmoid(x)


def _softplus(x):
    return jnp.maximum(x, 0.0) + jnp.log(1.0 + jnp.exp(-jnp.abs(x)))


def _log_sigmoid(x):
    return jnp.minimum(x, 0.0) - jnp.log(1.0 + jnp.exp(-jnp.abs(x)))


def _rms(x):
    return lax.rsqrt(jnp.mean(x * x, axis=-1, keepdims=True) + EPS)


def _cast_kernel(x_ref, o_ref):
    o_ref[...] = x_ref[...].astype(BF16)


def _cast_bf16(w, idx=None):
    rows, cols = w.shape[-2:]
    first = 0 if idx is None else idx
    w2 = w.reshape(-1, cols)
    tr = max(8, min(rows, (8 * 1024 * 1024) // (4 * cols) // 256 * 256))
    while rows % tr:
        tr //= 2
    n_blocks = rows // tr
    return pl.pallas_call(
        _cast_kernel,
        grid=(n_blocks,),
        in_specs=[pl.BlockSpec((tr, cols), lambda i: (first * n_blocks + i, 0))],
        out_specs=pl.BlockSpec((tr, cols), lambda i: (i, 0)),
        out_shape=jax.ShapeDtypeStruct((rows, cols), BF16),
        compiler_params=_cparams(1),
        name="cast_bf16",
    )(w2)


def _lane_block(cols):
    return jnp.concatenate([cols, jnp.zeros(cols.shape[:-1] + (SMALL_W - cols.shape[-1],), cols.dtype)], axis=-1)


W_IN_GATES = 4 * ML_W
W_IN_Z = W_IN_GATES + 4 * ML_HEADS
W_IN_DT = W_IN_Z + SSM_W + CONV_CH
W_IN_COLS = W_IN_DT + 2 * SSM_HEADS


def _split_w_in_rows(w_ref, om_ref, os_ref):
    om_ref[:W_IN_GATES, :] = w_ref[:W_IN_GATES, :].astype(BF16)
    om_ref[W_IN_GATES:, :] = w_ref[W_IN_Z:W_IN_DT, :].astype(BF16)
    n_ig = 2 * ML_HEADS

    def row_block(rows):
        return jnp.concatenate([rows, jnp.zeros((SMALL_W - rows.shape[0], rows.shape[1]), F32)], axis=0)

    os_ref[...] = jnp.concatenate([row_block(w_ref[W_IN_GATES:W_IN_GATES + n_ig, :]),
                                   row_block(w_ref[W_IN_GATES + n_ig:W_IN_Z, :]),
                                   row_block(w_ref[W_IN_DT:, :])], axis=0).astype(BF16)


def _ada_kernel(c_ref, w_ref, b_ref, o_ref):
    s = _silu(c_ref[...]).astype(BF16)
    o_ref[...] = _dot(s, w_ref[...].astype(BF16)) + b_ref[...]


def _ada(cv, w, b):
    n = w.shape[1]
    tn = 1024
    return pl.pallas_call(
        _ada_kernel,
        grid=(n // tn,),
        in_specs=[pl.BlockSpec((cv.shape[0], D_MODEL), lambda j: (0, 0)),
                  pl.BlockSpec((D_MODEL, tn), lambda j: (0, j)),
                  pl.BlockSpec((1, tn), lambda j: (0, j))],
        out_specs=pl.BlockSpec((cv.shape[0], tn), lambda j: (0, j)),
        out_shape=jax.ShapeDtypeStruct((cv.shape[0], n), F32),
        compiler_params=_cparams(1),
        name="ada_mod",
    )(cv, w, b)


def _ffn_kernel(x_ref, mod_ref, gin_ref, gout_ref, wg_ref, wu_ref, wd_ref, *rest, mod_base, first, side_cast):
    rest = list(rest)
    if side_cast:
        assert first
        src, (w_in_ref, w_out_ref), dst, (w_main_ref, w_small_ref, w_out_b_ref) = (
            rest[:3], rest[3:5], rest[7:10], rest[10:13])
        rest = rest[5:7] + rest[13:]
        for s_ref, d_ref in zip(src, dst):
            d_ref[...] = s_ref[...].astype(BF16)

        @pl.when(pl.program_id(1) == 0)
        def _():
            _split_w_in_rows(w_in_ref, w_main_ref, w_small_ref)
            w_out_b_ref[...] = w_out_ref[...].astype(BF16)
    if first:
        o_ref, hn_ref, h_ref = rest
    else:
        h_ref, o_ref = rest
    j = pl.program_id(1)
    last = pl.num_programs(1) - 1
    groups = [slice(r * ROW_GROUP, (r + 1) * ROW_GROUP) for r in range(x_ref.shape[0] // ROW_GROUP)]

    def swiglu(h):
        a = (_silu(_dot(h, wg_ref[...])) * _dot(h, wu_ref[...])).astype(BF16)
        return _dot(a, wd_ref[...])

    @pl.when(j == 0)
    def _():
        if first:
            shift = mod_ref[mod_base:mod_base + 1, :]
            scale1 = 1.0 + mod_ref[mod_base + 1:mod_base + 2, :]
            for rows in groups:
                x = x_ref[rows, :]
                h = ((x * _rms(x)) * gin_ref[0:1, :] * scale1 + shift).astype(BF16)
                h_ref[rows, :] = h
                o_ref[rows, :] = swiglu(h)
        else:
            o_ref[...] = swiglu(h_ref[...])

    @pl.when(jnp.logical_and(j > 0, j < last))
    def _():
        o_ref[...] += swiglu(h_ref[...])

    @pl.when(j == last)
    def _():
        gate = 0.5 * mod_ref[mod_base + 2:mod_base + 3, :]
        for rows in groups:
            y = o_ref[rows, :] + swiglu(h_ref[rows, :])
            o = x_ref[rows, :] + gate * ((y * _rms(y)) * gout_ref[...])
            o_ref[rows, :] = o
            if first:
                hn = (o * _rms(o)) * gin_ref[1:2, :] * (1.0 + mod_ref[4:5, :]) + mod_ref[3:4, :]
                hn_ref[rows, :] = hn.astype(BF16)


def _ffn(x, h, mod, g_in, g_out, wg, wu, wd, *, idx, tm, tf, cast_next=None):
    t = x.shape[0]
    rows_per_mod = t // mod.shape[0]
    first = h is None
    ni, nj = t // tm, D_FF // tf
    row_block = pl.BlockSpec((tm, D_MODEL), lambda i, j: (i, 0))
    in_specs = [row_block,
                pl.BlockSpec((None, N_MOD, D_MODEL), lambda i, j: ((i * tm) // rows_per_mod, 0, 0)),
                pl.BlockSpec(g_in.shape, lambda i, j: (0, 0)),
                pl.BlockSpec((1, D_MODEL), lambda i, j: (0, 0)),
                pl.BlockSpec((D_MODEL, tf), lambda i, j: (0, j)),
                pl.BlockSpec((D_MODEL, tf), lambda i, j: (0, j)),
                pl.BlockSpec((tf, D_MODEL), lambda i, j: (j, 0))]
    args = [x, mod, g_in, g_out, wg, wu, wd]
    if first:
        out_specs = [row_block, row_block]
        out_shape = [jax.ShapeDtypeStruct((t, D_MODEL), F32), jax.ShapeDtypeStruct((t, D_MODEL), BF16)]
        scratch = [pltpu.VMEM((tm, D_MODEL), BF16)]
    else:
        in_specs.append(row_block)
        args.append(h)
        out_specs = [row_block]
        out_shape = [jax.ShapeDtypeStruct((t, D_MODEL), F32)]
        scratch = []
    if cast_next is not None:
        k = cast_next[3]
        dr = D_MODEL // ni
        in_specs += [pl.BlockSpec((None, dr, tf), lambda i, j: (k, i, j)),
                     pl.BlockSpec((None, dr, tf), lambda i, j: (k, i, j)),
                     pl.BlockSpec((None, tf, dr), lambda i, j: (k, j, i))]
        in_specs += [pl.BlockSpec((W_IN_COLS, dr), lambda i, j: (0, i)),
                     pl.BlockSpec((dr, D_MODEL), lambda i, j: (i, 0))]
        args += list(cast_next[:3]) + list(cast_next[4:6])
        out_specs += [pl.BlockSpec((dr, tf), lambda i, j: (i, j)),
                      pl.BlockSpec((dr, tf), lambda i, j: (i, j)),
                      pl.BlockSpec((tf, dr), lambda i, j: (j, i)),
                      pl.BlockSpec((MAIN_W, dr), lambda i, j: (0, i)),
                      pl.BlockSpec((N_BLK * SMALL_W, dr), lambda i, j: (0, i)),
                      pl.BlockSpec((dr, D_MODEL), lambda i, j: (i, 0))]
        out_shape += [jax.ShapeDtypeStruct((D_MODEL, D_FF), BF16), jax.ShapeDtypeStruct((D_MODEL, D_FF), BF16),
                      jax.ShapeDtypeStruct((D_FF, D_MODEL), BF16),
                      jax.ShapeDtypeStruct((MAIN_W, D_MODEL), BF16),
                      jax.ShapeDtypeStruct((N_BLK * SMALL_W, D_MODEL), BF16),
                      jax.ShapeDtypeStruct((ML_W + SSM_W, D_MODEL), BF16)]
    out = pl.pallas_call(
        functools.partial(_ffn_kernel, mod_base=6 * idx, first=first, side_cast=cast_next is not None),
        grid=(ni, nj),
        in_specs=in_specs,
        out_specs=out_specs,
        out_shape=out_shape,
        scratch_shapes=scratch,
        compiler_params=_cparams(2, VMEM_LIMIT if cast_next is None else BIG_VMEM_LIMIT),
        name="ffn",
    )(*args)
    return out if len(out) > 1 else out[0]


def _conv_silu(u, cw_ref, cb_ref, cols, seg_len):
    rows = u.shape[0]
    pos = lax.broadcasted_iota(jnp.int32, u.shape, 0) % seg_len
    acc = jnp.zeros_like(u) + cb_ref[:, cols]
    for j in range(CONV_K):
        off = j - CONV_K // 2
        shifted = u if off == 0 else pltpu.roll(u, (-off) % rows, 0)
        valid = jnp.logical_and(pos + off >= 0, pos + off < seg_len)
        acc = acc + jnp.where(valid, shifted, 0.0) * cw_ref[j:j + 1, cols]
    return _silu(acc)


def _inproj_kernel(h_ref, w_ref, ws_ref, cw_ref, cb_ref, o_ref, os_ref, *, conv_col0, seg_len):
    j = pl.program_id(1)
    last = pl.num_programs(1) - 1

    @pl.when(j != last)
    def _():
        o_ref[...] = _dot_nt(h_ref[...], w_ref[...]).astype(BF16)

    @pl.when(j == last)
    def _():
        os_ref[...] = _dot_nt(h_ref[...], ws_ref[...])
        for r in range(h_ref.shape[0] // CONV_ROWS):
            rows = slice(r * CONV_ROWS, (r + 1) * CONV_ROWS)
            acc = _dot_nt(h_ref[rows, :], w_ref[...])
            o_ref[rows, :conv_col0] = acc[:, :conv_col0].astype(BF16)
            for ct in range(CONV_CH // CONV_TILE):
                cols = slice(ct * CONV_TILE, (ct + 1) * CONV_TILE)
                u = acc[:, conv_col0 + ct * CONV_TILE:conv_col0 + (ct + 1) * CONV_TILE]
                o_ref[rows, conv_col0 + ct * CONV_TILE:conv_col0 + (ct + 1) * CONV_TILE] = (
                    _conv_silu(u, cw_ref, cb_ref, cols, seg_len).astype(BF16))


def _inproj(h, w_main, w_small, conv_w, conv_b, *, seg_len, tm, tn):
    t = h.shape[0]
    n_tiles = MAIN_W // tn
    conv_col0 = MAIN_W - CONV_CH - (n_tiles - 1) * tn
    assert n_tiles * tn == MAIN_W and conv_col0 >= 0 and conv_col0 % 128 == 0
    assert tm % CONV_ROWS == 0 and CONV_ROWS % seg_len == 0
    return pl.pallas_call(
        functools.partial(_inproj_kernel, conv_col0=conv_col0, seg_len=seg_len),
        grid=(t // tm, n_tiles),
        in_specs=[pl.BlockSpec((tm, D_MODEL), lambda i, j: (i, 0)),
                  pl.BlockSpec((tn, D_MODEL), lambda i, j: (j, 0)),
                  pl.BlockSpec((N_BLK * SMALL_W, D_MODEL), lambda i, j: (0, 0)),
                  pl.BlockSpec((CONV_K, CONV_CH), lambda i, j: (0, 0)),
                  pl.BlockSpec((1, CONV_CH), lambda i, j: (0, 0))],
        out_specs=[pl.BlockSpec((tm, tn), lambda i, j: (i, j)),
                   pl.BlockSpec((tm, N_BLK * SMALL_W), lambda i, j: (i, 0))],
        out_shape=[jax.ShapeDtypeStruct((t, MAIN_W), BF16),
                   jax.ShapeDtypeStruct((t, N_BLK * SMALL_W), F32)],
        compiler_params=_cparams(2, BIG_VMEM_LIMIT),
        name="inproj",
    )(h, w_main, w_small, conv_w, conv_b)


def _tri_mask(direction):
    row = lax.broadcasted_iota(jnp.int32, (CHUNK, CHUNK), 0)
    col = lax.broadcasted_iota(jnp.int32, (CHUNK, CHUNK), 1)
    return (col <= row) if direction == 0 else (col >= row)


def _scan_specs(nc, width, col):
    return [pl.BlockSpec((CHUNK, width), lambda b, c: (b * nc + c, col)),
            pl.BlockSpec((CHUNK, width), lambda b, c: (b * nc + nc - 1 - c, col))]


def _scan_max(x, d):
    idx = lax.broadcasted_iota(jnp.int32, x.shape, 0)
    k = 1
    while k < CHUNK:
        if d == 0:
            shifted, valid = pltpu.roll(x, k, 0), idx >= k
        else:
            shifted, valid = pltpu.roll(x, CHUNK - k, 0), idx < CHUNK - k
        x = jnp.maximum(x, jnp.where(valid, shifted, -jnp.inf))
        k *= 2
    return x


def _lane_bcast(x, lane):
    return jnp.broadcast_to(x[:, lane:lane + 1], (x.shape[0], SMALL_W))


ML_AUG = ML_DH + SMALL_W
M_ROWS = 8


def _mlstm_chain(q_ref, k_ref, v_ref, ga_ref, gf_ref, bi_ref, bf_ref, cn_s, m_s, d, fresh):
    tri = _tri_mask(d)
    scale = ML_DH ** -0.5
    log_i = ga_ref[...] + bi_ref[...]
    log_f = _log_sigmoid(gf_ref[...] + bf_ref[...])
    b = _scan_sum(tri, log_f)
    g = log_i - b
    m_prev = jnp.zeros((1, SMALL_W), F32) if fresh else m_s[d:d + 1, :]
    m_inter = b + m_prev
    m_t = jnp.maximum(m_inter, b + _scan_max(g, d))
    c1 = b - m_t
    inter = jnp.exp(m_inter - m_t)
    e_negm = jnp.exp(-m_t)
    last = CHUNK - 1 if d == 0 else 0
    b_last = b[last:last + 1, :]
    log_w = b_last - b + log_i
    m_end = b_last + m_prev
    m_new = jnp.maximum(m_end, jnp.max(log_w, axis=0, keepdims=True))
    w = jnp.exp(log_w - m_new) * scale
    decay = jnp.exp(m_end - m_new)
    g_t = g.T
    ones = jnp.ones((CHUNK, SMALL_W), BF16)

    hs, new_cn = [], []
    for h in range(ML_HEADS):
        lane = d * ML_HEADS + h
        log_d = jnp.where(tri, _lane_bcast(c1, lane) + g_t[lane:lane + 1, :], -jnp.inf)
        dmat = jnp.exp(log_d) * scale
        sl = slice(h * ML_DH, (h + 1) * ML_DH)
        qh = q_ref[:, sl]
        kh = k_ref[:, sl]
        v_aug = jnp.concatenate([v_ref[:, sl], ones], axis=1)
        scores = (_dot_nt(qh, kh) * dmat).astype(BF16)
        r = _dot(scores, v_aug)
        if not fresh:
            cn_old = cn_s[:, lane * ML_AUG:(lane + 1) * ML_AUG]
            inter_rep = _lane_bcast(inter, lane)
            r = r + jnp.concatenate([inter_rep] * (ML_AUG // SMALL_W), axis=1) * _dot(qh, cn_old.astype(BF16))
        rden = 1.0 / jnp.maximum(jnp.abs(r[:, ML_DH:]), _lane_bcast(e_negm, lane))
        hs.append(r[:, :ML_DH] * jnp.concatenate([rden] * (ML_DH // SMALL_W), axis=1))

        w_rep = _lane_bcast(w, lane)
        kw = (kh.astype(F32) * jnp.concatenate([w_rep] * (ML_DH // SMALL_W), axis=1)).astype(BF16)
        upd = _dot_tn(kw, v_aug)
        new_cn.append(upd if fresh else decay[:, lane:lane + 1] * cn_old + upd)
    return jnp.concatenate(hs, axis=1), new_cn, m_new


def _mlstm_finish(hsum, og_ref, mlg_ref):
    outs = []
    for h in range(ML_HEADS):
        sl = slice(h * ML_DH, (h + 1) * ML_DH)
        hs = hsum[:, sl]
        cen = hs - jnp.mean(hs, axis=1, keepdims=True)
        var = jnp.mean(cen * cen, axis=1, keepdims=True)
        hn = cen * lax.rsqrt(var + EPS) * mlg_ref[:, sl]
        outs.append((jax.nn.sigmoid(og_ref[:, sl].astype(F32)) * hn).astype(BF16))
    return jnp.concatenate(outs, axis=1)


def _mlstm_kernel(*refs, zero_init, nc):
    refs = list(refs)
    qf, qb, kf, kb, vf, vb, of, ob, gaf, gab, gff, gfb, bi_ref, bf_ref, mlg_ref = refs[:15]
    pos = 15
    if not zero_init:
        c0_ref, n0_ref, m0_ref = refs[pos:pos + 3]
        pos += 3
    h_out_ref, c_out_ref, n_out_ref, m_out_ref, cn_s, m_s, hcur, stash = refs[pos:]

    c = pl.program_id(1)

    def scan_step(fresh):
        h_f, cn_f, m_f = _mlstm_chain(qf, kf, vf, gaf, gff, bi_ref, bf_ref, cn_s, m_s, 0, fresh)
        h_b, cn_b, m_b = _mlstm_chain(qb, kb, vb, gab, gfb, bi_ref, bf_ref, cn_s, m_s, 1, fresh)
        hcur[...] = jnp.concatenate([h_f, h_b], axis=1)
        cn_s[...] = jnp.concatenate(cn_f + cn_b, axis=1)
        m_s[...] = jnp.concatenate([m_f, m_b] + [jnp.zeros_like(m_f)] * (M_ROWS - 2), axis=0)

    if zero_init:
        pl.when(c == 0)(functools.partial(scan_step, True))
        pl.when(c > 0)(functools.partial(scan_step, False))
    else:
        @pl.when(c == 0)
        def _():
            for r in range(2 * ML_HEADS):
                d, h = divmod(r, ML_HEADS)
                cn_s[:, r * ML_AUG:r * ML_AUG + ML_DH] = c0_ref[d, h]
                n_rows = jnp.broadcast_to(n0_ref[d, h:h + 1, :], (SMALL_W, ML_DH))
                cn_s[:, r * ML_AUG + ML_DH:(r + 1) * ML_AUG] = n_rows.T
            m_s[...] = m0_ref[...]

        scan_step(False)

    half = nc // 2

    @pl.when(c < half)
    def _():
        stash[c] = hcur[...]

    @pl.when(c >= half)
    def _():
        s = nc - 1 - c
        h_out_ref[c] = _mlstm_finish(hcur[:, :ML_W] + stash[s, :, ML_W:], of, mlg_ref)
        h_out_ref[s] = _mlstm_finish(stash[s, :, :ML_W] + hcur[:, ML_W:], ob, mlg_ref)

    @pl.when(c == nc - 1)
    def _():
        for r in range(2 * ML_HEADS):
            d, h = divmod(r, ML_HEADS)
            c_out_ref[d, h] = cn_s[:, r * ML_AUG:r * ML_AUG + ML_DH]
            n_out_ref[d, h:h + 1, :] = cn_s[:, r * ML_AUG + ML_DH:(r + 1) * ML_AUG].T[0:1, :]
        m_out_ref[...] = m_s[...]


def _mlstm(p_main, small, bi_row, bf_row, mlg, state, layer, *, bsz, seq):
    nc = seq // CHUNK
    assert nc % 2 == 0
    zero_init = state is None
    const = lambda b, c: (0, 0)
    in_specs = (_scan_specs(nc, ML_W, COL_Q) + _scan_specs(nc, ML_W, COL_K) + _scan_specs(nc, ML_W, COL_V)
                + _scan_specs(nc, ML_W, COL_O) + _scan_specs(nc, SMALL_W, BLK_IG) + _scan_specs(nc, SMALL_W, BLK_FG)
                + [pl.BlockSpec((1, SMALL_W), const), pl.BlockSpec((1, SMALL_W), const),
                   pl.BlockSpec((1, ML_W), const)])
    args = [p_main] * 8 + [small] * 4 + [bi_row, bf_row, mlg]
    if not zero_init:
        in_specs += [pl.BlockSpec((None, None, 2, ML_HEADS, ML_DH, ML_DH), lambda b, c: (b, layer, 0, 0, 0, 0)),
                     pl.BlockSpec((None, None, 2, ML_HEADS, ML_DH), lambda b, c: (b, layer, 0, 0, 0)),
                     pl.BlockSpec((None, M_ROWS, SMALL_W), lambda b, c: (b, 0, 0))]
        args += list(state)
    out_specs = [pl.BlockSpec((nc, CHUNK, ML_W), lambda b, c: (b, 0, 0)),
                 pl.BlockSpec((None, 2, ML_HEADS, ML_DH, ML_DH), lambda b, c: (b, 0, 0, 0, 0)),
                 pl.BlockSpec((None, 2, ML_HEADS, ML_DH), lambda b, c: (b, 0, 0, 0)),
                 pl.BlockSpec((None, M_ROWS, SMALL_W), lambda b, c: (b, 0, 0))]
    out_shape = [jax.ShapeDtypeStruct((bsz * nc, CHUNK, ML_W), BF16),
                 jax.ShapeDtypeStruct((bsz, 2, ML_HEADS, ML_DH, ML_DH), F32),
                 jax.ShapeDtypeStruct((bsz, 2, ML_HEADS, ML_DH), F32),
                 jax.ShapeDtypeStruct((bsz, M_ROWS, SMALL_W), F32)]
    return pl.pallas_call(
        functools.partial(_mlstm_kernel, zero_init=zero_init, nc=nc),
        grid=(bsz, nc),
        in_specs=in_specs,
        out_specs=out_specs,
        out_shape=out_shape,
        scratch_shapes=[pltpu.VMEM((ML_DH, 2 * ML_HEADS * ML_AUG), F32),
                        pltpu.VMEM((M_ROWS, SMALL_W), F32),
                        pltpu.VMEM((CHUNK, 2 * ML_W), F32),
                        pltpu.VMEM((nc // 2, CHUNK, 2 * ML_W), F32)],
        compiler_params=_cparams(2),
        name="mlstm",
    )(*args)


def _ssd_chain(xs_ref, bc_ref, sm_ref, gb_ref, arow_ref, st_s, d, fresh):
    ys, new_st = [], []
    tri = _tri_mask(d)
    p = _softplus(sm_ref[...] + gb_ref[...])
    cum = _scan_sum(tri, p * arow_ref[...])
    p_t = p.T
    cum_t = cum.T
    last = CHUNK - 1 if d == 0 else 0
    lane0 = d * SSM_HEADS
    a_last = cum[last:last + 1, :]

    sel_p = _lane_selector(lane0, SSM_HEADS, SSM_P)
    to_end_x = _select_dot(jnp.exp(a_last - cum) * p, sel_p, 2)
    if not fresh:
        ea_x = _select_dot(jnp.exp(cum), sel_p, 2)
        ea_last_x = _select_dot(jnp.broadcast_to(jnp.exp(a_last), (8, SMALL_W)), sel_p, 3)[0:1, :]

    lo = lax.broadcasted_iota(jnp.int32, (CHUNK, 2 * SSM_P), 1) < SSM_P
    zero_b = jnp.zeros((CHUNK, 2 * SSM_P), BF16)

    for gi in range(SSM_G):
        bg = bc_ref[:, gi * SSM_N:(gi + 1) * SSM_N]
        cg = bc_ref[:, (SSM_G + gi) * SSM_N:(SSM_G + gi + 1) * SSM_N]
        cb = _dot_nt(cg, bg)
        gcols = slice(gi * SSM_R * SSM_P, (gi + 1) * SSM_R * SSM_P)
        y_pairs = []
        for pr in range(SSM_R // 2):
            h0 = gi * SSM_R + 2 * pr
            lhs = []
            for hh in (h0, h0 + 1):
                lane = lane0 + hh
                seg = _lane_bcast(cum, lane) - cum_t[lane:lane + 1, :]
                decay = jnp.exp(jnp.where(tri, seg, -jnp.inf))
                lhs.append((cb * decay * p_t[lane:lane + 1, :]).astype(BF16))
            xp = xs_ref[:, h0 * SSM_P:(h0 + 2) * SSM_P]
            rhs = jnp.concatenate([jnp.where(lo, xp, zero_b), jnp.where(lo, zero_b, xp)], axis=0)
            y_pairs.append(_dot(jnp.concatenate(lhs, axis=1), rhs))
        y_g = jnp.concatenate(y_pairs, axis=1)
        xw = (xs_ref[:, gcols].astype(F32) * to_end_x[:, gcols]).astype(BF16)
        upd = _dot_tn(bg, xw)
        if not fresh:
            st_old = st_s[:, d * SSM_W + gi * SSM_R * SSM_P:d * SSM_W + (gi + 1) * SSM_R * SSM_P]
            y_g = y_g + ea_x[:, gcols] * _dot(cg, st_old.astype(BF16))
            upd = st_old * ea_last_x[:, gcols] + upd
        ys.append(y_g)
        new_st.append(upd)
    return jnp.concatenate(ys, axis=1), jnp.concatenate(new_st, axis=1)


def _ssd_finish(ysum, xs_ref, z_ref, dsk_ref, ng_ref):
    y = ysum + dsk_ref[...] * xs_ref[...].astype(F32)
    yz = y * _silu(z_ref[...].astype(F32))
    return (yz * _rms(yz) * ng_ref[...]).astype(BF16)


def _ssd_kernel(*refs, zero_init, nc):
    refs = list(refs)
    xf, xb, bcf, bcb, zf, zb, smf, smb, gb_ref, arow_ref, dsk_ref, ng_ref = refs[:12]
    pos = 12
    if not zero_init:
        s0_ref = refs[pos]
        pos += 1
    y_out_ref, s_out_ref, st_s, ycur, stash = refs[pos:]

    c = pl.program_id(1)

    def scan_step(fresh):
        y_f, st_f = _ssd_chain(xf, bcf, smf, gb_ref, arow_ref, st_s, 0, fresh)
        y_b, st_b = _ssd_chain(xb, bcb, smb, gb_ref, arow_ref, st_s, 1, fresh)
        ycur[...] = jnp.concatenate([y_f, y_b], axis=1)
        st_s[...] = jnp.concatenate([st_f, st_b], axis=1)

    if zero_init:
        pl.when(c == 0)(functools.partial(scan_step, True))
        pl.when(c > 0)(functools.partial(scan_step, False))
    else:
        @pl.when(c == 0)
        def _():
            st_s[:, :SSM_W] = s0_ref[0].T
            st_s[:, SSM_W:] = s0_ref[1].T

        scan_step(False)

    half = nc // 2

    @pl.when(c < half)
    def _():
        stash[c] = ycur[...]

    @pl.when(c >= half)
    def _():
        s = nc - 1 - c
        y_out_ref[c] = _ssd_finish(ycur[:, :SSM_W] + stash[s, :, SSM_W:], xf, zf, dsk_ref, ng_ref)
        y_out_ref[s] = _ssd_finish(stash[s, :, :SSM_W] + ycur[:, SSM_W:], xb, zb, dsk_ref, ng_ref)

    @pl.when(c == nc - 1)
    def _():
        s_out_ref[0] = st_s[:, :SSM_W].T
        s_out_ref[1] = st_s[:, SSM_W:].T


def _ssd(p_main, small, gb_row, a_row, dsk, ng, state, layer, *, bsz, seq):
    nc = seq // CHUNK
    assert nc % 2 == 0
    zero_init = state is None
    const = lambda b, c: (0, 0)
    in_specs = (_scan_specs(nc, SSM_W, COL_XS) + _scan_specs(nc, BC_W, COL_BC) + _scan_specs(nc, SSM_W, COL_Z)
                + _scan_specs(nc, SMALL_W, BLK_DT)
                + [pl.BlockSpec((1, SMALL_W), const), pl.BlockSpec((1, SMALL_W), const),
                   pl.BlockSpec((1, SSM_W), const), pl.BlockSpec((1, SSM_W), const)])
    args = [p_main] * 6 + [small, small, gb_row, a_row, dsk, ng]
    if not zero_init:
        in_specs.append(pl.BlockSpec((None, None, 2, SSM_W, SSM_N), lambda b, c: (b, layer, 0, 0, 0)))
        args.append(state)
    return pl.pallas_call(
        functools.partial(_ssd_kernel, zero_init=zero_init, nc=nc),
        grid=(bsz, nc),
        in_specs=in_specs,
        out_specs=[pl.BlockSpec((nc, CHUNK, SSM_W), lambda b, c: (b, 0, 0)),
                   pl.BlockSpec((None, 2, SSM_W, SSM_N), lambda b, c: (b, 0, 0, 0))],
        out_shape=[jax.ShapeDtypeStruct((bsz * nc, CHUNK, SSM_W), BF16),
                   jax.ShapeDtypeStruct((bsz, 2, SSM_W, SSM_N), F32)],
        scratch_shapes=[pltpu.VMEM((SSM_N, 2 * SSM_W), F32),
                        pltpu.VMEM((CHUNK, 2 * SSM_W), F32),
                        pltpu.VMEM((nc // 2, CHUNK, 2 * SSM_W), F32)],
        compiler_params=_cparams(2),
        name="ssd",
    )(*args)


def _outproj_kernel(hml_ref, y_ref, x_ref, mod_ref, g_ref, gn_ref, w_ref, o_ref, hn_ref):
    gate = mod_ref[5:6, :]
    shift = mod_ref[6:7, :]
    scale1 = 1.0 + mod_ref[7:8, :]
    for r in range(x_ref.shape[0] // ROW_GROUP):
        rows = slice(r * ROW_GROUP, (r + 1) * ROW_GROUP)
        mix = _dot(hml_ref[rows, :], w_ref[0:ML_W, :]) + _dot(y_ref[rows, :], w_ref[ML_W:ML_W + SSM_W, :])
        o = x_ref[rows, :] + gate * ((mix * _rms(mix)) * g_ref[...])
        o_ref[rows, :] = o
        hn_ref[rows, :] = ((o * _rms(o)) * gn_ref[...] * scale1 + shift).astype(BF16)


def _outproj(hml, y, x, mod, g, g_next, w, *, tm):
    t = x.shape[0]
    rows_per_mod = t // mod.shape[0]
    row_block = pl.BlockSpec((tm, D_MODEL), lambda i: (i, 0))
    return pl.pallas_call(
        _outproj_kernel,
        grid=(t // tm,),
        in_specs=[pl.BlockSpec((tm, ML_W), lambda i: (i, 0)),
                  pl.BlockSpec((tm, SSM_W), lambda i: (i, 0)),
                  row_block,
                  pl.BlockSpec((None, N_MOD, D_MODEL), lambda i: ((i * tm) // rows_per_mod, 0, 0)),
                  pl.BlockSpec((1, D_MODEL), lambda i: (0, 0)),
                  pl.BlockSpec((1, D_MODEL), lambda i: (0, 0)),
                  pl.BlockSpec((ML_W + SSM_W, D_MODEL), lambda i: (0, 0))],
        out_specs=[row_block, row_block],
        out_shape=[jax.ShapeDtypeStruct((t, D_MODEL), F32), jax.ShapeDtypeStruct((t, D_MODEL), BF16)],
        compiler_params=_cparams(1),
        name="outproj",
    )(hml, y, x, mod, g, g_next, w)


def _prepare_params(norm_g, w_in, gate_bias, dt_bias, a_log, d_skip, conv_w, conv_b, ml_norm_g, ssm_norm_g,
                    w_out, w_gate, w_up, w_down):
    return dict(
        g=[norm_g[i].reshape(1, D_MODEL) for i in range(6)],
        g_first=jnp.stack([norm_g[0], norm_g[2]], axis=0),
        w_in_t=jnp.swapaxes(w_in, 0, 1),
        bi_row=_lane_block(gate_bias[0].reshape(1, -1)), bf_row=_lane_block(gate_bias[1].reshape(1, -1)),
        bd_row=_lane_block(dt_bias.reshape(1, -1)), a_row=_lane_block(-jnp.exp(a_log.reshape(1, -1))),
        dsk=jnp.repeat(d_skip, SSM_P).reshape(1, SSM_W),
        conv_w=conv_w, conv_b=conv_b.reshape(1, CONV_CH),
        mlg=ml_norm_g.reshape(1, ML_W), ng=ssm_norm_g.reshape(1, SSM_W),
        w_out=w_out,
        ffn1_w=(_cast_bf16(w_gate, 0), _cast_bf16(w_up, 0), _cast_bf16(w_down, 0)),
        ffn_w_f32=(w_gate, w_up, w_down),
    )


def _trunk_path(x, mod, p, ml_state, ssm_state, layer, seg_len, shared_w=None, *,
                tm_ffn=512, tf=512, tm_proj=512, tn_proj=3328):
    bsz, seq, _ = x.shape
    t = bsz * seq
    x = x.reshape(t, D_MODEL)
    cast_next = None if shared_w is not None else p['ffn_w_f32'] + (1, p['w_in_t'], p['w_out'])
    res = _ffn(x, None, mod, p['g_first'], p['g'][1], *p['ffn1_w'], idx=0, tm=tm_ffn, tf=tf, cast_next=cast_next)
    x, h = res[:2]
    if shared_w is None:
        shared_w = dict(ffn2=tuple(res[2:5]), w_main=res[5], w_small=res[6], w_out=res[7])
    p_main, small = _inproj(h, shared_w['w_main'], shared_w['w_small'], p['conv_w'], p['conv_b'],
                            seg_len=seg_len, tm=tm_proj, tn=tn_proj)
    if ml_state is not None:
        c0, n0, m0 = ml_state
        m0 = m0[:, layer]
        pad = SMALL_W - 2 * ML_HEADS
        m0 = jnp.stack([jnp.pad(m0[:, 0], ((0, 0), (0, pad + ML_HEADS))),
                        jnp.pad(m0[:, 1], ((0, 0), (ML_HEADS, pad)))], axis=1)
        m0 = jnp.pad(m0, ((0, 0), (0, M_ROWS - 2), (0, 0)))
        ml_state = (c0, n0, m0)
        ssm_state = ssm_state.reshape(ssm_state.shape[:3] + (SSM_W, SSM_N))
    hml, new_c, new_n, new_m = _mlstm(p_main, small, p['bi_row'], p['bf_row'], p['mlg'], ml_state, layer,
                                      bsz=bsz, seq=seq)
    y, new_s = _ssd(p_main, small, p['bd_row'], p['a_row'], p['dsk'], p['ng'], ssm_state, layer, bsz=bsz, seq=seq)
    x, h = _outproj(hml.reshape(t, ML_W), y.reshape(t, SSM_W), x, mod, p['g'][3], p['g'][4], shared_w['w_out'],
                    tm=tm_proj)
    x = _ffn(x, h, mod, p['g'][4], p['g'][5], *shared_w['ffn2'], idx=1, tm=tm_ffn, tf=tf)
    new_m = new_m[:, :2, :2 * ML_HEADS].reshape(bsz, 2, 2, ML_HEADS)
    new_m = jnp.sum(jnp.where(jnp.eye(2, dtype=bool)[None, :, :, None], new_m, 0.0), axis=2)
    states = (new_c, new_n, new_m, new_s.reshape(bsz, 2, SSM_HEADS, SSM_P, SSM_N))
    return x.reshape(bsz, seq, D_MODEL), states, shared_w


def _stack_layers(parts):
    return parts[0][:, None] if len(parts) == 1 else jnp.stack(parts, axis=1)


def kernel(x_prompt, x_sample, state_mlstm_C, state_mlstm_n, state_mlstm_m, state_ssm, c, c_ctx, w_ada, b_ada,
           norm_g, w_in, gate_bias, dt_bias, a_log, d_skip, conv_w, conv_b, ml_norm_g, ssm_norm_g, w_out,
           ffn_w_gate, ffn_w_up, ffn_w_down):
    depth = w_in.shape[0]
    bd = x_sample.shape[0]
    y_p, y_s = x_prompt, x_sample
    new_states = []
    cvec = jnp.concatenate([c_ctx[None], c, jnp.zeros((8 - 1 - bd, D_MODEL), F32)], axis=0)
    for l in range(depth):
        p = _prepare_params(norm_g[l], w_in[l], gate_bias[l], dt_bias[l], a_log[l], d_skip[l], conv_w[l], conv_b[l],
                            ml_norm_g[l], ssm_norm_g[l], w_out[l], ffn_w_gate[l], ffn_w_up[l], ffn_w_down[l])
        mod = _ada(cvec, w_ada[l], b_ada[l].reshape(1, -1)).reshape(8, N_MOD, D_MODEL)
        y_p, states, shared_w = _trunk_path(y_p, mod[0:1], p, None, None, l, x_prompt.shape[1])
        lat_state = (state_mlstm_C, state_mlstm_n, state_mlstm_m)
        y_s, _, _ = _trunk_path(y_s, mod[1:1 + bd], p, lat_state, state_ssm, l, GRID_W, shared_w)
        new_states.append(states)
    return (y_p, y_s) + tuple(_stack_layers([st[k] for st in new_states]) for k in range(4))
```

```python
import functools

import jax
import jax.numpy as jnp
from jax import lax
from jax.experimental import pallas as pl
from jax.experimental.pallas import tpu as pltpu

F32 = jnp.float32
BF16 = jnp.bfloat16

D_MODEL = 2048
GRID_W = 64
ML_HEADS = 4
ML_W = 1024
ML_DH = ML_W // ML_HEADS
SSM_W = 1024
SSM_P = 64
SSM_HEADS = SSM_W // SSM_P
SSM_N = 128
SSM_G = 2
SSM_R = SSM_HEADS // SSM_G
CONV_K = 5
CONV_CH = SSM_W + 2 * SSM_G * SSM_N
BC_W = 2 * SSM_G * SSM_N
D_FF = 5632
CHUNK = 128
N_MOD = 9
EPS = 1e-6

MAIN_W = 4 * ML_W + SSM_W + CONV_CH
COL_Q, COL_K, COL_V, COL_O, COL_Z, COL_XS = 0, 1, 2, 3, 4, 5
COL_BC = (MAIN_W - BC_W) // BC_W
SMALL_W = 128
BLK_IG, BLK_FG, BLK_DT = 0, 1, 2
N_BLK = 3

VMEM_LIMIT = 48 * 1024 * 1024
BIG_VMEM_LIMIT = 60 * 1024 * 1024
ROW_GROUP = 256
CONV_ROWS = 256
CONV_TILE = 512


def _cparams(n_axes, vmem=VMEM_LIMIT):
    return pltpu.CompilerParams(dimension_semantics=("arbitrary",) * n_axes, vmem_limit_bytes=vmem)


def _dot(a, b):
    return jnp.dot(a, b, preferred_element_type=F32)


def _dot_nt(a, b):
    return lax.dot_general(a, b, (((1,), (1,)), ((), ())), preferred_element_type=F32)


def _dot_tn(a, b):
    return lax.dot_general(a, b, (((0,), (0,)), ((), ())), preferred_element_type=F32)


def _split_bf16(x, terms):
    parts, rest = [], x
    for i in range(terms):
        piece = rest.astype(BF16)
        parts.append(piece)
        if i + 1 < terms:
            rest = rest - piece.astype(F32)
    return parts


def _select_dot(x, sel, terms):
    return _dot(jnp.concatenate(_split_bf16(x, terms), axis=1), jnp.concatenate([sel] * terms, axis=0))


def _scan_sum(tri, x):
    tri_b = jnp.where(tri, 1.0, 0.0).astype(BF16)
    return _dot(jnp.concatenate([tri_b] * 3, axis=1), jnp.concatenate(_split_bf16(x, 3), axis=0))


def _lane_selector(first_lane, n_blocks, width):
    row = lax.broadcasted_iota(jnp.int32, (SMALL_W, n_blocks * width), 0)
    col = lax.broadcasted_iota(jnp.int32, (SMALL_W, n_blocks * width), 1)
    return jnp.where(row == first_lane + col // width, 1.0, 0.0).astype(BF16)


def _silu(x):
    return x * jax.nn.sigmoid(x)


def _softplus(x):
    return jnp.maximum(x, 0.0) + jnp.log(1.0 + jnp.exp(-jnp.abs(x)))


def _log_sigmoid(x):
    return jnp.minimum(x, 0.0) - jnp.log(1.0 + jnp.exp(-jnp.abs(x)))


def _rms(x):
    return lax.rsqrt(jnp.mean(x * x, axis=-1, keepdims=True) + EPS)


def _cast_kernel(x_ref, o_ref):
    o_ref[...] = x_ref[...].astype(BF16)


def _cast_bf16(w, idx=None):
    rows, cols = w.shape[-2:]
    first = 0 if idx is None else idx
    w2 = w.reshape(-1, cols)
    tr = max(8, min(rows, (8 * 1024 * 1024) // (4 * cols) // 256 * 256))
    while rows % tr:
        tr //= 2
    n_blocks = rows // tr
    return pl.pallas_call(
        _cast_kernel,
        grid=(n_blocks,),
        in_specs=[pl.BlockSpec((tr, cols), lambda i: (first * n_blocks + i, 0))],
        out_specs=pl.BlockSpec((tr, cols), lambda i: (i, 0)),
        out_shape=jax.ShapeDtypeStruct((rows, cols), BF16),
        compiler_params=_cparams(1),
        name="cast_bf16",
    )(w2)


def _lane_block(cols):
    return jnp.concatenate([cols, jnp.zeros(cols.shape[:-1] + (SMALL_W - cols.shape[-1],), cols.dtype)], axis=-1)


W_IN_GATES = 4 * ML_W
W_IN_Z = W_IN_GATES + 4 * ML_HEADS
W_IN_DT = W_IN_Z + SSM_W + CONV_CH
W_IN_COLS = W_IN_DT + 2 * SSM_HEADS


def _split_w_in_rows(w_ref, om_ref, os_ref):
    om_ref[:W_IN_GATES, :] = w_ref[:W_IN_GATES, :].astype(BF16)
    om_ref[W_IN_GATES:, :] = w_ref[W_IN_Z:W_IN_DT, :].astype(BF16)
    n_ig = 2 * ML_HEADS

    def row_block(rows):
        return jnp.concatenate([rows, jnp.zeros((SMALL_W - rows.shape[0], rows.shape[1]), F32)], axis=0)

    os_ref[...] = jnp.concatenate([row_block(w_ref[W_IN_GATES:W_IN_GATES + n_ig, :]),
                                   row_block(w_ref[W_IN_GATES + n_ig:W_IN_Z, :]),
                                   row_block(w_ref[W_IN_DT:, :])], axis=0).astype(BF16)


def _ada_kernel(c_ref, w_ref, b_ref, o_ref):
    s = _silu(c_ref[...]).astype(BF16)
    o_ref[...] = _dot(s, w_ref[...].astype(BF16)) + b_ref[...]


def _ada(cv, w, b):
    n = w.shape[1]
    tn = 1024
    return pl.pallas_call(
        _ada_kernel,
        grid=(n // tn,),
        in_specs=[pl.BlockSpec((cv.shape[0], D_MODEL), lambda j: (0, 0)),
                  pl.BlockSpec((D_MODEL, tn), lambda j: (0, j)),
                  pl.BlockSpec((1, tn), lambda j: (0, j))],
        out_specs=pl.BlockSpec((cv.shape[0], tn), lambda j: (0, j)),
        out_shape=jax.ShapeDtypeStruct((cv.shape[0], n), F32),
        compiler_params=_cparams(1),
        name="ada_mod",
    )(cv, w, b)


def _ffn2_pair_kernel(x_ref, mod_ref, gout_ref, wga, wua, wda, wgb, wub, wdb, h_ref, o_ref, *, mod_base):
    j = pl.program_id(1)
    last = pl.num_programs(1) - 1
    groups = [slice(r * ROW_GROUP, (r + 1) * ROW_GROUP) for r in range(x_ref.shape[0] // ROW_GROUP)]

    def swiglu(h, wg, wu, wd):
        a = (_silu(_dot(h, wg[...])) * _dot(h, wu[...])).astype(BF16)
        return _dot(a, wd[...])

    @pl.when(j == 0)
    def _():
        h = h_ref[...]
        o_ref[...] = swiglu(h, wga, wua, wda) + swiglu(h, wgb, wub, wdb)

    @pl.when(jnp.logical_and(j > 0, j < last))
    def _():
        h = h_ref[...]
        o_ref[...] += swiglu(h, wga, wua, wda) + swiglu(h, wgb, wub, wdb)

    @pl.when(j == last)
    def _():
        gate = 0.5 * mod_ref[mod_base + 2:mod_base + 3, :]
        for rows in groups:
            y = o_ref[rows, :] + swiglu(h_ref[rows, :], wga, wua, wda)
            o_ref[rows, :] = x_ref[rows, :] + gate * ((y * _rms(y)) * gout_ref[...])


def _ffn2_pair(x, h, mod, g_out, wg, wu, wd, *, tm, tf):
    t = x.shape[0]
    rows_per_mod = t // mod.shape[0]
    nt = D_FF // tf
    assert nt % 2 == 1 and nt >= 3
    nj = (nt + 1) // 2
    row_block = pl.BlockSpec((tm, D_MODEL), lambda i, j: (i, 0))
    ia = lambda j: 2 * j
    ib = lambda j: jnp.minimum(2 * j + 1, nt - 1)
    in_specs = [row_block,
                pl.BlockSpec((None, N_MOD, D_MODEL), lambda i, j: ((i * tm) // rows_per_mod, 0, 0)),
                pl.BlockSpec((1, D_MODEL), lambda i, j: (0, 0)),
                pl.BlockSpec((D_MODEL, tf), lambda i, j: (0, ia(j))),
                pl.BlockSpec((D_MODEL, tf), lambda i, j: (0, ia(j))),
                pl.BlockSpec((tf, D_MODEL), lambda i, j: (ia(j), 0)),
                pl.BlockSpec((D_MODEL, tf), lambda i, j: (0, ib(j))),
                pl.BlockSpec((D_MODEL, tf), lambda i, j: (0, ib(j))),
                pl.BlockSpec((tf, D_MODEL), lambda i, j: (ib(j), 0)),
                row_block]
    return pl.pallas_call(
        functools.partial(_ffn2_pair_kernel, mod_base=6),
        grid=(t // tm, nj),
        in_specs=in_specs,
        out_specs=row_block,
        out_shape=jax.ShapeDtypeStruct((t, D_MODEL), F32),
        compiler_params=_cparams(2, BIG_VMEM_LIMIT),
        name="ffn2pair",
    )(x, mod, g_out, wg, wu, wd, wg, wu, wd, h)


def _ffn_kernel(x_ref, mod_ref, gin_ref, gout_ref, wg_ref, wu_ref, wd_ref, *rest, mod_base, first, side_cast):
    rest = list(rest)
    if side_cast:
        assert first
        src, (w_in_ref, w_out_ref), dst, (w_main_ref, w_small_ref, w_out_b_ref) = (
            rest[:3], rest[3:5], rest[7:10], rest[10:13])
        rest = rest[5:7] + rest[13:]
        for s_ref, d_ref in zip(src, dst):
            d_ref[...] = s_ref[...].astype(BF16)

        @pl.when(pl.program_id(1) == 0)
        def _():
            _split_w_in_rows(w_in_ref, w_main_ref, w_small_ref)
            w_out_b_ref[...] = w_out_ref[...].astype(BF16)
    if first:
        o_ref, hn_ref, h_ref = rest
    else:
        h_ref, o_ref = rest
    j = pl.program_id(1)
    last = pl.num_programs(1) - 1
    groups = [slice(r * ROW_GROUP, (r + 1) * ROW_GROUP) for r in range(x_ref.shape[0] // ROW_GROUP)]

    def swiglu(h):
        a = (_silu(_dot(h, wg_ref[...])) * _dot(h, wu_ref[...])).astype(BF16)
        return _dot(a, wd_ref[...])

    @pl.when(j == 0)
    def _():
        if first:
            shift = mod_ref[mod_base:mod_base + 1, :]
            scale1 = 1.0 + mod_ref[mod_base + 1:mod_base + 2, :]
            for rows in groups:
                x = x_ref[rows, :]
                h = ((x * _rms(x)) * gin_ref[0:1, :] * scale1 + shift).astype(BF16)
                h_ref[rows, :] = h
                o_ref[rows, :] = swiglu(h)
        else:
            o_ref[...] = swiglu(h_ref[...])

    @pl.when(jnp.logical_and(j > 0, j < last))
    def _():
        o_ref[...] += swiglu(h_ref[...])

    @pl.when(j == last)
    def _():
        gate = 0.5 * mod_ref[mod_base + 2:mod_base + 3, :]
        for rows in groups:
            y = o_ref[rows, :] + swiglu(h_ref[rows, :])
            o = x_ref[rows, :] + gate * ((y * _rms(y)) * gout_ref[...])
            o_ref[rows, :] = o
            if first:
                hn = (o * _rms(o)) * gin_ref[1:2, :] * (1.0 + mod_ref[4:5, :]) + mod_ref[3:4, :]
                hn_ref[rows, :] = hn.astype(BF16)


def _ffn(x, h, mod, g_in, g_out, wg, wu, wd, *, idx, tm, tf, cast_next=None):
    t = x.shape[0]
    rows_per_mod = t // mod.shape[0]
    first = h is None
    ni, nj = t // tm, D_FF // tf
    row_block = pl.BlockSpec((tm, D_MODEL), lambda i, j: (i, 0))
    in_specs = [row_block,
                pl.BlockSpec((None, N_MOD, D_MODEL), lambda i, j: ((i * tm) // rows_per_mod, 0, 0)),
                pl.BlockSpec(g_in.shape, lambda i, j: (0, 0)),
                pl.BlockSpec((1, D_MODEL), lambda i, j: (0, 0)),
                pl.BlockSpec((D_MODEL, tf), lambda i, j: (0, j)),
                pl.BlockSpec((D_MODEL, tf), lambda i, j: (0, j)),
                pl.BlockSpec((tf, D_MODEL), lambda i, j: (j, 0))]
    args = [x, mod, g_in, g_out, wg, wu, wd]
    if first:
        out_specs = [row_block, row_block]
        out_shape = [jax.ShapeDtypeStruct((t, D_MODEL), F32), jax.ShapeDtypeStruct((t, D_MODEL), BF16)]
        scratch = [pltpu.VMEM((tm, D_MODEL), BF16)]
    else:
        in_specs.append(row_block)
        args.append(h)
        out_specs = [row_block]
        out_shape = [jax.ShapeDtypeStruct((t, D_MODEL), F32)]
        scratch = []
    if cast_next is not None:
        k = cast_next[3]
        dr = D_MODEL // ni
        in_specs += [pl.BlockSpec((None, dr, tf), lambda i, j: (k, i, j)),
                     pl.BlockSpec((None, dr, tf), lambda i, j: (k, i, j)),
                     pl.BlockSpec((None, tf, dr), lambda i, j: (k, j, i))]
        in_specs += [pl.BlockSpec((W_IN_COLS, dr), lambda i, j: (0, i)),
                     pl.BlockSpec((dr, D_MODEL), lambda i, j: (i, 0))]
        args += list(cast_next[:3]) + list(cast_next[4:6])
        out_specs += [pl.BlockSpec((dr, tf), lambda i, j: (i, j)),
                      pl.BlockSpec((dr, tf), lambda i, j: (i, j)),
                      pl.BlockSpec((tf, dr), lambda i, j: (j, i)),
                      pl.BlockSpec((MAIN_W, dr), lambda i, j: (0, i)),
                      pl.BlockSpec((N_BLK * SMALL_W, dr), lambda i, j: (0, i)),
                      pl.BlockSpec((dr, D_MODEL), lambda i, j: (i, 0))]
        out_shape += [jax.ShapeDtypeStruct((D_MODEL, D_FF), BF16), jax.ShapeDtypeStruct((D_MODEL, D_FF), BF16),
                      jax.ShapeDtypeStruct((D_FF, D_MODEL), BF16),
                      jax.ShapeDtypeStruct((MAIN_W, D_MODEL), BF16),
                      jax.ShapeDtypeStruct((N_BLK * SMALL_W, D_MODEL), BF16),
                      jax.ShapeDtypeStruct((ML_W + SSM_W, D_MODEL), BF16)]
    out = pl.pallas_call(
        functools.partial(_ffn_kernel, mod_base=6 * idx, first=first, side_cast=cast_next is not None),
        grid=(ni, nj),
        in_specs=in_specs,
        out_specs=out_specs,
        out_shape=out_shape,
        scratch_shapes=scratch,
        compiler_params=_cparams(2, VMEM_LIMIT if cast_next is None else BIG_VMEM_LIMIT),
        name="ffn",
    )(*args)
    return out if len(out) > 1 else out[0]


def _conv_silu(u, cw_ref, cb_ref, cols, seg_len):
    rows = u.shape[0]
    pos = lax.broadcasted_iota(jnp.int32, u.shape, 0) % seg_len
    acc = jnp.zeros_like(u) + cb_ref[:, cols]
    for j in range(CONV_K):
        off = j - CONV_K // 2
        shifted = u if off == 0 else pltpu.roll(u, (-off) % rows, 0)
        valid = jnp.logical_and(pos + off >= 0, pos + off < seg_len)
        acc = acc + jnp.where(valid, shifted, 0.0) * cw_ref[j:j + 1, cols]
    return _silu(acc)


def _inproj_kernel(h_ref, w_ref, ws_ref, cw_ref, cb_ref, o_ref, os_ref, *, conv_col0, seg_len):
    j = pl.program_id(1)
    last = pl.num_programs(1) - 1

    @pl.when(j != last)
    def _():
        o_ref[...] = _dot_nt(h_ref[...], w_ref[...]).astype(BF16)

    @pl.when(j == last)
    def _():
        os_ref[...] = _dot_nt(h_ref[...], ws_ref[...])
        for r in range(h_ref.shape[0] // CONV_ROWS):
            rows = slice(r * CONV_ROWS, (r + 1) * CONV_ROWS)
            acc = _dot_nt(h_ref[rows, :], w_ref[...])
            o_ref[rows, :conv_col0] = acc[:, :conv_col0].astype(BF16)
            for ct in range(CONV_CH // CONV_TILE):
                cols = slice(ct * CONV_TILE, (ct + 1) * CONV_TILE)
                u = acc[:, conv_col0 + ct * CONV_TILE:conv_col0 + (ct + 1) * CONV_TILE]
                o_ref[rows, conv_col0 + ct * CONV_TILE:conv_col0 + (ct + 1) * CONV_TILE] = (
                    _conv_silu(u, cw_ref, cb_ref, cols, seg_len).astype(BF16))


def _inproj(h, w_main, w_small, conv_w, conv_b, *, seg_len, tm, tn):
    t = h.shape[0]
    n_tiles = MAIN_W // tn
    conv_col0 = MAIN_W - CONV_CH - (n_tiles - 1) * tn
    assert n_tiles * tn == MAIN_W and conv_col0 >= 0 and conv_col0 % 128 == 0
    assert tm % CONV_ROWS == 0 and CONV_ROWS % seg_len == 0
    return pl.pallas_call(
        functools.partial(_inproj_kernel, conv_col0=conv_col0, seg_len=seg_len),
        grid=(t // tm, n_tiles),
        in_specs=[pl.BlockSpec((tm, D_MODEL), lambda i, j: (i, 0)),
                  pl.BlockSpec((tn, D_MODEL), lambda i, j: (j, 0)),
                  pl.BlockSpec((N_BLK * SMALL_W, D_MODEL), lambda i, j: (0, 0)),
                  pl.BlockSpec((CONV_K, CONV_CH), lambda i, j: (0, 0)),
                  pl.BlockSpec((1, CONV_CH), lambda i, j: (0, 0))],
        out_specs=[pl.BlockSpec((tm, tn), lambda i, j: (i, j)),
                   pl.BlockSpec((tm, N_BLK * SMALL_W), lambda i, j: (i, 0))],
        out_shape=[jax.ShapeDtypeStruct((t, MAIN_W), BF16),
                   jax.ShapeDtypeStruct((t, N_BLK * SMALL_W), F32)],
        compiler_params=_cparams(2, BIG_VMEM_LIMIT),
        name="inproj",
    )(h, w_main, w_small, conv_w, conv_b)


def _tri_mask(direction):
    row = lax.broadcasted_iota(jnp.int32, (CHUNK, CHUNK), 0)
    col = lax.broadcasted_iota(jnp.int32, (CHUNK, CHUNK), 1)
    return (col <= row) if direction == 0 else (col >= row)


def _scan_specs(nc, width, col):
    return [pl.BlockSpec((CHUNK, width), lambda b, c: (b * nc + c, col)),
            pl.BlockSpec((CHUNK, width), lambda b, c: (b * nc + nc - 1 - c, col))]


def _scan_max(x, d):
    idx = lax.broadcasted_iota(jnp.int32, x.shape, 0)
    k = 1
    while k < CHUNK:
        if d == 0:
            shifted, valid = pltpu.roll(x, k, 0), idx >= k
        else:
            shifted, valid = pltpu.roll(x, CHUNK - k, 0), idx < CHUNK - k
        x = jnp.maximum(x, jnp.where(valid, shifted, -jnp.inf))
        k *= 2
    return x


def _lane_bcast(x, lane):
    return jnp.broadcast_to(x[:, lane:lane + 1], (x.shape[0], SMALL_W))


ML_AUG = ML_DH + SMALL_W
M_ROWS = 8


def _mlstm_chain(q_ref, k_ref, v_ref, ga_ref, gf_ref, bi_ref, bf_ref, cn_s, m_s, d, fresh):
    tri = _tri_mask(d)
    scale = ML_DH ** -0.5
    log_i = ga_ref[...] + bi_ref[...]
    log_f = _log_sigmoid(gf_ref[...] + bf_ref[...])
    b = _scan_sum(tri, log_f)
    g = log_i - b
    m_prev = jnp.zeros((1, SMALL_W), F32) if fresh else m_s[d:d + 1, :]
    m_inter = b + m_prev
    m_t = jnp.maximum(m_inter, b + _scan_max(g, d))
    c1 = b - m_t
    inter = jnp.exp(m_inter - m_t)
    e_negm = jnp.exp(-m_t)
    last = CHUNK - 1 if d == 0 else 0
    b_last = b[last:last + 1, :]
    log_w = b_last - b + log_i
    m_end = b_last + m_prev
    m_new = jnp.maximum(m_end, jnp.max(log_w, axis=0, keepdims=True))
    w = jnp.exp(log_w - m_new) * scale
    decay = jnp.exp(m_end - m_new)
    g_t = g.T
    ones = jnp.ones((CHUNK, SMALL_W), BF16)

    hs, new_cn = [], []
    for h in range(ML_HEADS):
        lane = d * ML_HEADS + h
        log_d = jnp.where(tri, _lane_bcast(c1, lane) + g_t[lane:lane + 1, :], -jnp.inf)
        dmat = jnp.exp(log_d) * scale
        sl = slice(h * ML_DH, (h + 1) * ML_DH)
        qh = q_ref[:, sl]
        kh = k_ref[:, sl]
        v_aug = jnp.concatenate([v_ref[:, sl], ones], axis=1)
        scores = (_dot_nt(qh, kh) * dmat).astype(BF16)
        r = _dot(scores, v_aug)
        if not fresh:
            cn_old = cn_s[:, lane * ML_AUG:(lane + 1) * ML_AUG]
            inter_rep = _lane_bcast(inter, lane)
            r = r + jnp.concatenate([inter_rep] * (ML_AUG // SMALL_W), axis=1) * _dot(qh, cn_old.astype(BF16))
        rden = 1.0 / jnp.maximum(jnp.abs(r[:, ML_DH:]), _lane_bcast(e_negm, lane))
        hs.append(r[:, :ML_DH] * jnp.concatenate([rden] * (ML_DH // SMALL_W), axis=1))

        w_rep = _lane_bcast(w, lane)
        kw = (kh.astype(F32) * jnp.concatenate([w_rep] * (ML_DH // SMALL_W), axis=1)).astype(BF16)
        upd = _dot_tn(kw, v_aug)
        new_cn.append(upd if fresh else decay[:, lane:lane + 1] * cn_old + upd)
    return jnp.concatenate(hs, axis=1), new_cn, m_new


def _mlstm_finish(hsum, og_ref, mlg_ref):
    outs = []
    for h in range(ML_HEADS):
        sl = slice(h * ML_DH, (h + 1) * ML_DH)
        hs = hsum[:, sl]
        cen = hs - jnp.mean(hs, axis=1, keepdims=True)
        var = jnp.mean(cen * cen, axis=1, keepdims=True)
        hn = cen * lax.rsqrt(var + EPS) * mlg_ref[:, sl]
        outs.append((jax.nn.sigmoid(og_ref[:, sl].astype(F32)) * hn).astype(BF16))
    return jnp.concatenate(outs, axis=1)


def _mlstm_kernel(*refs, zero_init, nc):
    refs = list(refs)
    qf, qb, kf, kb, vf, vb, of, ob, gaf, gab, gff, gfb, bi_ref, bf_ref, mlg_ref = refs[:15]
    pos = 15
    if not zero_init:
        c0_ref, n0_ref, m0_ref = refs[pos:pos + 3]
        pos += 3
    h_out_ref, c_out_ref, n_out_ref, m_out_ref, cn_s, m_s, hcur, stash = refs[pos:]

    c = pl.program_id(1)

    def scan_step(fresh):
        h_f, cn_f, m_f = _mlstm_chain(qf, kf, vf, gaf, gff, bi_ref, bf_ref, cn_s, m_s, 0, fresh)
        h_b, cn_b, m_b = _mlstm_chain(qb, kb, vb, gab, gfb, bi_ref, bf_ref, cn_s, m_s, 1, fresh)
        hcur[...] = jnp.concatenate([h_f, h_b], axis=1)
        cn_s[...] = jnp.concatenate(cn_f + cn_b, axis=1)
        m_s[...] = jnp.concatenate([m_f, m_b] + [jnp.zeros_like(m_f)] * (M_ROWS - 2), axis=0)

    if zero_init:
        pl.when(c == 0)(functools.partial(scan_step, True))
        pl.when(c > 0)(functools.partial(scan_step, False))
    else:
        @pl.when(c == 0)
        def _():
            for r in range(2 * ML_HEADS):
                d, h = divmod(r, ML_HEADS)
                cn_s[:, r * ML_AUG:r * ML_AUG + ML_DH] = c0_ref[d, h]
                n_rows = jnp.broadcast_to(n0_ref[d, h:h + 1, :], (SMALL_W, ML_DH))
                cn_s[:, r * ML_AUG + ML_DH:(r + 1) * ML_AUG] = n_rows.T
            m_s[...] = m0_ref[...]

        scan_step(False)

    half = nc // 2

    @pl.when(c < half)
    def _():
        stash[c] = hcur[...]

    @pl.when(c >= half)
    def _():
        s = nc - 1 - c
        h_out_ref[c] = _mlstm_finish(hcur[:, :ML_W] + stash[s, :, ML_W:], of, mlg_ref)
        h_out_ref[s] = _mlstm_finish(stash[s, :, :ML_W] + hcur[:, ML_W:], ob, mlg_ref)

    @pl.when(c == nc - 1)
    def _():
        for r in range(2 * ML_HEADS):
            d, h = divmod(r, ML_HEADS)
            c_out_ref[d, h] = cn_s[:, r * ML_AUG:r * ML_AUG + ML_DH]
            n_out_ref[d, h:h + 1, :] = cn_s[:, r * ML_AUG + ML_DH:(r + 1) * ML_AUG].T[0:1, :]
        m_out_ref[...] = m_s[...]


def _mlstm(p_main, small, bi_row, bf_row, mlg, state, layer, *, bsz, seq):
    nc = seq // CHUNK
    assert nc % 2 == 0
    zero_init = state is None
    const = lambda b, c: (0, 0)
    in_specs = (_scan_specs(nc, ML_W, COL_Q) + _scan_specs(nc, ML_W, COL_K) + _scan_specs(nc, ML_W, COL_V)
                + _scan_specs(nc, ML_W, COL_O) + _scan_specs(nc, SMALL_W, BLK_IG) + _scan_specs(nc, SMALL_W, BLK_FG)
                + [pl.BlockSpec((1, SMALL_W), const), pl.BlockSpec((1, SMALL_W), const),
                   pl.BlockSpec((1, ML_W), const)])
    args = [p_main] * 8 + [small] * 4 + [bi_row, bf_row, mlg]
    if not zero_init:
        in_specs += [pl.BlockSpec((None, None, 2, ML_HEADS, ML_DH, ML_DH), lambda b, c: (b, layer, 0, 0, 0, 0)),
                     pl.BlockSpec((None, None, 2, ML_HEADS, ML_DH), lambda b, c: (b, layer, 0, 0, 0)),
                     pl.BlockSpec((None, M_ROWS, SMALL_W), lambda b, c: (b, 0, 0))]
        args += list(state)
    out_specs = [pl.BlockSpec((nc, CHUNK, ML_W), lambda b, c: (b, 0, 0)),
                 pl.BlockSpec((None, 2, ML_HEADS, ML_DH, ML_DH), lambda b, c: (b, 0, 0, 0, 0)),
                 pl.BlockSpec((None, 2, ML_HEADS, ML_DH), lambda b, c: (b, 0, 0, 0)),
                 pl.BlockSpec((None, M_ROWS, SMALL_W), lambda b, c: (b, 0, 0))]
    out_shape = [jax.ShapeDtypeStruct((bsz * nc, CHUNK, ML_W), BF16),
                 jax.ShapeDtypeStruct((bsz, 2, ML_HEADS, ML_DH, ML_DH), F32),
                 jax.ShapeDtypeStruct((bsz, 2, ML_HEADS, ML_DH), F32),
                 jax.ShapeDtypeStruct((bsz, M_ROWS, SMALL_W), F32)]
    return pl.pallas_call(
        functools.partial(_mlstm_kernel, zero_init=zero_init, nc=nc),
        grid=(bsz, nc),
        in_specs=in_specs,
        out_specs=out_specs,
        out_shape=out_shape,
        scratch_shapes=[pltpu.VMEM((ML_DH, 2 * ML_HEADS * ML_AUG), F32),
                        pltpu.VMEM((M_ROWS, SMALL_W), F32),
                        pltpu.VMEM((CHUNK, 2 * ML_W), F32),
                        pltpu.VMEM((nc // 2, CHUNK, 2 * ML_W), F32)],
        compiler_params=_cparams(2),
        name="mlstm",
    )(*args)


def _ssd_chain(xs_ref, bc_ref, sm_ref, gb_ref, arow_ref, st_s, d, fresh):
    ys, new_st = [], []
    tri = _tri_mask(d)
    p = _softplus(sm_ref[...] + gb_ref[...])
    cum = _scan_sum(tri, p * arow_ref[...])
    p_t = p.T
    cum_t = cum.T
    last = CHUNK - 1 if d == 0 else 0
    lane0 = d * SSM_HEADS
    a_last = cum[last:last + 1, :]

    sel_p = _lane_selector(lane0, SSM_HEADS, SSM_P)
    to_end_x = _select_dot(jnp.exp(a_last - cum) * p, sel_p, 2)
    if not fresh:
        ea_x = _select_dot(jnp.exp(cum), sel_p, 2)
        ea_last_x = _select_dot(jnp.broadcast_to(jnp.exp(a_last), (8, SMALL_W)), sel_p, 3)[0:1, :]

    lo = lax.broadcasted_iota(jnp.int32, (CHUNK, 2 * SSM_P), 1) < SSM_P
    zero_b = jnp.zeros((CHUNK, 2 * SSM_P), BF16)

    for gi in range(SSM_G):
        bg = bc_ref[:, gi * SSM_N:(gi + 1) * SSM_N]
        cg = bc_ref[:, (SSM_G + gi) * SSM_N:(SSM_G + gi + 1) * SSM_N]
        cb = _dot_nt(cg, bg)
        gcols = slice(gi * SSM_R * SSM_P, (gi + 1) * SSM_R * SSM_P)
        y_pairs = []
        for pr in range(SSM_R // 2):
            h0 = gi * SSM_R + 2 * pr
            lhs = []
            for hh in (h0, h0 + 1):
                lane = lane0 + hh
                seg = _lane_bcast(cum, lane) - cum_t[lane:lane + 1, :]
                decay = jnp.exp(jnp.where(tri, seg, -jnp.inf))
                lhs.append((cb * decay * p_t[lane:lane + 1, :]).astype(BF16))
            xp = xs_ref[:, h0 * SSM_P:(h0 + 2) * SSM_P]
            rhs = jnp.concatenate([jnp.where(lo, xp, zero_b), jnp.where(lo, zero_b, xp)], axis=0)
            y_pairs.append(_dot(jnp.concatenate(lhs, axis=1), rhs))
        y_g = jnp.concatenate(y_pairs, axis=1)
        xw = (xs_ref[:, gcols].astype(F32) * to_end_x[:, gcols]).astype(BF16)
        upd = _dot_tn(bg, xw)
        if not fresh:
            st_old = st_s[:, d * SSM_W + gi * SSM_R * SSM_P:d * SSM_W + (gi + 1) * SSM_R * SSM_P]
            y_g = y_g + ea_x[:, gcols] * _dot(cg, st_old.astype(BF16))
            upd = st_old * ea_last_x[:, gcols] + upd
        ys.append(y_g)
        new_st.append(upd)
    return jnp.concatenate(ys, axis=1), jnp.concatenate(new_st, axis=1)


def _ssd_finish(ysum, xs_ref, z_ref, dsk_ref, ng_ref):
    y = ysum + dsk_ref[...] * xs_ref[...].astype(F32)
    yz = y * _silu(z_ref[...].astype(F32))
    return (yz * _rms(yz) * ng_ref[...]).astype(BF16)


def _ssd_kernel(*refs, zero_init, nc):
    refs = list(refs)
    xf, xb, bcf, bcb, zf, zb, smf, smb, gb_ref, arow_ref, dsk_ref, ng_ref = refs[:12]
    pos = 12
    if not zero_init:
        s0_ref = refs[pos]
        pos += 1
    y_out_ref, s_out_ref, st_s, ycur, stash = refs[pos:]

    c = pl.program_id(1)

    def scan_step(fresh):
        y_f, st_f = _ssd_chain(xf, bcf, smf, gb_ref, arow_ref, st_s, 0, fresh)
        y_b, st_b = _ssd_chain(xb, bcb, smb, gb_ref, arow_ref, st_s, 1, fresh)
        ycur[...] = jnp.concatenate([y_f, y_b], axis=1)
        st_s[...] = jnp.concatenate([st_f, st_b], axis=1)

    if zero_init:
        pl.when(c == 0)(functools.partial(scan_step, True))
        pl.when(c > 0)(functools.partial(scan_step, False))
    else:
        @pl.when(c == 0)
        def _():
            st_s[:, :SSM_W] = s0_ref[0].T
            st_s[:, SSM_W:] = s0_ref[1].T

        scan_step(False)

    half = nc // 2

    @pl.when(c < half)
    def _():
        stash[c] = ycur[...]

    @pl.when(c >= half)
    def _():
        s = nc - 1 - c
        y_out_ref[c] = _ssd_finish(ycur[:, :SSM_W] + stash[s, :, SSM_W:], xf, zf, dsk_ref, ng_ref)
        y_out_ref[s] = _ssd_finish(stash[s, :, :SSM_W] + ycur[:, SSM_W:], xb, zb, dsk_ref, ng_ref)

    @pl.when(c == nc - 1)
    def _():
        s_out_ref[0] = st_s[:, :SSM_W].T
        s_out_ref[1] = st_s[:, SSM_W:].T


def _ssd(p_main, small, gb_row, a_row, dsk, ng, state, layer, *, bsz, seq):
    nc = seq // CHUNK
    assert nc % 2 == 0
    zero_init = state is None
    const = lambda b, c: (0, 0)
    in_specs = (_scan_specs(nc, SSM_W, COL_XS) + _scan_specs(nc, BC_W, COL_BC) + _scan_specs(nc, SSM_W, COL_Z)
                + _scan_specs(nc, SMALL_W, BLK_DT)
                + [pl.BlockSpec((1, SMALL_W), const), pl.BlockSpec((1, SMALL_W), const),
                   pl.BlockSpec((1, SSM_W), const), pl.BlockSpec((1, SSM_W), const)])
    args = [p_main] * 6 + [small, small, gb_row, a_row, dsk, ng]
    if not zero_init:
        in_specs.append(pl.BlockSpec((None, None, 2, SSM_W, SSM_N), lambda b, c: (b, layer, 0, 0, 0)))
        args.append(state)
    return pl.pallas_call(
        functools.partial(_ssd_kernel, zero_init=zero_init, nc=nc),
        grid=(bsz, nc),
        in_specs=in_specs,
        out_specs=[pl.BlockSpec((nc, CHUNK, SSM_W), lambda b, c: (b, 0, 0)),
                   pl.BlockSpec((None, 2, SSM_W, SSM_N), lambda b, c: (b, 0, 0, 0))],
        out_shape=[jax.ShapeDtypeStruct((bsz * nc, CHUNK, SSM_W), BF16),
                   jax.ShapeDtypeStruct((bsz, 2, SSM_W, SSM_N), F32)],
        scratch_shapes=[pltpu.VMEM((SSM_N, 2 * SSM_W), F32),
                        pltpu.VMEM((CHUNK, 2 * SSM_W), F32),
                        pltpu.VMEM((nc // 2, CHUNK, 2 * SSM_W), F32)],
        compiler_params=_cparams(2),
        name="ssd",
    )(*args)


def _outproj_kernel(hml_ref, y_ref, x_ref, mod_ref, g_ref, gn_ref, w_ref, o_ref, hn_ref):
    gate = mod_ref[5:6, :]
    shift = mod_ref[6:7, :]
    scale1 = 1.0 + mod_ref[7:8, :]
    for r in range(x_ref.shape[0] // ROW_GROUP):
        rows = slice(r * ROW_GROUP, (r + 1) * ROW_GROUP)
        mix = _dot(hml_ref[rows, :], w_ref[0:ML_W, :]) + _dot(y_ref[rows, :], w_ref[ML_W:ML_W + SSM_W, :])
        o = x_ref[rows, :] + gate * ((mix * _rms(mix)) * g_ref[...])
        o_ref[rows, :] = o
        hn_ref[rows, :] = ((o * _rms(o)) * gn_ref[...] * scale1 + shift).astype(BF16)


def _outproj(hml, y, x, mod, g, g_next, w, *, tm):
    t = x.shape[0]
    rows_per_mod = t // mod.shape[0]
    row_block = pl.BlockSpec((tm, D_MODEL), lambda i: (i, 0))
    return pl.pallas_call(
        _outproj_kernel,
        grid=(t // tm,),
        in_specs=[pl.BlockSpec((tm, ML_W), lambda i: (i, 0)),
                  pl.BlockSpec((tm, SSM_W), lambda i: (i, 0)),
                  row_block,
                  pl.BlockSpec((None, N_MOD, D_MODEL), lambda i: ((i * tm) // rows_per_mod, 0, 0)),
                  pl.BlockSpec((1, D_MODEL), lambda i: (0, 0)),
                  pl.BlockSpec((1, D_MODEL), lambda i: (0, 0)),
                  pl.BlockSpec((ML_W + SSM_W, D_MODEL), lambda i: (0, 0))],
        out_specs=[row_block, row_block],
        out_shape=[jax.ShapeDtypeStruct((t, D_MODEL), F32), jax.ShapeDtypeStruct((t, D_MODEL), BF16)],
        compiler_params=_cparams(1),
        name="outproj",
    )(hml, y, x, mod, g, g_next, w)


def _prepare_params(norm_g, w_in, gate_bias, dt_bias, a_log, d_skip, conv_w, conv_b, ml_norm_g, ssm_norm_g,
                    w_out, w_gate, w_up, w_down):
    return dict(
        g=[norm_g[i].reshape(1, D_MODEL) for i in range(6)],
        g_first=jnp.stack([norm_g[0], norm_g[2]], axis=0),
        w_in_t=jnp.swapaxes(w_in, 0, 1),
        bi_row=_lane_block(gate_bias[0].reshape(1, -1)), bf_row=_lane_block(gate_bias[1].reshape(1, -1)),
        bd_row=_lane_block(dt_bias.reshape(1, -1)), a_row=_lane_block(-jnp.exp(a_log.reshape(1, -1))),
        dsk=jnp.repeat(d_skip, SSM_P).reshape(1, SSM_W),
        conv_w=conv_w, conv_b=conv_b.reshape(1, CONV_CH),
        mlg=ml_norm_g.reshape(1, ML_W), ng=ssm_norm_g.reshape(1, SSM_W),
        w_out=w_out,
        ffn1_w=(_cast_bf16(w_gate, 0), _cast_bf16(w_up, 0), _cast_bf16(w_down, 0)),
        ffn_w_f32=(w_gate, w_up, w_down),
    )


def _trunk_path(x, mod, p, ml_state, ssm_state, layer, seg_len, shared_w=None, *,
                tm_ffn=512, tf=512, tm_proj=512, tn_proj=3328):
    bsz, seq, _ = x.shape
    t = bsz * seq
    x = x.reshape(t, D_MODEL)
    cast_next = None if shared_w is not None else p['ffn_w_f32'] + (1, p['w_in_t'], p['w_out'])
    res = _ffn(x, None, mod, p['g_first'], p['g'][1], *p['ffn1_w'], idx=0, tm=tm_ffn, tf=tf, cast_next=cast_next)
    x, h = res[:2]
    if shared_w is None:
        shared_w = dict(ffn2=tuple(res[2:5]), w_main=res[5], w_small=res[6], w_out=res[7])
    p_main, small = _inproj(h, shared_w['w_main'], shared_w['w_small'], p['conv_w'], p['conv_b'],
                            seg_len=seg_len, tm=tm_proj, tn=tn_proj)
    if ml_state is not None:
        c0, n0, m0 = ml_state
        m0 = m0[:, layer]
        pad = SMALL_W - 2 * ML_HEADS
        m0 = jnp.stack([jnp.pad(m0[:, 0], ((0, 0), (0, pad + ML_HEADS))),
                        jnp.pad(m0[:, 1], ((0, 0), (ML_HEADS, pad)))], axis=1)
        m0 = jnp.pad(m0, ((0, 0), (0, M_ROWS - 2), (0, 0)))
        ml_state = (c0, n0, m0)
        ssm_state = ssm_state.reshape(ssm_state.shape[:3] + (SSM_W, SSM_N))
    hml, new_c, new_n, new_m = _mlstm(p_main, small, p['bi_row'], p['bf_row'], p['mlg'], ml_state, layer,
                                      bsz=bsz, seq=seq)
    y, new_s = _ssd(p_main, small, p['bd_row'], p['a_row'], p['dsk'], p['ng'], ssm_state, layer, bsz=bsz, seq=seq)
    x, h = _outproj(hml.reshape(t, ML_W), y.reshape(t, SSM_W), x, mod, p['g'][3], p['g'][4], shared_w['w_out'],
                    tm=tm_proj)
    x = _ffn2_pair(x, h, mod, p['g'][5], *shared_w['ffn2'], tm=tm_ffn, tf=tf)
    new_m = new_m[:, :2, :2 * ML_HEADS].reshape(bsz, 2, 2, ML_HEADS)
    new_m = jnp.sum(jnp.where(jnp.eye(2, dtype=bool)[None, :, :, None], new_m, 0.0), axis=2)
    states = (new_c, new_n, new_m, new_s.reshape(bsz, 2, SSM_HEADS, SSM_P, SSM_N))
    return x.reshape(bsz, seq, D_MODEL), states, shared_w


def _stack_layers(parts):
    return parts[0][:, None] if len(parts) == 1 else jnp.stack(parts, axis=1)


def kernel(x_prompt, x_sample, state_mlstm_C, state_mlstm_n, state_mlstm_m, state_ssm, c, c_ctx, w_ada, b_ada,
           norm_g, w_in, gate_bias, dt_bias, a_log, d_skip, conv_w, conv_b, ml_norm_g, ssm_norm_g, w_out,
           ffn_w_gate, ffn_w_up, ffn_w_down):
    depth = w_in.shape[0]
    bd = x_sample.shape[0]
    y_p, y_s = x_prompt, x_sample
    new_states = []
    cvec = jnp.concatenate([c_ctx[None], c, jnp.zeros((8 - 1 - bd, D_MODEL), F32)], axis=0)
    for l in range(depth):
        p = _prepare_params(norm_g[l], w_in[l], gate_bias[l], dt_bias[l], a_log[l], d_skip[l], conv_w[l], conv_b[l],
                            ml_norm_g[l], ssm_norm_g[l], w_out[l], ffn_w_gate[l], ffn_w_up[l], ffn_w_down[l])
        mod = _ada(cvec, w_ada[l], b_ada[l].reshape(1, -1)).reshape(8, N_MOD, D_MODEL)
        y_p, states, shared_w = _trunk_path(y_p, mod[0:1], p, None, None, l, x_prompt.shape[1])
        lat_state = (state_mlstm_C, state_mlstm_n, state_mlstm_m)
        y_s, _, _ = _trunk_path(y_s, mod[1:1 + bd], p, lat_state, state_ssm, l, GRID_W, shared_w)
        new_states.append(states)
    return (y_p, y_s) + tuple(_stack_layers([st[k] for st in new_states]) for k in range(4))
```

```python
import functools

import jax
import jax.numpy as jnp
from jax import lax
from jax.experimental import pallas as pl
from jax.experimental.pallas import tpu as pltpu

F32 = jnp.float32
BF16 = jnp.bfloat16

D_MODEL = 2048
GRID_W = 64
ML_HEADS = 4
ML_W = 1024
ML_DH = ML_W // ML_HEADS
SSM_W = 1024
SSM_P = 64
SSM_HEADS = SSM_W // SSM_P
SSM_N = 128
SSM_G = 2
SSM_R = SSM_HEADS // SSM_G
CONV_K = 5
CONV_CH = SSM_W + 2 * SSM_G * SSM_N
BC_W = 2 * SSM_G * SSM_N
D_FF = 5632
CHUNK = 128
N_MOD = 9
EPS = 1e-6

MAIN_W = 4 * ML_W + SSM_W + CONV_CH
COL_Q, COL_K, COL_V, COL_O, COL_Z, COL_XS = 0, 1, 2, 3, 4, 5
COL_BC = (MAIN_W - BC_W) // BC_W
SMALL_W = 128
BLK_IG, BLK_FG, BLK_DT = 0, 1, 2
N_BLK = 3

VMEM_LIMIT = 48 * 1024 * 1024
BIG_VMEM_LIMIT = 60 * 1024 * 1024
ROW_GROUP = 256
CONV_ROWS = 256
CONV_TILE = 512


def _cparams(n_axes, vmem=VMEM_LIMIT):
    return pltpu.CompilerParams(dimension_semantics=("arbitrary",) * n_axes, vmem_limit_bytes=vmem)


def _dot(a, b):
    return jnp.dot(a, b, preferred_element_type=F32)


def _dot_nt(a, b):
    return lax.dot_general(a, b, (((1,), (1,)), ((), ())), preferred_element_type=F32)


def _dot_tn(a, b):
    return lax.dot_general(a, b, (((0,), (0,)), ((), ())), preferred_element_type=F32)


def _split_bf16(x, terms):
    parts, rest = [], x
    for i in range(terms):
        piece = rest.astype(BF16)
        parts.append(piece)
        if i + 1 < terms:
            rest = rest - piece.astype(F32)
    return parts


def _select_dot(x, sel, terms):
    return _dot(jnp.concatenate(_split_bf16(x, terms), axis=1), jnp.concatenate([sel] * terms, axis=0))


def _scan_sum(tri, x):
    tri_b = jnp.where(tri, 1.0, 0.0).astype(BF16)
    return _dot(jnp.concatenate([tri_b] * 3, axis=1), jnp.concatenate(_split_bf16(x, 3), axis=0))


def _lane_selector(first_lane, n_blocks, width):
    row = lax.broadcasted_iota(jnp.int32, (SMALL_W, n_blocks * width), 0)
    col = lax.broadcasted_iota(jnp.int32, (SMALL_W, n_blocks * width), 1)
    return jnp.where(row == first_lane + col // width, 1.0, 0.0).astype(BF16)


def _silu(x):
    return x * jax.nn.sigmoid(x)


def _softplus(x):
    return jnp.maximum(x, 0.0) + jnp.log(1.0 + jnp.exp(-jnp.abs(x)))


def _log_sigmoid(x):
    return jnp.minimum(x, 0.0) - jnp.log(1.0 + jnp.exp(-jnp.abs(x)))


def _rms(x):
    return lax.rsqrt(jnp.mean(x * x, axis=-1, keepdims=True) + EPS)


def _cast_kernel(x_ref, o_ref):
    o_ref[...] = x_ref[...].astype(BF16)


def _cast_bf16(w, idx=None):
    rows, cols = w.shape[-2:]
    first = 0 if idx is None else idx
    w2 = w.reshape(-1, cols)
    tr = max(8, min(rows, (8 * 1024 * 1024) // (4 * cols) // 256 * 256))
    while rows % tr:
        tr //= 2
    n_blocks = rows // tr
    return pl.pallas_call(
        _cast_kernel,
        grid=(n_blocks,),
        in_specs=[pl.BlockSpec((tr, cols), lambda i: (first * n_blocks + i, 0))],
        out_specs=pl.BlockSpec((tr, cols), lambda i: (i, 0)),
        out_shape=jax.ShapeDtypeStruct((rows, cols), BF16),
        compiler_params=_cparams(1),
        name="cast_bf16",
    )(w2)


def _lane_block(cols):
    return jnp.concatenate([cols, jnp.zeros(cols.shape[:-1] + (SMALL_W - cols.shape[-1],), cols.dtype)], axis=-1)


W_IN_GATES = 4 * ML_W
W_IN_Z = W_IN_GATES + 4 * ML_HEADS
W_IN_DT = W_IN_Z + SSM_W + CONV_CH
W_IN_COLS = W_IN_DT + 2 * SSM_HEADS


def _split_w_in_rows(w_ref, om_ref, os_ref):
    om_ref[:W_IN_GATES, :] = w_ref[:W_IN_GATES, :].astype(BF16)
    om_ref[W_IN_GATES:, :] = w_ref[W_IN_Z:W_IN_DT, :].astype(BF16)
    n_ig = 2 * ML_HEADS

    def row_block(rows):
        return jnp.concatenate([rows, jnp.zeros((SMALL_W - rows.shape[0], rows.shape[1]), F32)], axis=0)

    os_ref[...] = jnp.concatenate([row_block(w_ref[W_IN_GATES:W_IN_GATES + n_ig, :]),
                                   row_block(w_ref[W_IN_GATES + n_ig:W_IN_Z, :]),
                                   row_block(w_ref[W_IN_DT:, :])], axis=0).astype(BF16)


def _ada_kernel(c_ref, w_ref, b_ref, o_ref):
    s = _silu(c_ref[...]).astype(BF16)
    o_ref[...] = _dot(s, w_ref[...].astype(BF16)) + b_ref[...]


def _ada(cv, w, b):
    n = w.shape[1]
    tn = 1024
    return pl.pallas_call(
        _ada_kernel,
        grid=(n // tn,),
        in_specs=[pl.BlockSpec((cv.shape[0], D_MODEL), lambda j: (0, 0)),
                  pl.BlockSpec((D_MODEL, tn), lambda j: (0, j)),
                  pl.BlockSpec((1, tn), lambda j: (0, j))],
        out_specs=pl.BlockSpec((cv.shape[0], tn), lambda j: (0, j)),
        out_shape=jax.ShapeDtypeStruct((cv.shape[0], n), F32),
        compiler_params=_cparams(1),
        name="ada_mod",
    )(cv, w, b)


def _ffn_kernel(x_ref, mod_ref, gin_ref, gout_ref, wg_ref, wu_ref, wd_ref, *rest, mod_base, first, side_cast):
    rest = list(rest)
    if side_cast:
        assert first
        src, (w_in_ref, w_out_ref), dst, (w_main_ref, w_small_ref, w_out_b_ref) = (
            rest[:3], rest[3:5], rest[7:10], rest[10:13])
        rest = rest[5:7] + rest[13:]
        for s_ref, d_ref in zip(src, dst):
            d_ref[...] = s_ref[...].astype(BF16)

        @pl.when(pl.program_id(1) == 0)
        def _():
            _split_w_in_rows(w_in_ref, w_main_ref, w_small_ref)
            w_out_b_ref[...] = w_out_ref[...].astype(BF16)
    if first:
        o_ref, hn_ref, h_ref = rest
    else:
        h_ref, o_ref = rest
    j = pl.program_id(1)
    last = pl.num_programs(1) - 1
    groups = [slice(r * ROW_GROUP, (r + 1) * ROW_GROUP) for r in range(x_ref.shape[0] // ROW_GROUP)]

    def swiglu(h):
        a = (_silu(_dot(h, wg_ref[...])) * _dot(h, wu_ref[...])).astype(BF16)
        return _dot(a, wd_ref[...])

    @pl.when(j == 0)
    def _():
        if first:
            shift = mod_ref[mod_base:mod_base + 1, :]
            scale1 = 1.0 + mod_ref[mod_base + 1:mod_base + 2, :]
            for rows in groups:
                x = x_ref[rows, :]
                h = ((x * _rms(x)) * gin_ref[0:1, :] * scale1 + shift).astype(BF16)
                h_ref[rows, :] = h
                o_ref[rows, :] = swiglu(h)
        else:
            o_ref[...] = swiglu(h_ref[...])

    @pl.when(jnp.logical_and(j > 0, j < last))
    def _():
        o_ref[...] += swiglu(h_ref[...])

    @pl.when(j == last)
    def _():
        gate = 0.5 * mod_ref[mod_base + 2:mod_base + 3, :]
        for rows in groups:
            y = o_ref[rows, :] + swiglu(h_ref[rows, :])
            o = x_ref[rows, :] + gate * ((y * _rms(y)) * gout_ref[...])
            o_ref[rows, :] = o
            if first:
                hn = (o * _rms(o)) * gin_ref[1:2, :] * (1.0 + mod_ref[4:5, :]) + mod_ref[3:4, :]
                hn_ref[rows, :] = hn.astype(BF16)


def _ffn(x, h, mod, g_in, g_out, wg, wu, wd, *, idx, tm, tf, cast_next=None):
    t = x.shape[0]
    rows_per_mod = t // mod.shape[0]
    first = h is None
    ni, nj = t // tm, D_FF // tf
    row_block = pl.BlockSpec((tm, D_MODEL), lambda i, j: (i, 0))
    in_specs = [row_block,
                pl.BlockSpec((None, N_MOD, D_MODEL), lambda i, j: ((i * tm) // rows_per_mod, 0, 0)),
                pl.BlockSpec(g_in.shape, lambda i, j: (0, 0)),
                pl.BlockSpec((1, D_MODEL), lambda i, j: (0, 0)),
                pl.BlockSpec((D_MODEL, tf), lambda i, j: (0, j)),
                pl.BlockSpec((D_MODEL, tf), lambda i, j: (0, j)),
                pl.BlockSpec((tf, D_MODEL), lambda i, j: (j, 0))]
    args = [x, mod, g_in, g_out, wg, wu, wd]
    if first:
        out_specs = [row_block, row_block]
        out_shape = [jax.ShapeDtypeStruct((t, D_MODEL), F32), jax.ShapeDtypeStruct((t, D_MODEL), BF16)]
        scratch = [pltpu.VMEM((tm, D_MODEL), BF16)]
    else:
        in_specs.append(row_block)
        args.append(h)
        out_specs = [row_block]
        out_shape = [jax.ShapeDtypeStruct((t, D_MODEL), F32)]
        scratch = []
    if cast_next is not None:
        k = cast_next[3]
        dr = D_MODEL // ni
        in_specs += [pl.BlockSpec((None, dr, tf), lambda i, j: (k, i, j)),
                     pl.BlockSpec((None, dr, tf), lambda i, j: (k, i, j)),
                     pl.BlockSpec((None, tf, dr), lambda i, j: (k, j, i))]
        in_specs += [pl.BlockSpec((W_IN_COLS, dr), lambda i, j: (0, i)),
                     pl.BlockSpec((dr, D_MODEL), lambda i, j: (i, 0))]
        args += list(cast_next[:3]) + list(cast_next[4:6])
        out_specs += [pl.BlockSpec((dr, tf), lambda i, j: (i, j)),
                      pl.BlockSpec((dr, tf), lambda i, j: (i, j)),
                      pl.BlockSpec((tf, dr), lambda i, j: (j, i)),
                      pl.BlockSpec((MAIN_W, dr), lambda i, j: (0, i)),
                      pl.BlockSpec((N_BLK * SMALL_W, dr), lambda i, j: (0, i)),
                      pl.BlockSpec((dr, D_MODEL), lambda i, j: (i, 0))]
        out_shape += [jax.ShapeDtypeStruct((D_MODEL, D_FF), BF16), jax.ShapeDtypeStruct((D_MODEL, D_FF), BF16),
                      jax.ShapeDtypeStruct((D_FF, D_MODEL), BF16),
                      jax.ShapeDtypeStruct((MAIN_W, D_MODEL), BF16),
                      jax.ShapeDtypeStruct((N_BLK * SMALL_W, D_MODEL), BF16),
                      jax.ShapeDtypeStruct((ML_W + SSM_W, D_MODEL), BF16)]
    out = pl.pallas_call(
        functools.partial(_ffn_kernel, mod_base=6 * idx, first=first, side_cast=cast_next is not None),
        grid=(ni, nj),
        in_specs=in_specs,
        out_specs=out_specs,
        out_shape=out_shape,
        scratch_shapes=scratch,
        compiler_params=_cparams(2, VMEM_LIMIT if cast_next is None else BIG_VMEM_LIMIT),
        name="ffn",
    )(*args)
    return out if len(out) > 1 else out[0]


def _conv_silu(u, cw_ref, cb_ref, cols, seg_len):
    rows = u.shape[0]
    pos = lax.broadcasted_iota(jnp.int32, u.shape, 0) % seg_len
    acc = jnp.zeros_like(u) + cb_ref[:, cols]
    for j in range(CONV_K):
        off = j - CONV_K // 2
        shifted = u if off == 0 else pltpu.roll(u, (-off) % rows, 0)
        valid = jnp.logical_and(pos + off >= 0, pos + off < seg_len)
        acc = acc + jnp.where(valid, shifted, 0.0) * cw_ref[j:j + 1, cols]
    return _silu(acc)


def _inproj_kernel(h_ref, w_ref, ws_ref, cw_ref, cb_ref, o_ref, os_ref, *, conv_col0, seg_len):
    j = pl.program_id(1)
    last = pl.num_programs(1) - 1

    @pl.when(j != last)
    def _():
        o_ref[...] = _dot_nt(h_ref[...], w_ref[...]).astype(BF16)

    @pl.when(j == last)
    def _():
        os_ref[...] = _dot_nt(h_ref[...], ws_ref[...])
        for r in range(h_ref.shape[0] // CONV_ROWS):
            rows = slice(r * CONV_ROWS, (r + 1) * CONV_ROWS)
            acc = _dot_nt(h_ref[rows, :], w_ref[...])
            o_ref[rows, :conv_col0] = acc[:, :conv_col0].astype(BF16)
            for ct in range(CONV_CH // CONV_TILE):
                cols = slice(ct * CONV_TILE, (ct + 1) * CONV_TILE)
                u = acc[:, conv_col0 + ct * CONV_TILE:conv_col0 + (ct + 1) * CONV_TILE]
                o_ref[rows, conv_col0 + ct * CONV_TILE:conv_col0 + (ct + 1) * CONV_TILE] = (
                    _conv_silu(u, cw_ref, cb_ref, cols, seg_len).astype(BF16))


def _inproj(h, w_main, w_small, conv_w, conv_b, *, seg_len, tm, tn):
    t = h.shape[0]
    n_tiles = MAIN_W // tn
    conv_col0 = MAIN_W - CONV_CH - (n_tiles - 1) * tn
    assert n_tiles * tn == MAIN_W and conv_col0 >= 0 and conv_col0 % 128 == 0
    assert tm % CONV_ROWS == 0 and CONV_ROWS % seg_len == 0
    return pl.pallas_call(
        functools.partial(_inproj_kernel, conv_col0=conv_col0, seg_len=seg_len),
        grid=(t // tm, n_tiles),
        in_specs=[pl.BlockSpec((tm, D_MODEL), lambda i, j: (i, 0)),
                  pl.BlockSpec((tn, D_MODEL), lambda i, j: (j, 0)),
                  pl.BlockSpec((N_BLK * SMALL_W, D_MODEL), lambda i, j: (0, 0)),
                  pl.BlockSpec((CONV_K, CONV_CH), lambda i, j: (0, 0)),
                  pl.BlockSpec((1, CONV_CH), lambda i, j: (0, 0))],
        out_specs=[pl.BlockSpec((tm, tn), lambda i, j: (i, j)),
                   pl.BlockSpec((tm, N_BLK * SMALL_W), lambda i, j: (i, 0))],
        out_shape=[jax.ShapeDtypeStruct((t, MAIN_W), BF16),
                   jax.ShapeDtypeStruct((t, N_BLK * SMALL_W), F32)],
        compiler_params=_cparams(2, BIG_VMEM_LIMIT),
        name="inproj",
    )(h, w_main, w_small, conv_w, conv_b)


def _tri_mask(direction):
    row = lax.broadcasted_iota(jnp.int32, (CHUNK, CHUNK), 0)
    col = lax.broadcasted_iota(jnp.int32, (CHUNK, CHUNK), 1)
    return (col <= row) if direction == 0 else (col >= row)


def _scan_specs(nc, width, col):
    return [pl.BlockSpec((CHUNK, width), lambda b, c: (b * nc + c, col)),
            pl.BlockSpec((CHUNK, width), lambda b, c: (b * nc + nc - 1 - c, col))]


def _scan_max(x, d):
    idx = lax.broadcasted_iota(jnp.int32, x.shape, 0)
    k = 1
    while k < CHUNK:
        if d == 0:
            shifted, valid = pltpu.roll(x, k, 0), idx >= k
        else:
            shifted, valid = pltpu.roll(x, CHUNK - k, 0), idx < CHUNK - k
        x = jnp.maximum(x, jnp.where(valid, shifted, -jnp.inf))
        k *= 2
    return x


def _lane_bcast(x, lane):
    return jnp.broadcast_to(x[:, lane:lane + 1], (x.shape[0], SMALL_W))


ML_AUG = ML_DH + SMALL_W
M_ROWS = 8


def _mlstm_chain(q_ref, k_ref, v_ref, ga_ref, gf_ref, bi_ref, bf_ref, cn_s, m_s, d, fresh):
    tri = _tri_mask(d)
    scale = ML_DH ** -0.5
    log_i = ga_ref[...] + bi_ref[...]
    log_f = _log_sigmoid(gf_ref[...] + bf_ref[...])
    b = _scan_sum(tri, log_f)
    g = log_i - b
    m_prev = jnp.zeros((1, SMALL_W), F32) if fresh else m_s[d:d + 1, :]
    m_inter = b + m_prev
    m_t = jnp.maximum(m_inter, b + _scan_max(g, d))
    c1 = b - m_t
    inter = jnp.exp(m_inter - m_t)
    e_negm = jnp.exp(-m_t)
    last = CHUNK - 1 if d == 0 else 0
    b_last = b[last:last + 1, :]
    log_w = b_last - b + log_i
    m_end = b_last + m_prev
    m_new = jnp.maximum(m_end, jnp.max(log_w, axis=0, keepdims=True))
    w = jnp.exp(log_w - m_new) * scale
    decay = jnp.exp(m_end - m_new)
    g_t = g.T
    ones = jnp.ones((CHUNK, SMALL_W), BF16)

    hs, new_cn = [], []
    for h in range(ML_HEADS):
        lane = d * ML_HEADS + h
        log_d = jnp.where(tri, _lane_bcast(c1, lane) + g_t[lane:lane + 1, :], -jnp.inf)
        dmat = jnp.exp(log_d) * scale
        sl = slice(h * ML_DH, (h + 1) * ML_DH)
        qh = q_ref[:, sl]
        kh = k_ref[:, sl]
        v_aug = jnp.concatenate([v_ref[:, sl], ones], axis=1)
        scores = (_dot_nt(qh, kh) * dmat).astype(BF16)
        r = _dot(scores, v_aug)
        if not fresh:
            cn_old = cn_s[:, lane * ML_AUG:(lane + 1) * ML_AUG]
            inter_rep = _lane_bcast(inter, lane)
            r = r + jnp.concatenate([inter_rep] * (ML_AUG // SMALL_W), axis=1) * _dot(qh, cn_old.astype(BF16))
        rden = 1.0 / jnp.maximum(jnp.abs(r[:, ML_DH:]), _lane_bcast(e_negm, lane))
        hs.append(r[:, :ML_DH] * jnp.concatenate([rden] * (ML_DH // SMALL_W), axis=1))

        w_rep = _lane_bcast(w, lane)
        kw = (kh.astype(F32) * jnp.concatenate([w_rep] * (ML_DH // SMALL_W), axis=1)).astype(BF16)
        upd = _dot_tn(kw, v_aug)
        new_cn.append(upd if fresh else decay[:, lane:lane + 1] * cn_old + upd)
    return jnp.concatenate(hs, axis=1), new_cn, m_new


def _mlstm_finish(hsum, og_ref, mlg_ref):
    outs = []
    for h in range(ML_HEADS):
        sl = slice(h * ML_DH, (h + 1) * ML_DH)
        hs = hsum[:, sl]
        cen = hs - jnp.mean(hs, axis=1, keepdims=True)
        var = jnp.mean(cen * cen, axis=1, keepdims=True)
        hn = cen * lax.rsqrt(var + EPS) * mlg_ref[:, sl]
        outs.append((jax.nn.sigmoid(og_ref[:, sl].astype(F32)) * hn).astype(BF16))
    return jnp.concatenate(outs, axis=1)


def _mlstm_kernel(*refs, zero_init, nc):
    refs = list(refs)
    qf, qb, kf, kb, vf, vb, of, ob, gaf, gab, gff, gfb, bi_ref, bf_ref, mlg_ref = refs[:15]
    pos = 15
    if not zero_init:
        c0_ref, n0_ref, m0_ref = refs[pos:pos + 3]
        pos += 3
    h_out_ref, c_out_ref, n_out_ref, m_out_ref, cn_s, m_s, hcur, stash = refs[pos:]

    c = pl.program_id(1)

    def scan_step(fresh):
        h_f, cn_f, m_f = _mlstm_chain(qf, kf, vf, gaf, gff, bi_ref, bf_ref, cn_s, m_s, 0, fresh)
        h_b, cn_b, m_b = _mlstm_chain(qb, kb, vb, gab, gfb, bi_ref, bf_ref, cn_s, m_s, 1, fresh)
        hcur[...] = jnp.concatenate([h_f, h_b], axis=1)
        cn_s[...] = jnp.concatenate(cn_f + cn_b, axis=1)
        m_s[...] = jnp.concatenate([m_f, m_b] + [jnp.zeros_like(m_f)] * (M_ROWS - 2), axis=0)

    if zero_init:
        pl.when(c == 0)(functools.partial(scan_step, True))
        pl.when(c > 0)(functools.partial(scan_step, False))
    else:
        @pl.when(c == 0)
        def _():
            for r in range(2 * ML_HEADS):
                d, h = divmod(r, ML_HEADS)
                cn_s[:, r * ML_AUG:r * ML_AUG + ML_DH] = c0_ref[d, h]
                n_rows = jnp.broadcast_to(n0_ref[d, h:h + 1, :], (SMALL_W, ML_DH))
                cn_s[:, r * ML_AUG + ML_DH:(r + 1) * ML_AUG] = n_rows.T
            m_s[...] = m0_ref[...]

        scan_step(False)

    half = nc // 2

    @pl.when(c < half)
    def _():
        stash[c] = hcur[...]

    @pl.when(c >= half)
    def _():
        s = nc - 1 - c
        h_out_ref[c] = _mlstm_finish(hcur[:, :ML_W] + stash[s, :, ML_W:], of, mlg_ref)
        h_out_ref[s] = _mlstm_finish(stash[s, :, :ML_W] + hcur[:, ML_W:], ob, mlg_ref)

    @pl.when(c == nc - 1)
    def _():
        for r in range(2 * ML_HEADS):
            d, h = divmod(r, ML_HEADS)
            c_out_ref[d, h] = cn_s[:, r * ML_AUG:r * ML_AUG + ML_DH]
            n_out_ref[d, h:h + 1, :] = cn_s[:, r * ML_AUG + ML_DH:(r + 1) * ML_AUG].T[0:1, :]
        m_out_ref[...] = m_s[...]


def _mlstm(p_main, small, bi_row, bf_row, mlg, state, layer, *, bsz, seq):
    nc = seq // CHUNK
    assert nc % 2 == 0
    zero_init = state is None
    const = lambda b, c: (0, 0)
    in_specs = (_scan_specs(nc, ML_W, COL_Q) + _scan_specs(nc, ML_W, COL_K) + _scan_specs(nc, ML_W, COL_V)
                + _scan_specs(nc, ML_W, COL_O) + _scan_specs(nc, SMALL_W, BLK_IG) + _scan_specs(nc, SMALL_W, BLK_FG)
                + [pl.BlockSpec((1, SMALL_W), const), pl.BlockSpec((1, SMALL_W), const),
                   pl.BlockSpec((1, ML_W), const)])
    args = [p_main] * 8 + [small] * 4 + [bi_row, bf_row, mlg]
    if not zero_init:
        in_specs += [pl.BlockSpec((None, None, 2, ML_HEADS, ML_DH, ML_DH), lambda b, c: (b, layer, 0, 0, 0, 0)),
                     pl.BlockSpec((None, None, 2, ML_HEADS, ML_DH), lambda b, c: (b, layer, 0, 0, 0)),
                     pl.BlockSpec((None, M_ROWS, SMALL_W), lambda b, c: (b, 0, 0))]
        args += list(state)
    out_specs = [pl.BlockSpec((nc, CHUNK, ML_W), lambda b, c: (b, 0, 0)),
                 pl.BlockSpec((None, 2, ML_HEADS, ML_DH, ML_DH), lambda b, c: (b, 0, 0, 0, 0)),
                 pl.BlockSpec((None, 2, ML_HEADS, ML_DH), lambda b, c: (b, 0, 0, 0)),
                 pl.BlockSpec((None, M_ROWS, SMALL_W), lambda b, c: (b, 0, 0))]
    out_shape = [jax.ShapeDtypeStruct((bsz * nc, CHUNK, ML_W), BF16),
                 jax.ShapeDtypeStruct((bsz, 2, ML_HEADS, ML_DH, ML_DH), F32),
                 jax.ShapeDtypeStruct((bsz, 2, ML_HEADS, ML_DH), F32),
                 jax.ShapeDtypeStruct((bsz, M_ROWS, SMALL_W), F32)]
    return pl.pallas_call(
        functools.partial(_mlstm_kernel, zero_init=zero_init, nc=nc),
        grid=(bsz, nc),
        in_specs=in_specs,
        out_specs=out_specs,
        out_shape=out_shape,
        scratch_shapes=[pltpu.VMEM((ML_DH, 2 * ML_HEADS * ML_AUG), F32),
                        pltpu.VMEM((M_ROWS, SMALL_W), F32),
                        pltpu.VMEM((CHUNK, 2 * ML_W), F32),
                        pltpu.VMEM((nc // 2, CHUNK, 2 * ML_W), F32)],
        compiler_params=_cparams(2),
        name="mlstm",
    )(*args)


def _ssd_chain(xs_ref, bc_ref, sm_ref, gb_ref, arow_ref, st_s, d, fresh):
    ys, new_st = [], []
    tri = _tri_mask(d)
    p = _softplus(sm_ref[...] + gb_ref[...])
    cum = _scan_sum(tri, p * arow_ref[...])
    p_t = p.T
    cum_t = cum.T
    last = CHUNK - 1 if d == 0 else 0
    lane0 = d * SSM_HEADS
    a_last = cum[last:last + 1, :]

    sel_p = _lane_selector(lane0, SSM_HEADS, SSM_P)
    to_end_x = _select_dot(jnp.exp(a_last - cum) * p, sel_p, 2)
    if not fresh:
        ea_x = _select_dot(jnp.exp(cum), sel_p, 2)
        ea_last_x = _select_dot(jnp.broadcast_to(jnp.exp(a_last), (8, SMALL_W)), sel_p, 3)[0:1, :]

    lo = lax.broadcasted_iota(jnp.int32, (CHUNK, 2 * SSM_P), 1) < SSM_P
    zero_b = jnp.zeros((CHUNK, 2 * SSM_P), BF16)

    for gi in range(SSM_G):
        bg = bc_ref[:, gi * SSM_N:(gi + 1) * SSM_N]
        cg = bc_ref[:, (SSM_G + gi) * SSM_N:(SSM_G + gi + 1) * SSM_N]
        cb = _dot_nt(cg, bg)
        gcols = slice(gi * SSM_R * SSM_P, (gi + 1) * SSM_R * SSM_P)
        y_pairs = []
        for pr in range(SSM_R // 2):
            h0 = gi * SSM_R + 2 * pr
            lhs = []
            for hh in (h0, h0 + 1):
                lane = lane0 + hh
                seg = _lane_bcast(cum, lane) - cum_t[lane:lane + 1, :]
                decay = jnp.exp(jnp.where(tri, seg, -jnp.inf))
                lhs.append((cb * decay * p_t[lane:lane + 1, :]).astype(BF16))
            xp = xs_ref[:, h0 * SSM_P:(h0 + 2) * SSM_P]
            rhs = jnp.concatenate([jnp.where(lo, xp, zero_b), jnp.where(lo, zero_b, xp)], axis=0)
            y_pairs.append(_dot(jnp.concatenate(lhs, axis=1), rhs))
        y_g = jnp.concatenate(y_pairs, axis=1)
        xw = (xs_ref[:, gcols].astype(F32) * to_end_x[:, gcols]).astype(BF16)
        upd = _dot_tn(bg, xw)
        if not fresh:
            st_old = st_s[:, d * SSM_W + gi * SSM_R * SSM_P:d * SSM_W + (gi + 1) * SSM_R * SSM_P]
            y_g = y_g + ea_x[:, gcols] * _dot(cg, st_old.astype(BF16))
            upd = st_old * ea_last_x[:, gcols] + upd
        ys.append(y_g)
        new_st.append(upd)
    return jnp.concatenate(ys, axis=1), jnp.concatenate(new_st, axis=1)


def _ssd_finish(ysum, xs_ref, z_ref, dsk_ref, ng_ref):
    y = ysum + dsk_ref[...] * xs_ref[...].astype(F32)
    yz = y * _silu(z_ref[...].astype(F32))
    return (yz * _rms(yz) * ng_ref[...]).astype(BF16)


def _ssd_kernel(*refs, zero_init, nc):
    refs = list(refs)
    xf, xb, bcf, bcb, zf, zb, smf, smb, gb_ref, arow_ref, dsk_ref, ng_ref = refs[:12]
    pos = 12
    if not zero_init:
        s0_ref = refs[pos]
        pos += 1
    y_out_ref, s_out_ref, st_s, ycur, stash = refs[pos:]

    c = pl.program_id(1)

    def scan_step(fresh):
        y_f, st_f = _ssd_chain(xf, bcf, smf, gb_ref, arow_ref, st_s, 0, fresh)
        y_b, st_b = _ssd_chain(xb, bcb, smb, gb_ref, arow_ref, st_s, 1, fresh)
        ycur[...] = jnp.concatenate([y_f, y_b], axis=1)
        st_s[...] = jnp.concatenate([st_f, st_b], axis=1)

    if zero_init:
        pl.when(c == 0)(functools.partial(scan_step, True))
        pl.when(c > 0)(functools.partial(scan_step, False))
    else:
        @pl.when(c == 0)
        def _():
            st_s[:, :SSM_W] = s0_ref[0].T
            st_s[:, SSM_W:] = s0_ref[1].T

        scan_step(False)

    half = nc // 2

    @pl.when(c < half)
    def _():
        stash[c] = ycur[...]

    @pl.when(c >= half)
    def _():
        s = nc - 1 - c
        y_out_ref[c] = _ssd_finish(ycur[:, :SSM_W] + stash[s, :, SSM_W:], xf, zf, dsk_ref, ng_ref)
        y_out_ref[s] = _ssd_finish(stash[s, :, :SSM_W] + ycur[:, SSM_W:], xb, zb, dsk_ref, ng_ref)

    @pl.when(c == nc - 1)
    def _():
        s_out_ref[0] = st_s[:, :SSM_W].T
        s_out_ref[1] = st_s[:, SSM_W:].T


def _ssd(p_main, small, gb_row, a_row, dsk, ng, state, layer, *, bsz, seq):
    nc = seq // CHUNK
    assert nc % 2 == 0
    zero_init = state is None
    const = lambda b, c: (0, 0)
    in_specs = (_scan_specs(nc, SSM_W, COL_XS) + _scan_specs(nc, BC_W, COL_BC) + _scan_specs(nc, SSM_W, COL_Z)
                + _scan_specs(nc, SMALL_W, BLK_DT)
                + [pl.BlockSpec((1, SMALL_W), const), pl.BlockSpec((1, SMALL_W), const),
                   pl.BlockSpec((1, SSM_W), const), pl.BlockSpec((1, SSM_W), const)])
    args = [p_main] * 6 + [small, small, gb_row, a_row, dsk, ng]
    if not zero_init:
        in_specs.append(pl.BlockSpec((None, None, 2, SSM_W, SSM_N), lambda b, c: (b, layer, 0, 0, 0)))
        args.append(state)
    return pl.pallas_call(
        functools.partial(_ssd_kernel, zero_init=zero_init, nc=nc),
        grid=(bsz, nc),
        in_specs=in_specs,
        out_specs=[pl.BlockSpec((nc, CHUNK, SSM_W), lambda b, c: (b, 0, 0)),
                   pl.BlockSpec((None, 2, SSM_W, SSM_N), lambda b, c: (b, 0, 0, 0))],
        out_shape=[jax.ShapeDtypeStruct((bsz * nc, CHUNK, SSM_W), BF16),
                   jax.ShapeDtypeStruct((bsz, 2, SSM_W, SSM_N), F32)],
        scratch_shapes=[pltpu.VMEM((SSM_N, 2 * SSM_W), F32),
                        pltpu.VMEM((CHUNK, 2 * SSM_W), F32),
                        pltpu.VMEM((nc // 2, CHUNK, 2 * SSM_W), F32)],
        compiler_params=_cparams(2),
        name="ssd",
    )(*args)


def _outproj_kernel(hml_ref, y_ref, x_ref, mod_ref, g_ref, gn_ref, w_ref, o_ref, hn_ref):
    gate = mod_ref[5:6, :]
    shift = mod_ref[6:7, :]
    scale1 = 1.0 + mod_ref[7:8, :]
    for r in range(x_ref.shape[0] // ROW_GROUP):
        rows = slice(r * ROW_GROUP, (r + 1) * ROW_GROUP)
        mix = _dot(hml_ref[rows, :], w_ref[0:ML_W, :]) + _dot(y_ref[rows, :], w_ref[ML_W:ML_W + SSM_W, :])
        o = x_ref[rows, :] + gate * ((mix * _rms(mix)) * g_ref[...])
        o_ref[rows, :] = o
        hn_ref[rows, :] = ((o * _rms(o)) * gn_ref[...] * scale1 + shift).astype(BF16)


def _outproj(hml, y, x, mod, g, g_next, w, *, tm):
    t = x.shape[0]
    rows_per_mod = t // mod.shape[0]
    row_block = pl.BlockSpec((tm, D_MODEL), lambda i: (i, 0))
    return pl.pallas_call(
        _outproj_kernel,
        grid=(t // tm,),
        in_specs=[pl.BlockSpec((tm, ML_W), lambda i: (i, 0)),
                  pl.BlockSpec((tm, SSM_W), lambda i: (i, 0)),
                  row_block,
                  pl.BlockSpec((None, N_MOD, D_MODEL), lambda i: ((i * tm) // rows_per_mod, 0, 0)),
                  pl.BlockSpec((1, D_MODEL), lambda i: (0, 0)),
                  pl.BlockSpec((1, D_MODEL), lambda i: (0, 0)),
                  pl.BlockSpec((ML_W + SSM_W, D_MODEL), lambda i: (0, 0))],
        out_specs=[row_block, row_block],
        out_shape=[jax.ShapeDtypeStruct((t, D_MODEL), F32), jax.ShapeDtypeStruct((t, D_MODEL), BF16)],
        compiler_params=_cparams(1),
        name="outproj",
    )(hml, y, x, mod, g, g_next, w)


def _prepare_params(norm_g, w_in, gate_bias, dt_bias, a_log, d_skip, conv_w, conv_b, ml_norm_g, ssm_norm_g,
                    w_out, w_gate, w_up, w_down):
    return dict(
        g=[norm_g[i].reshape(1, D_MODEL) for i in range(6)],
        g_first=jnp.stack([norm_g[0], norm_g[2]], axis=0),
        w_in_t=jnp.swapaxes(w_in, 0, 1),
        bi_row=_lane_block(gate_bias[0].reshape(1, -1)), bf_row=_lane_block(gate_bias[1].reshape(1, -1)),
        bd_row=_lane_block(dt_bias.reshape(1, -1)), a_row=_lane_block(-jnp.exp(a_log.reshape(1, -1))),
        dsk=jnp.repeat(d_skip, SSM_P).reshape(1, SSM_W),
        conv_w=conv_w, conv_b=conv_b.reshape(1, CONV_CH),
        mlg=ml_norm_g.reshape(1, ML_W), ng=ssm_norm_g.reshape(1, SSM_W),
        w_out=w_out,
        ffn1_w=(_cast_bf16(w_gate, 0), _cast_bf16(w_up, 0), _cast_bf16(w_down, 0)),
        ffn_w_f32=(w_gate, w_up, w_down),
    )


def _trunk_path(x, mod, p, ml_state, ssm_state, layer, seg_len, shared_w=None, *,
                tm_ffn=512, tf=512, tm_proj=512, tn_proj=3328):
    bsz, seq, _ = x.shape
    t = bsz * seq
    x = x.reshape(t, D_MODEL)
    cast_next = None if shared_w is not None else p['ffn_w_f32'] + (1, p['w_in_t'], p['w_out'])
    res = _ffn(x, None, mod, p['g_first'], p['g'][1], *p['ffn1_w'], idx=0, tm=tm_ffn, tf=tf, cast_next=cast_next)
    x, h = res[:2]
    if shared_w is None:
        shared_w = dict(ffn2=tuple(res[2:5]), w_main=res[5], w_small=res[6], w_out=res[7])
    p_main, small = _inproj(h, shared_w['w_main'], shared_w['w_small'], p['conv_w'], p['conv_b'],
                            seg_len=seg_len, tm=2 * tm_proj, tn=tn_proj)
    if ml_state is not None:
        c0, n0, m0 = ml_state
        m0 = m0[:, layer]
        pad = SMALL_W - 2 * ML_HEADS
        m0 = jnp.stack([jnp.pad(m0[:, 0], ((0, 0), (0, pad + ML_HEADS))),
                        jnp.pad(m0[:, 1], ((0, 0), (ML_HEADS, pad)))], axis=1)
        m0 = jnp.pad(m0, ((0, 0), (0, M_ROWS - 2), (0, 0)))
        ml_state = (c0, n0, m0)
        ssm_state = ssm_state.reshape(ssm_state.shape[:3] + (SSM_W, SSM_N))
    hml, new_c, new_n, new_m = _mlstm(p_main, small, p['bi_row'], p['bf_row'], p['mlg'], ml_state, layer,
                                      bsz=bsz, seq=seq)
    y, new_s = _ssd(p_main, small, p['bd_row'], p['a_row'], p['dsk'], p['ng'], ssm_state, layer, bsz=bsz, seq=seq)
    x, h = _outproj(hml.reshape(t, ML_W), y.reshape(t, SSM_W), x, mod, p['g'][3], p['g'][4], shared_w['w_out'],
                    tm=tm_proj)
    x = _ffn(x, h, mod, p['g'][4], p['g'][5], *shared_w['ffn2'], idx=1, tm=tm_ffn, tf=tf)
    new_m = new_m[:, :2, :2 * ML_HEADS].reshape(bsz, 2, 2, ML_HEADS)
    new_m = jnp.sum(jnp.where(jnp.eye(2, dtype=bool)[None, :, :, None], new_m, 0.0), axis=2)
    states = (new_c, new_n, new_m, new_s.reshape(bsz, 2, SSM_HEADS, SSM_P, SSM_N))
    return x.reshape(bsz, seq, D_MODEL), states, shared_w


def _stack_layers(parts):
    return parts[0][:, None] if len(parts) == 1 else jnp.stack(parts, axis=1)


def kernel(x_prompt, x_sample, state_mlstm_C, state_mlstm_n, state_mlstm_m, state_ssm, c, c_ctx, w_ada, b_ada,
           norm_g, w_in, gate_bias, dt_bias, a_log, d_skip, conv_w, conv_b, ml_norm_g, ssm_norm_g, w_out,
           ffn_w_gate, ffn_w_up, ffn_w_down):
    depth = w_in.shape[0]
    bd = x_sample.shape[0]
    y_p, y_s = x_prompt, x_sample
    new_states = []
    cvec = jnp.concatenate([c_ctx[None], c, jnp.zeros((8 - 1 - bd, D_MODEL), F32)], axis=0)
    for l in range(depth):
        p = _prepare_params(norm_g[l], w_in[l], gate_bias[l], dt_bias[l], a_log[l], d_skip[l], conv_w[l], conv_b[l],
                            ml_norm_g[l], ssm_norm_g[l], w_out[l], ffn_w_gate[l], ffn_w_up[l], ffn_w_down[l])
        mod = _ada(cvec, w_ada[l], b_ada[l].reshape(1, -1)).reshape(8, N_MOD, D_MODEL)
        y_p, states, shared_w = _trunk_path(y_p, mod[0:1], p, None, None, l, x_prompt.shape[1])
        lat_state = (state_mlstm_C, state_mlstm_n, state_mlstm_m)
        y_s, _, _ = _trunk_path(y_s, mod[1:1 + bd], p, lat_state, state_ssm, l, GRID_W, shared_w)
        new_states.append(states)
    return (y_p, y_s) + tuple(_stack_layers([st[k] for st in new_states]) for k in range(4))
```
